```python
import math
import jax
import jax.numpy as jnp
from jax import lax
import numpy as np

D_MODEL = 1024
BATCH = 8
SEQ = 4096
DEPTH = 2

CTX_LEN = 256
GRID_W = 64
Q_BLOCK = 128
ROPE_BASE = 10000.0
HEAD_DIM = 64
N_BRANCH = 4
BRANCH_W = D_MODEL // N_BRANCH

MLA_HEADS = 4
MLA_Q_LORA = D_MODEL // 4
MLA_KV_LORA = D_MODEL // 8
MLA_NOPE = 64
MLA_ROPE = 32
MLA_V = 64
MLA_SCALE = (MLA_NOPE + MLA_ROPE) ** -0.5

RWKV_HEADS = 4
RWKV_N = 64
RWKV_DECAY_LORA = 64
RWKV_AAA_LORA = 64
RWKV_GN_EPS = 64e-5
RWKV_DECAY_SCALE = math.exp(-0.5)
RWKV_SPLITS = (BRANCH_W, BRANCH_W, BRANCH_W,
               RWKV_DECAY_LORA, RWKV_DECAY_LORA, RWKV_AAA_LORA, RWKV_AAA_LORA)
RWKV_SHIFT_W = sum(RWKV_SPLITS)

GQA_Q_HEADS = 4
GQA_KV_HEADS = 2
GQA_SCALE = HEAD_DIM ** -0.5

DIFF_HEADS = 4
DIFF_D = 32
DIFF_V = 64
DIFF_SCALE = DIFF_D ** -0.5

ALPHA = (2 * DEPTH) ** 0.25
BETA = (8 * DEPTH) ** -0.25

IN_SPLITS = (
    MLA_Q_LORA, MLA_KV_LORA, MLA_ROPE, BRANCH_W,
    RWKV_SHIFT_W, BRANCH_W,
    GQA_Q_HEADS * HEAD_DIM, GQA_KV_HEADS * HEAD_DIM, GQA_KV_HEADS * HEAD_DIM, BRANCH_W,
    DIFF_HEADS * 2 * DIFF_D, DIFF_HEADS * 2 * DIFF_D, DIFF_HEADS * DIFF_V, BRANCH_W,
    N_BRANCH * D_MODEL,
)
IN_W = sum(IN_SPLITS)

kernel_name = "hybrid_mla_rwkv7_gqa_diffattn_dit"


def split_cols(y, widths):
    out, o = [], 0
    for w in widths:
        out.append(y[..., o:o + w])
        o += w
    return out


def flat_heads(o):
    return o.reshape(o.shape[0], o.shape[1], -1)


def layer_norm(x, eps=1e-6):
    xf = x.astype(jnp.float32)
    mu = jnp.mean(xf, axis=-1, keepdims=True)
    var = jnp.mean(jnp.square(xf - mu), axis=-1, keepdims=True)
    return ((xf - mu) * lax.rsqrt(var + eps)).astype(x.dtype)


def post_norm(h, g, b):
    return layer_norm(h, 1e-5) * g + b


def rms_norm(x, g, eps):
    xf = x.astype(jnp.float32)
    return (xf * lax.rsqrt(jnp.mean(jnp.square(xf), axis=-1, keepdims=True) + eps)).astype(x.dtype) * g


def l2_normalize(t):
    tf = t.astype(jnp.float32)
    return (tf * lax.rsqrt(jnp.sum(jnp.square(tf), axis=-1, keepdims=True) + 1e-12)).astype(t.dtype)


def group_norm_heads(y, w, b):
    mu = jnp.mean(y, axis=-1, keepdims=True)
    var = jnp.mean(jnp.square(y - mu), axis=-1, keepdims=True)
    return ((y - mu) * lax.rsqrt(var + RWKV_GN_EPS)).astype(w.dtype) * w + b


def modulate(x, shift, scale):
    return layer_norm(x) * (1.0 + scale) + shift


def axial_rope_tables(row, col, rot_dim):
    quarter = rot_dim // 4
    inv_freq = ROPE_BASE ** (-jnp.arange(quarter, dtype=jnp.float32) / quarter)
    ang = jnp.concatenate([row[:, None] * inv_freq, col[:, None] * inv_freq], axis=-1)
    return jnp.cos(ang), jnp.sin(ang)


def apply_rope(t, cos, sin):
    half = t.shape[-1] // 2
    c = cos[:, None, :].astype(t.dtype)
    s = sin[:, None, :].astype(t.dtype)
    t1, t2 = t[..., :half], t[..., half:]
    return jnp.concatenate([t1 * c - t2 * s, t2 * c + t1 * s], axis=-1)


def sweep_query_blocks(fn, *qs):
    B, L = qs[0].shape[:2]
    nb = L // Q_BLOCK
    blocks = tuple(jnp.moveaxis(q.reshape(B, nb, Q_BLOCK, *q.shape[2:]), 1, 0) for q in qs)
    out = lax.map(lambda qb: fn(*qb), blocks)
    return jnp.moveaxis(out, 0, 1).reshape(B, L, *out.shape[3:])


def softmax_attention(q, k, v, scale):
    Hq, Hk = q.shape[2], k.shape[2]
    G = Hq // Hk

    def block(qb):
        Bq, Q = qb.shape[:2]
        qg = qb.reshape(Bq, Q, Hk, G, qb.shape[-1])
        s = jnp.einsum('bqhgd,bkhd->bhgqk', qg, k).astype(jnp.float32) * scale
        p = jax.nn.softmax(s, axis=-1).astype(v.dtype)
        o = jnp.einsum('bhgqk,bkhe->bqhge', p, v)
        return o.reshape(Bq, Q, Hq, v.shape[-1])

    return sweep_query_blocks(block, q)


def differential_attention(q1, q2, k1, k2, v, lam):
    def block(q1b, q2b):
        s1 = jnp.einsum('bqhd,bkhd->bhqk', q1b, k1).astype(jnp.float32) * DIFF_SCALE
        s2 = jnp.einsum('bqhd,bkhd->bhqk', q2b, k2).astype(jnp.float32) * DIFF_SCALE
        p = jax.nn.softmax(s1, axis=-1) - lam * jax.nn.softmax(s2, axis=-1)
        return jnp.einsum('bhqk,bkhe->bqhe', p.astype(v.dtype), v)

    return sweep_query_blocks(block, q1, q2)


def mla_project(q_lat, kv_lat, k_rope, q_norm, w_uq, kv_norm, w_ukv, rope):
    B, L = q_lat.shape[:2]
    q = (rms_norm(q_lat, q_norm, 1e-6) @ w_uq).reshape(B, L, MLA_HEADS, MLA_NOPE + MLA_ROPE)
    kv = (rms_norm(kv_lat, kv_norm, 1e-6) @ w_ukv).reshape(B, L, MLA_HEADS, MLA_NOPE + MLA_V)
    q_nope, q_pe = q[..., :MLA_NOPE], q[..., MLA_NOPE:]
    k_nope, v = kv[..., :MLA_NOPE], kv[..., MLA_NOPE:]
    k_pe = k_rope[:, :, None, :]
    if rope is not None:
        q_pe = apply_rope(q_pe, *rope)
        k_pe = apply_rope(k_pe, *rope)
    q = jnp.concatenate([q_nope, q_pe], axis=-1)
    k = jnp.concatenate([k_nope, jnp.broadcast_to(k_pe, (B, L, MLA_HEADS, MLA_ROPE))], axis=-1)
    return q, k, v


def gqa_project(q, k, v, q_norm, k_norm, rope):
    B, L = q.shape[:2]
    q = rms_norm(q.reshape(B, L, GQA_Q_HEADS, HEAD_DIM), q_norm, 1e-6)
    k = rms_norm(k.reshape(B, L, GQA_KV_HEADS, HEAD_DIM), k_norm, 1e-6)
    v = v.reshape(B, L, GQA_KV_HEADS, HEAD_DIM)
    if rope is not None:
        q = apply_rope(q, *rope)
        k = apply_rope(k, *rope)
    return q, k, v


def diff_project(q, k, v, rope):
    B, L = q.shape[:2]
    q = q.reshape(B, L, DIFF_HEADS, 2, DIFF_D)
    k = k.reshape(B, L, DIFF_HEADS, 2, DIFF_D)
    v = v.reshape(B, L, DIFF_HEADS, DIFF_V)
    q1, q2, k1, k2 = q[..., 0, :], q[..., 1, :], k[..., 0, :], k[..., 1, :]
    if rope is not None:
        q1, q2, k1, k2 = (apply_rope(t, *rope) for t in (q1, q2, k1, k2))
    return q1, q2, k1, k2, v


def centred_shift(t, mu_prev, mu_next):
    zero = jnp.zeros_like(t[:, :1])
    t_prev = jnp.concatenate([zero, t[:, :-1]], axis=1)
    t_next = jnp.concatenate([t[:, 1:], zero], axis=1)
    return t + mu_prev * (t_prev - t) + mu_next * (t_next - t)


def rwkv_prepare(sh, mu_prev, mu_next, k_k):
    sh = centred_shift(sh, mu_prev, mu_next)
    r, k, v, wd_f, wd_b, ad_f, ad_b = split_cols(sh, RWKV_SPLITS)
    B, L = sh.shape[:2]
    r, k, v = (t.reshape(B, L, RWKV_HEADS, RWKV_N) for t in (r, k, v))
    kk = l2_normalize(k * k_k)
    return r, k, v, kk, (wd_f, wd_b), (ad_f, ad_b)


def rwkv_direction_inputs(r, k, v, wd, ad, w0, w_up, a0, a_up, k_a, r_k):
    shape = r.shape
    z = (w0 + jnp.tanh(wd) @ w_up).astype(jnp.float32)
    w = jnp.exp(-RWKV_DECAY_SCALE * jax.nn.sigmoid(z)).reshape(shape)
    a = jax.nn.sigmoid(a0 + ad @ a_up).reshape(shape)
    k_dir = k * (1.0 + (a - 1.0) * k_a)
    bonus = jnp.sum(r * k_dir * r_k, axis=-1, keepdims=True) * v
    return w, k_dir, a, bonus


def rwkv_scan(state0, r, w, k, v, kk, a, reverse):
    xs = tuple(jnp.moveaxis(t.astype(jnp.float32), 1, 0) for t in (r, w, k, v, kk, a))

    def step(S, inp):
        r_t, w_t, k_t, v_t, kk_t, a_t = inp
        S = (S * w_t[:, :, None, :]
             - jnp.einsum('bhvk,bhk->bhv', S, kk_t)[..., None] * (kk_t * a_t)[:, :, None, :]
             + v_t[..., None] * k_t[:, :, None, :])
        return S, jnp.einsum('bhvk,bhk->bhv', S, r_t)

    S_final, ys = lax.scan(step, state0, xs, reverse=reverse)
    return jnp.moveaxis(ys, 0, 1), S_final


def rwkv_branch(sh_l, sh_c, mu, w0, w_up, a0, a_up, k_k, k_a, r_k, gn_w, gn_b, update_ctx):
    hd = (RWKV_HEADS, RWKV_N)
    k_k, k_a, r_k, gn_w, gn_b = (t.reshape(hd) for t in (k_k, k_a, r_k, gn_w, gn_b))
    lat = rwkv_prepare(sh_l, mu[0], mu[1], k_k)
    cx = rwkv_prepare(sh_c, mu[0], mu[1], k_k)
    B = sh_l.shape[0]
    ys_lat, bon_lat, ys_ctx, bon_ctx = [], [], [], []
    for d in range(2):
        reverse = d == 1
        w_c, kd_c, a_c, b_c = rwkv_direction_inputs(cx[0], cx[1], cx[2], cx[4][d], cx[5][d],
                                                    w0[d], w_up[d], a0[d], a_up[d], k_a, r_k)
        state0 = jnp.zeros((B, RWKV_HEADS, RWKV_N, RWKV_N), jnp.float32)
        y_c, state_c = rwkv_scan(state0, cx[0], w_c, kd_c, cx[2], cx[3], a_c, reverse)
        w_l, kd_l, a_l, b_l = rwkv_direction_inputs(lat[0], lat[1], lat[2], lat[4][d], lat[5][d],
                                                    w0[d], w_up[d], a0[d], a_up[d], k_a, r_k)
        y_l, _ = rwkv_scan(state_c, lat[0], w_l, kd_l, lat[2], lat[3], a_l, reverse)
        ys_lat.append(y_l)
        bon_lat.append(b_l)
        ys_ctx.append(y_c)
        bon_ctx.append(b_c)
    o_l = group_norm_heads(ys_lat[0] + ys_lat[1], gn_w, gn_b) + bon_lat[0] + bon_lat[1]
    o_c = None
    if update_ctx:
        o_c = flat_heads(group_norm_heads(ys_ctx[0] + ys_ctx[1], gn_w, gn_b) + bon_ctx[0] + bon_ctx[1])
    return flat_heads(o_l), o_c


def merge_branches(outs, gate_logits, merge_b, branch_w, out_w):
    g = jax.nn.sigmoid(gate_logits + merge_b)
    y = g[..., :D_MODEL] * (outs[0] @ branch_w[0])
    for i in range(1, N_BRANCH):
        y = y + g[..., i * D_MODEL:(i + 1) * D_MODEL] * (outs[i] @ branch_w[i])
    return y @ out_w


def hybrid_layer(x, ctx, c, c_ctx, ada_w, ada_b, in_w, mla_q_norm, mla_w_uq, mla_kv_norm, mla_w_ukv,
                 rwkv_mu, rwkv_w0, rwkv_w_up, rwkv_a0, rwkv_a_up, rwkv_k_k, rwkv_k_a, rwkv_r_k,
                 rwkv_gn_w, rwkv_gn_b, gqa_q_norm, gqa_k_norm, diff_lambda, diff_subln,
                 merge_b, branch_w, out_w, ln_g, ln_b, lambda_init, rope32, rope64, update_ctx):
    silu = jax.nn.silu
    shift_l, scale_l, gate_l = jnp.split((silu(c) @ ada_w + ada_b)[:, None, :], 3, axis=-1)
    shift_c, scale_c, gate_c = jnp.split(silu(c_ctx) @ ada_w + ada_b, 3, axis=-1)
    pl = split_cols(modulate(x, shift_l, scale_l) @ in_w, IN_SPLITS)
    pc = split_cols(modulate(ctx, shift_c, scale_c) @ in_w, IN_SPLITS)

    qa_l, ka_l, va_l = mla_project(pl[0], pl[1], pl[2], mla_q_norm, mla_w_uq, mla_kv_norm, mla_w_ukv, rope32)
    qa_c, ka_c, va_c = mla_project(pc[0], pc[1], pc[2], mla_q_norm, mla_w_uq, mla_kv_norm, mla_w_ukv, None)
    oa_l = softmax_attention(qa_l, jnp.concatenate([ka_c, ka_l], 1), jnp.concatenate([va_c, va_l], 1), MLA_SCALE)

    ob_l, ob_c = rwkv_branch(pl[4], pc[4], rwkv_mu, rwkv_w0, rwkv_w_up, rwkv_a0, rwkv_a_up,
                             rwkv_k_k, rwkv_k_a, rwkv_r_k, rwkv_gn_w, rwkv_gn_b, update_ctx)

    qc_l, kc_l, vc_l = gqa_project(pl[6], pl[7], pl[8], gqa_q_norm, gqa_k_norm, rope64)
    qc_c, kc_c, vc_c = gqa_project(pc[6], pc[7], pc[8], gqa_q_norm, gqa_k_norm, None)
    oc_l = softmax_attention(qc_l, jnp.concatenate([kc_c, kc_l], 1), jnp.concatenate([vc_c, vc_l], 1), GQA_SCALE)

    lam = (jnp.exp(jnp.sum(diff_lambda[0] * diff_lambda[1]).astype(jnp.float32))
           - jnp.exp(jnp.sum(diff_lambda[2] * diff_lambda[3]).astype(jnp.float32)) + lambda_init)
    q1_l, q2_l, k1_l, k2_l, vd_l = diff_project(pl[10], pl[11], pl[12], rope32)
    q1_c, q2_c, k1_c, k2_c, vd_c = diff_project(pc[10], pc[11], pc[12], None)
    od_l = differential_attention(q1_l, q2_l, jnp.concatenate([k1_c, k1_l], 1), jnp.concatenate([k2_c, k2_l], 1),
                                  jnp.concatenate([vd_c, vd_l], 1), lam)
    od_l = rms_norm(od_l, diff_subln, 1e-5) * (1.0 - lambda_init)

    outs_l = [flat_heads(oa_l) * silu(pl[3]), ob_l * silu(pl[5]),
              flat_heads(oc_l) * silu(pl[9]), flat_heads(od_l) * silu(pl[13])]
    x_new = post_norm(ALPHA * x + gate_l * merge_branches(outs_l, pl[14], merge_b, branch_w, out_w), ln_g, ln_b)

    ctx_new = ctx
    if update_ctx:
        oa_c = softmax_attention(qa_c, ka_c, va_c, MLA_SCALE)
        oc_c = softmax_attention(qc_c, kc_c, vc_c, GQA_SCALE)
        od_c = rms_norm(differential_attention(q1_c, q2_c, k1_c, k2_c, vd_c, lam), diff_subln, 1e-5) * (1.0 - lambda_init)
        outs_c = [flat_heads(oa_c) * silu(pc[3]), ob_c * silu(pc[5]),
                  flat_heads(oc_c) * silu(pc[9]), flat_heads(od_c) * silu(pc[13])]
        ctx_new = post_norm(ALPHA * ctx + gate_c * merge_branches(outs_c, pc[14], merge_b, branch_w, out_w), ln_g, ln_b)
    return x_new, ctx_new


def setup_inputs(seed: int = 0) -> dict:
    key = jax.random.key(seed)
    ks = iter(jax.random.split(key, 40))

    def nrm(shape, scale):
        return scale * jax.random.normal(next(ks), shape, jnp.float32)

    def gain(shape):
        return 1.0 + nrm(shape, 0.05)

    return {
        "x": nrm((BATCH, SEQ, D_MODEL), 1.0),
        "c": nrm((BATCH, D_MODEL), 1.0),
        "ctx": nrm((BATCH, CTX_LEN, D_MODEL), 1.0),
        "c_ctx": nrm((D_MODEL,), 1.0),
        "ada_w": nrm((DEPTH, D_MODEL, 3 * D_MODEL), D_MODEL ** -0.5),
        "ada_b": nrm((DEPTH, 3 * D_MODEL), 0.02),
        "in_w": nrm((DEPTH, D_MODEL, IN_W), D_MODEL ** -0.5),
        "mla_q_norm": gain((DEPTH, MLA_Q_LORA)),
        "mla_w_uq": nrm((DEPTH, MLA_Q_LORA, MLA_HEADS * (MLA_NOPE + MLA_ROPE)), MLA_Q_LORA ** -0.5),
        "mla_kv_norm": gain((DEPTH, MLA_KV_LORA)),
        "mla_w_ukv": nrm((DEPTH, MLA_KV_LORA, MLA_HEADS * (MLA_NOPE + MLA_V)), MLA_KV_LORA ** -0.5),
        "rwkv_mu": jax.random.uniform(next(ks), (DEPTH, 2, RWKV_SHIFT_W), jnp.float32, 0.0, 0.5),
        "rwkv_w0": nrm((DEPTH, 2, BRANCH_W), 0.5),
        "rwkv_w_up": nrm((DEPTH, 2, RWKV_DECAY_LORA, BRANCH_W), RWKV_DECAY_LORA ** -0.5),
        "rwkv_a0": nrm((DEPTH, 2, BRANCH_W), 0.5),
        "rwkv_a_up": nrm((DEPTH, 2, RWKV_AAA_LORA, BRANCH_W), RWKV_AAA_LORA ** -0.5),
        "rwkv_k_k": 0.85 + nrm((DEPTH, BRANCH_W), 0.05),
        "rwkv_k_a": gain((DEPTH, BRANCH_W)),
        "rwkv_r_k": nrm((DEPTH, BRANCH_W), 0.1),
        "rwkv_gn_w": gain((DEPTH, BRANCH_W)),
        "rwkv_gn_b": nrm((DEPTH, BRANCH_W), 0.02),
        "gqa_q_norm": gain((DEPTH, HEAD_DIM)),
        "gqa_k_norm": gain((DEPTH, HEAD_DIM)),
        "diff_lambda": nrm((DEPTH, 4, DIFF_D), 0.1),
        "diff_subln": gain((DEPTH, DIFF_V)),
        "merge_b": nrm((DEPTH, N_BRANCH * D_MODEL), 0.02),
        "branch_w": nrm((DEPTH, N_BRANCH, BRANCH_W, D_MODEL), BETA * BRANCH_W ** -0.5),
        "out_w": nrm((DEPTH, D_MODEL, D_MODEL), BETA * D_MODEL ** -0.5),
        "ln_g": gain((DEPTH, D_MODEL)),
        "ln_b": nrm((DEPTH, D_MODEL), 0.02),
    }


def reference(x, c, ctx, c_ctx, ada_w, ada_b, in_w, mla_q_norm, mla_w_uq, mla_kv_norm, mla_w_ukv,
              rwkv_mu, rwkv_w0, rwkv_w_up, rwkv_a0, rwkv_a_up, rwkv_k_k, rwkv_k_a, rwkv_r_k,
              rwkv_gn_w, rwkv_gn_b, gqa_q_norm, gqa_k_norm, diff_lambda, diff_subln,
              merge_b, branch_w, out_w, ln_g, ln_b):
    L = x.shape[1]
    rows = L // GRID_W
    row = jnp.repeat(jnp.arange(rows), GRID_W).astype(jnp.float32)
    col = jnp.tile(jnp.arange(GRID_W), rows).astype(jnp.float32)
    rope32 = axial_rope_tables(row, col, MLA_ROPE)
    rope64 = axial_rope_tables(row, col, HEAD_DIM)
    for l in range(DEPTH):
        x, ctx = hybrid_layer(
            x, ctx, c, c_ctx, ada_w[l], ada_b[l], in_w[l], mla_q_norm[l], mla_w_uq[l], mla_kv_norm[l], mla_w_ukv[l],
            rwkv_mu[l], rwkv_w0[l], rwkv_w_up[l], rwkv_a0[l], rwkv_a_up[l], rwkv_k_k[l], rwkv_k_a[l], rwkv_r_k[l],
            rwkv_gn_w[l], rwkv_gn_b[l], gqa_q_norm[l], gqa_k_norm[l], diff_lambda[l], diff_subln[l],
            merge_b[l], branch_w[l], out_w[l], ln_g[l], ln_b[l],
            lambda_init=0.8 - 0.6 * math.exp(-0.3 * l), rope32=rope32, rope64=rope64,
            update_ctx=l < DEPTH - 1)
    return x
```

```python
import functools
import math

import jax
import jax.numpy as jnp
from jax import lax
from jax.experimental import pallas as pl
from jax.experimental.pallas import tpu as pltpu

F32 = jnp.float32
BF16 = jnp.bfloat16

D_MODEL = 1024
GRID_W = 64
ROPE_BASE = 10000.0
HEAD_DIM = 64
N_BRANCH = 4
BRANCH_W = 256
MLA_HEADS = 4
MLA_Q_LORA = 256
MLA_KV_LORA = 128
MLA_NOPE = 64
MLA_ROPE = 32
MLA_V = 64
MLA_SCALE = (MLA_NOPE + MLA_ROPE) ** -0.5
RWKV_N = 64
RWKV_GN_EPS = 64e-5
RWKV_DECAY_SCALE = math.exp(-0.5)
GQA_SCALE = HEAD_DIM ** -0.5
DIFF_D = 32
DIFF_SCALE = DIFF_D ** -0.5

LANES = 128
VMEM_LIMIT = 56 * 1024 * 1024

P_AQ, P_AKV, P_AKR = 0, 256, 384
P_CQ, P_CK, P_CV = 512, 768, 896
P_DQ, P_DK, P_DV = 1024, 1280, 1536
P_RWKV = 2048
P_SILU = 3072
P_MERGE = 4096
P_WIDTH = 8192
ATTN_W = 2048

RWKV_CHUNK = 64


def _cparams(sem):
    return pltpu.CompilerParams(dimension_semantics=sem, vmem_limit_bytes=VMEM_LIMIT)


def _split3(x):
    h = x.astype(BF16)
    r = x - h.astype(F32)
    m = r.astype(BF16)
    l = (r - m.astype(F32)).astype(BF16)
    return h, m, l


def _dot(a, b):
    return jnp.dot(a, b, preferred_element_type=F32)


def _dot_hi(a, b):
    ah, am, _ = _split3(a)
    bh, bm, _ = _split3(b)
    return _dot(ah, bh) + (_dot(ah, bm) + _dot(am, bh))


def _dot_exact_lhs(a_bf16, b):
    bh, bm, bl = _split3(b)
    return _dot(a_bf16, bh) + (_dot(a_bf16, bm) + _dot(a_bf16, bl))


def _iota(shape, dim):
    return lax.broadcasted_iota(jnp.int32, shape, dim)


def _ada_kernel(c_ref, w_ref, b_ref, o_ref):
    c = c_ref[...]
    s = c * jax.nn.sigmoid(c)
    o_ref[0] = _dot_hi(s, w_ref[0]) + b_ref[0]


def _ada_call(cvec, ada_w, ada_b):
    depth = ada_w.shape[0]
    rows = cvec.shape[0]
    nblk = 3
    return pl.pallas_call(
        _ada_kernel,
        out_shape=jax.ShapeDtypeStruct((depth, rows, 3 * D_MODEL), F32),
        grid=(depth, nblk),
        in_specs=[
            pl.BlockSpec((rows, D_MODEL), lambda l, j: (0, 0)),
            pl.BlockSpec((1, D_MODEL, D_MODEL), lambda l, j: (l, 0, j)),
            pl.BlockSpec((1, 1, D_MODEL), lambda l, j: (l, 0, j)),
        ],
        out_specs=pl.BlockSpec((1, rows, D_MODEL), lambda l, j: (l, 0, j)),
        compiler_params=_cparams(("arbitrary", "arbitrary")),
        name="ada",
    )(cvec, ada_w, ada_b.reshape(depth, 1, 3 * D_MODEL))


def _inproj_kernel(x_ref, mod_ref, w_ref, o_ref, xm_ref, *, lc, tm, nb):
    b = pl.program_id(0)
    i = pl.program_id(1)
    j = pl.program_id(2)

    @pl.when(j == 0)
    def _():
        x = x_ref[0]
        mu = jnp.mean(x, -1, keepdims=True)
        xc = x - mu
        var = jnp.mean(xc * xc, -1, keepdims=True)
        xn = xc * lax.rsqrt(var + 1e-6)
        row = i * tm + _iota((tm, D_MODEL), 0)
        is_ctx = row < lc
        ml = mod_ref[pl.ds(b, 1), :]
        mc = mod_ref[pl.ds(nb, 1), :]
        shift = jnp.where(is_ctx, mc[:, 0:D_MODEL], ml[:, 0:D_MODEL])
        scale = jnp.where(is_ctx, mc[:, D_MODEL:2 * D_MODEL], ml[:, D_MODEL:2 * D_MODEL])
        xm_ref[...] = (xn * (1.0 + scale) + shift).astype(BF16)

    o_ref[0] = _dot(xm_ref[...], w_ref[...]).astype(BF16)


def _inproj_call(xx, mod, w_packed, lc):
    nb, t, _ = xx.shape
    tm = _pick_tile(t, 1088, 16)
    tn = 1024
    kern = functools.partial(_inproj_kernel, lc=lc, tm=tm, nb=nb)
    return pl.pallas_call(
        kern,
        out_shape=jax.ShapeDtypeStruct((nb, t, P_WIDTH), BF16),
        grid=(nb, t // tm, P_WIDTH // tn),
        in_specs=[
            pl.BlockSpec((1, tm, D_MODEL), lambda b, i, j: (b, i, 0)),
            pl.BlockSpec(mod.shape, lambda b, i, j: (0, 0)),
            pl.BlockSpec((D_MODEL, tn), lambda b, i, j: (0, j)),
        ],
        out_specs=pl.BlockSpec((1, tm, tn), lambda b, i, j: (b, i, j)),
        scratch_shapes=[pltpu.VMEM((tm, D_MODEL), BF16)],
        compiler_params=_cparams(("arbitrary", "arbitrary", "arbitrary")),
        name="inproj",
    )(xx, mod, w_packed)


def _pick_tile(n, cap, mult):
    best = None
    for d in range(mult, min(n, cap) + 1, mult):
        if n % d == 0:
            best = d
    assert best is not None, (n, cap, mult)
    return best


def _swap_halves(x, half):
    n = x.shape[-1]
    first = (_iota(x.shape, 1) & (2 * half - 1)) < half
    up = pltpu.roll(x, n - half, 1)
    dn = pltpu.roll(x, half, 1)
    return jnp.where(first, up, dn)


def _half_sums(x):
    lo = _iota(x.shape, 1) < HEAD_DIM
    s_lo = jnp.sum(jnp.where(lo, x, 0.0), -1, keepdims=True)
    s_hi = jnp.sum(jnp.where(lo, 0.0, x), -1, keepdims=True)
    return jnp.where(lo, s_lo, s_hi)


def _prep_kernel(p_ref, cosa_ref, sina_ref, cosc_ref, sinc_ref, cosd_ref, sind_ref,
                 qn_ref, wuq_ref, kvn_ref, wk_ref, wv_ref, gq_ref, gk_ref,
                 qa_ref, ka_ref, va_ref, qc_ref, kc_ref, vc_ref, qd_ref, kd_ref, vd_ref):
    def seg(off, w):
        return p_ref[0, :, off:off + w].astype(F32)

    lane = _iota((p_ref.shape[1], LANES), 1)
    lo = lane < HEAD_DIM

    cosa, sina = cosa_ref[...], sina_ref[...]
    ql = seg(P_AQ, MLA_Q_LORA)
    ql = ql * lax.rsqrt(jnp.mean(ql * ql, -1, keepdims=True) + 1e-6) * qn_ref[...]
    q = _dot(ql.astype(BF16), wuq_ref[...])
    kvl = seg(P_AKV, MLA_KV_LORA)
    kvl = (kvl * lax.rsqrt(jnp.mean(kvl * kvl, -1, keepdims=True) + 1e-6) * kvn_ref[...]).astype(BF16)
    kn = _dot(kvl, wk_ref[...])
    vv = _dot(kvl, wv_ref[...])
    kr = seg(P_AKR, LANES)
    kr = kr * cosa + _swap_halves(kr, MLA_ROPE // 2) * sina
    for h in range(MLA_HEADS):
        qh = q[:, h * LANES:(h + 1) * LANES]
        qh = qh * cosa + _swap_halves(qh, MLA_ROPE // 2) * sina
        qa_ref[0, h] = (qh * MLA_SCALE).astype(BF16)
        ka_ref[0, h] = (kn[:, h * LANES:(h + 1) * LANES] + kr).astype(BF16)
        va_ref[0, h] = vv[:, h * LANES:(h + 1) * LANES].astype(BF16)

    cosc, sinc = cosc_ref[...], sinc_ref[...]

    def norm_rope(x, g):
        x = x * lax.rsqrt(_half_sums(x * x) * (1.0 / HEAD_DIM) + 1e-6) * g
        return x * cosc + _swap_halves(x, HEAD_DIM // 2) * sinc

    def split_heads(blk):
        return jnp.where(lo, blk, 0.0), jnp.where(lo, pltpu.roll(blk, HEAD_DIM, 1), 0.0)

    for pr in range(2):
        qb = norm_rope(seg(P_CQ + pr * LANES, LANES), gq_ref[...]) * GQA_SCALE
        q0, q1 = split_heads(qb)
        qc_ref[0, 2 * pr] = q0.astype(BF16)
        qc_ref[0, 2 * pr + 1] = q1.astype(BF16)
    k0, k1 = split_heads(norm_rope(seg(P_CK, LANES), gk_ref[...]))
    kc_ref[0, 0] = k0.astype(BF16)
    kc_ref[0, 1] = k1.astype(BF16)
    vb = seg(P_CV, LANES)
    v_g0 = jnp.where(lo, vb, 0.0)
    v_g1 = jnp.where(lo, 0.0, vb)
    vc_ref[0, 0] = v_g0.astype(BF16)
    vc_ref[0, 1] = pltpu.roll(v_g0, HEAD_DIM, 1).astype(BF16)
    vc_ref[0, 2] = pltpu.roll(v_g1, HEAD_DIM, 1).astype(BF16)
    vc_ref[0, 3] = v_g1.astype(BF16)

    cosd, sind = cosd_ref[...], sind_ref[...]
    piece = lane < DIFF_D
    for pr in range(2):
        qb = seg(P_DQ + pr * LANES, LANES)
        qb = (qb * cosd + _swap_halves(qb, DIFF_D // 2) * sind) * DIFF_SCALE
        kb = seg(P_DK + pr * LANES, LANES)
        kb = kb * cosd + _swap_halves(kb, DIFF_D // 2) * sind
        vb = seg(P_DV + pr * LANES, LANES)
        for hh in range(2):
            for m in range(2):
                off = hh * HEAD_DIM + m * DIFF_D
                idx = (2 * pr + hh) * 2 + m
                qs = qb if off == 0 else pltpu.roll(qb, LANES - off, 1)
                ks = kb if off == 0 else pltpu.roll(kb, LANES - off, 1)
                qd_ref[0, idx] = jnp.where(piece, qs, 0.0).astype(BF16)
                kd_ref[0, idx] = jnp.where(piece, ks, 0.0).astype(BF16)
        vd_ref[0, 2 * pr] = jnp.where(lo, vb, 0.0).astype(BF16)
        vd_ref[0, 2 * pr + 1] = jnp.where(lo, 0.0, vb).astype(BF16)


def _prep_call(p, tabs, wts):
    nb, t, _ = p.shape
    tm = _pick_tile(t, 256, 16)
    row_spec = pl.BlockSpec((tm, LANES), lambda b, i: (i, 0))

    def full(a):
        return pl.BlockSpec(a.shape, lambda b, i: (0,) * a.ndim)

    def head_out(nh):
        return (jax.ShapeDtypeStruct((nb, nh, t, LANES), BF16),
                pl.BlockSpec((1, nh, tm, LANES), lambda b, i: (b, 0, i, 0)))

    outs = [head_out(n) for n in (4, 4, 4, 4, 2, 4, 8, 8, 4)]
    return pl.pallas_call(
        _prep_kernel,
        out_shape=[o[0] for o in outs],
        grid=(nb, t // tm),
        in_specs=[pl.BlockSpec((1, tm, ATTN_W), lambda b, i: (b, i, 0))]
        + [row_spec] * 6 + [full(w) for w in wts],
        out_specs=[o[1] for o in outs],
        compiler_params=_cparams(("arbitrary", "arbitrary")),
        name="attn_prep",
    )(p, *tabs, *wts)


def _attn_kernel(*refs, hp, n_maps, shared_k, lk, tk, lam_init):
    if n_maps == 2:
        q_ref, k_ref, v_ref, lam_ref, sub_ref, o_ref, s_scr = refs
    else:
        q_ref, k_ref, v_ref, o_ref, s_scr = refs
    tq = q_ref.shape[2]
    nchunks = lk // tk
    nfold = tk // LANES

    def fold(x, op):
        r = x[:, 0:LANES]
        for u in range(1, nfold):
            r = op(r, x[:, u * LANES:(u + 1) * LANES])
        return r

    def one_map(qi, ki, vi):
        q = q_ref[0, qi]

        def sweep1(c, mx):
            kc = k_ref[0, ki, pl.ds(pl.multiple_of(c * tk, tk), tk), :]
            s = lax.dot_general(q, kc, (((1,), (1,)), ((), ())), preferred_element_type=F32)
            s_scr[c] = s
            return jnp.maximum(mx, fold(s, jnp.maximum))

        mx = lax.fori_loop(0, nchunks, sweep1, jnp.full((tq, LANES), -jnp.inf, F32))
        m = jnp.max(mx, -1, keepdims=True)

        def sweep2(c, carry):
            ls, acc = carry
            p = jnp.exp(s_scr[c] - m)
            vc = v_ref[0, vi, pl.ds(pl.multiple_of(c * tk, tk), tk), :]
            return ls + fold(p, jnp.add), acc + _dot(p.astype(BF16), vc)

        zero = jnp.zeros((tq, LANES), F32)
        ls, acc = lax.fori_loop(0, nchunks, sweep2, (zero, zero))
        return acc / jnp.sum(ls, -1, keepdims=True)

    total = None
    for hh in range(hp):
        ki = 0 if shared_k else hh * n_maps
        o = one_map(hh * n_maps, ki, hh)
        if n_maps == 2:
            lp = lam_ref[...]
            lam = (jnp.exp(jnp.sum(lp[0:1] * lp[1:2], keepdims=True))
                   - jnp.exp(jnp.sum(lp[2:3] * lp[3:4], keepdims=True)) + lam_init)
            o = o - lam * one_map(hh * n_maps + 1, ki + 1, hh)
            ms = jnp.sum(o * o, -1, keepdims=True) * (1.0 / HEAD_DIM)
            o = o * lax.rsqrt(ms + 1e-5) * sub_ref[...] * (1.0 - lam_init)
        total = o if total is None else total + o
    o_ref[0, 0] = total


def _attn_call(q, k, v, *, n_maps, shared_k, lk, q_off, nq_rows, extra=(), lam_init=0.0):
    nb = q.shape[0]
    t = q_off + nq_rows
    hp = 2
    tq = _pick_tile(math.gcd(nq_rows, q_off) if q_off else nq_rows, 256, 16)
    tk = 256 if lk % 256 == 0 else _pick_tile(lk, 256, LANES)
    qo = q_off // tq
    kh = 1 if shared_k else hp * n_maps
    kern = functools.partial(_attn_kernel, hp=hp, n_maps=n_maps, shared_k=shared_k, lk=lk, tk=tk,
                             lam_init=lam_init)
    in_specs = [
        pl.BlockSpec((1, hp * n_maps, tq, LANES), lambda b, p, i: (b, p, i + qo, 0)),
        pl.BlockSpec((1, kh, lk, LANES), lambda b, p, i: (b, p, 0, 0)),
        pl.BlockSpec((1, hp, lk, LANES), lambda b, p, i: (b, p, 0, 0)),
    ] + [pl.BlockSpec(e.shape, lambda b, p, i: (0, 0)) for e in extra]
    return pl.pallas_call(
        kern,
        out_shape=jax.ShapeDtypeStruct((nb, 2, t, LANES), F32),
        grid=(nb, 2, nq_rows // tq),
        in_specs=in_specs,
        out_specs=pl.BlockSpec((1, 1, tq, LANES), lambda b, p, i: (b, p, i + qo, 0)),
        scratch_shapes=[pltpu.VMEM((lk // tk, tq, tk), F32)],
        compiler_params=_cparams(("arbitrary", "arbitrary", "arbitrary")),
        name="attn",
    )(q, k, v, *extra)


def _rwkv_prep_kernel(cur_ref, prv_ref, nxt_ref, mu_ref, wl_ref, bias_ref, kk_ref, ka_ref, rk_ref,
                      sh_ref, dr_ref, bon_ref, *, lc, t_total, tm, halo):
    i = pl.program_id(1)
    cur = cur_ref[0]
    r_i = _iota((tm, tm), 0)
    c_i = _iota((tm, tm), 1)
    s_dn = jnp.where(c_i == r_i - 1, 1.0, 0.0).astype(BF16)
    s_up = jnp.where(c_i == r_i + 1, 1.0, 0.0).astype(BF16)
    row = _iota((tm, D_MODEL), 0)
    tg = i * tm + row
    prv = jnp.where(row == 0, prv_ref[0].astype(F32)[halo - 1:halo, :], _dot(s_dn, cur))
    nxt = jnp.where(row == tm - 1, nxt_ref[0].astype(F32)[0:1, :], _dot(s_up, cur))
    prv = jnp.where((tg == 0) | (tg == lc), 0.0, prv)
    nxt = jnp.where((tg == lc - 1) | (tg == t_total - 1), 0.0, nxt)
    x = cur.astype(F32)
    sh = x + mu_ref[0:1, :] * (prv - x) + mu_ref[1:2, :] * (nxt - x)

    r = sh[:, 0:256]
    k = sh[:, 256:512]
    v = sh[:, 512:768]
    lora = sh[:, 768:1024]
    lora = jnp.where(_iota(lora.shape, 1) < 2 * RWKV_N, jnp.tanh(lora), lora)
    z = _dot(lora.astype(BF16), wl_ref[...]) + bias_ref[...]

    def per_head(fn, x):
        return jnp.concatenate([fn(x[:, 0:LANES]), fn(x[:, LANES:2 * LANES])], axis=1)

    kq = k * kk_ref[...]
    kk = kq * lax.rsqrt(per_head(_half_sums, kq * kq) + 1e-12)
    sh_ref[0, :, 0:256] = r
    sh_ref[0, :, 256:512] = v
    sh_ref[0, :, 512:768] = kk
    hs = None
    for d in range(2):
        logw = -RWKV_DECAY_SCALE * jax.nn.sigmoid(z[:, d * 256:(d + 1) * 256])
        a = jax.nn.sigmoid(z[:, 512 + d * 256:512 + (d + 1) * 256])
        kd = k * (1.0 + (a - 1.0) * ka_ref[...])
        dr_ref[0, d, :, 0:256] = logw
        dr_ref[0, d, :, 256:512] = kd
        dr_ref[0, d, :, 512:768] = a * kk
        s = per_head(_half_sums, r * kd * rk_ref[...])
        hs = s if hs is None else hs + s
    bon_ref[0] = hs * v


def _rwkv_prep_call(p, mu, wl, bias, k_k, k_a, r_k, lc):
    nb, t, _ = p.shape
    tm = _pick_tile(t, 256, 16)
    halo = 16
    cb = P_RWKV // D_MODEL
    hb = tm // halo
    last = t // halo - 1
    kern = functools.partial(_rwkv_prep_kernel, lc=lc, t_total=t, tm=tm, halo=halo)

    def full(a):
        return pl.BlockSpec(a.shape, lambda b, i: (0,) * a.ndim)

    return pl.pallas_call(
        kern,
        out_shape=[jax.ShapeDtypeStruct((nb, t, 768), F32),
                   jax.ShapeDtypeStruct((nb, 2, t, 768), F32),
                   jax.ShapeDtypeStruct((nb, t, 256), F32)],
        grid=(nb, t // tm),
        in_specs=[
            pl.BlockSpec((1, tm, D_MODEL), lambda b, i: (b, i, cb)),
            pl.BlockSpec((1, halo, D_MODEL), lambda b, i: (b, jnp.maximum(i * hb - 1, 0), cb)),
            pl.BlockSpec((1, halo, D_MODEL), lambda b, i: (b, jnp.minimum((i + 1) * hb, last), cb)),
            full(mu), full(wl), full(bias), full(k_k), full(k_a), full(r_k),
        ],
        out_specs=[pl.BlockSpec((1, tm, 768), lambda b, i: (b, i, 0)),
                   pl.BlockSpec((1, 2, tm, 768), lambda b, i: (b, 0, i, 0)),
                   pl.BlockSpec((1, tm, 256), lambda b, i: (b, i, 0))],
        compiler_params=_cparams(("arbitrary", "arbitrary")),
        name="rwkv_prep",
    )(p, p, p, mu, wl, bias, k_k, k_a, r_k)


def _expand(x, bd):
    return jnp.where(bd, jnp.concatenate([x] * 4, axis=0), 0.0)


def _contract(x):
    c = x.shape[0] // 4
    return (x[0:c] + x[c:2 * c]) + (x[2 * c:3 * c] + x[3 * c:4 * c])


def _rwkv_local_kernel(sh_ref, dr_ref, g_ref, hh_ref, rh_ref, yh_ref, *, nck):
    C = RWKV_CHUNK
    n = 4 * C
    d = pl.program_id(1)
    rr = _iota((n, n), 0)
    cc = _iota((n, n), 1)
    bd = (rr >> 6) == (cc >> 6)
    tr = rr & (C - 1)
    tc = cc & (C - 1)
    sgn = jnp.where(d == 0, 1, -1)
    dlt = (tc - tr) * sgn
    strict = bd & (dlt < 0)
    incl = bd & (dlt <= 0)
    r1 = _iota((C, C), 0)
    c1 = _iota((C, C), 1)
    incl1 = jnp.where((c1 - r1) * sgn <= 0, 1.0, 0.0).astype(BF16)
    eye = rr == cc

    for ck in range(nck):
        rows = slice(ck * C, (ck + 1) * C)
        r = sh_ref[0, rows, 0:256]
        v = sh_ref[0, rows, 256:512]
        kk = sh_ref[0, rows, 512:768]
        lw = dr_ref[0, 0, rows, 0:256]
        kd = dr_ref[0, 0, rows, 256:512]
        b = dr_ref[0, 0, rows, 512:768]

        lp = _dot_exact_lhs(incl1, lw)
        ltot = jnp.sum(lw, 0, keepdims=True)
        e_neg = jnp.exp(-lp)
        kap = kk * jnp.exp(lp - lw)
        rt = r * jnp.exp(lp)
        bt = b * e_neg
        kt = kd * e_neg
        e_rem = jnp.exp(ltot - lp)
        bh = b * e_rem
        kh = kd * e_rem

        kap_e = _expand(kap, bd)
        rt_e = _expand(rt, bd)
        v_e = _expand(v, bd)
        lhs = jnp.concatenate([kap_e, rt_e], axis=0).astype(BF16)
        rhs = jnp.concatenate([bt] * 4 + [kt] * 4, axis=0).astype(BF16)
        gram = lax.dot_general(lhs, rhs, (((1,), (1,)), ((), ())), preferred_element_type=F32)
        mb = jnp.where(strict, gram[0:n, 0:n], 0.0)
        mk = jnp.where(strict, gram[0:n, n:2 * n], 0.0)
        mrb = jnp.where(incl, gram[n:2 * n, 0:n], 0.0)
        mrk = jnp.where(incl, gram[n:2 * n, n:2 * n], 0.0)

        inv = jnp.where(eye, 1.0, 0.0) - jnp.where((rr >> 1) == (cc >> 1), mb, 0.0)
        sft = 1
        while (1 << sft) < C:
            off = ((rr >> sft) != (cc >> sft)) & ((rr >> (sft + 1)) == (cc >> (sft + 1)))
            ck_m = jnp.where(off, mb, 0.0).astype(BF16)
            inv_b = inv.astype(BF16)
            inv = inv - _dot(_dot(inv_b, ck_m).astype(BF16), inv_b)
            sft += 1

        mkv = _dot(mk.astype(BF16), v_e.astype(BF16))
        wu = _dot(inv.astype(BF16), jnp.concatenate([kap_e, mkv], axis=1).astype(BF16))
        wu_b = wu.astype(BF16)
        mw = _dot(mrb.astype(BF16), wu_b)
        rh = rt - _contract(mw[:, 0:n])
        yh = _contract(_dot(mrk.astype(BF16), v_e.astype(BF16)) - mw[:, n:2 * n])
        bw = _dot(_expand(bh, bd).T.astype(BF16), wu_b)
        kv = _dot(_expand(kh, bd).T.astype(BF16), v_e.astype(BF16))
        g = jnp.where(eye, jnp.exp(ltot), 0.0) - bw[:, 0:n]
        hh = kv - bw[:, n:2 * n]

        g_ref[0, 0, ck] = _contract(g)
        hh_ref[0, 0, ck] = _contract(hh)
        rh_ref[0, 0, rows, :] = rh
        yh_ref[0, 0, rows, :] = yh


def _rwkv_local_call(shared, dirs):
    nb, t, _ = shared.shape
    C = RWKV_CHUNK
    nch = t // C
    nck = 2 if nch % 2 == 0 else 1
    tm = nck * C
    kern = functools.partial(_rwkv_local_kernel, nck=nck)
    mat = jax.ShapeDtypeStruct((nb, 2, nch, C, 256), F32)
    seq = jax.ShapeDtypeStruct((nb, 2, t, 256), F32)
    mat_spec = pl.BlockSpec((1, 1, nck, C, 256), lambda b, d, i: (b, d, i, 0, 0))
    seq_spec = pl.BlockSpec((1, 1, tm, 256), lambda b, d, i: (b, d, i, 0))
    return pl.pallas_call(
        kern,
        out_shape=[mat, mat, seq, seq],
        grid=(nb, 2, nch // nck),
        in_specs=[pl.BlockSpec((1, tm, 768), lambda b, d, i: (b, i, 0)),
                  pl.BlockSpec((1, 1, tm, 768), lambda b, d, i: (b, d, i, 0))],
        out_specs=[mat_spec, mat_spec, seq_spec, seq_spec],
        compiler_params=_cparams(("arbitrary", "arbitrary", "arbitrary")),
        name="rwkv_local",
    )(shared, dirs)


def _rwkv_scan_kernel(gf_ref, hf_ref, rf_ref, yf_ref, gb_ref, hb_ref, rb_ref, yb_ref,
                      of_ref, ob_ref, st_ref, *, nb):
    C = RWKV_CHUNK
    n = 4 * C
    j = pl.program_id(0)

    @pl.when(j == 0)
    def _():
        st_ref[...] = jnp.zeros(st_ref.shape, F32)

    rr = _iota((n, n), 0)
    cc = _iota((n, n), 1)
    bd = (rr >> 6) == (cc >> 6)

    def step(d, b, g_ref, h_ref, r_ref, y_ref, o_ref):
        sh_, sm_, _ = _split3(st_ref[d, b])
        rh_, rm_, _ = _split3(_expand(r_ref[b, 0], bd))
        y = _dot(rh_, sh_) + (_dot(rh_, sm_) + _dot(rm_, sh_))
        o_ref[b] = _contract(y) + y_ref[b, 0]
        gh_, gm_, _ = _split3(_expand(g_ref[b, 0, 0], bd))
        new = _dot(gh_, sh_) + (_dot(gh_, sm_) + _dot(gm_, sh_))
        st_ref[d, b] = new + _expand(h_ref[b, 0, 0], bd)

    def body(b, carry):
        step(0, b, gf_ref, hf_ref, rf_ref, yf_ref, of_ref)
        step(1, b, gb_ref, hb_ref, rb_ref, yb_ref, ob_ref)
        return carry

    lax.fori_loop(0, nb, body, 0)


def _rwkv_scan_call(g, hh, rh, yh, lc):
    nb, _, nch, C, _ = g.shape
    t = nch * C
    ncc = lc // C

    def bwd(j):
        return jnp.where(j < ncc, ncc - 1 - j, nch - 1 + ncc - j)

    def mat_spec(d):
        if d == 0:
            return pl.BlockSpec((nb, 1, 1, C, 256), lambda j: (0, 0, j, 0, 0))
        return pl.BlockSpec((nb, 1, 1, C, 256), lambda j: (0, 1, bwd(j), 0, 0))

    def seq_spec(d):
        if d == 0:
            return pl.BlockSpec((nb, 1, C, 256), lambda j: (0, 0, j, 0))
        return pl.BlockSpec((nb, 1, C, 256), lambda j: (0, 1, bwd(j), 0))

    kern = functools.partial(_rwkv_scan_kernel, nb=nb)
    out = jax.ShapeDtypeStruct((nb, t, 256), F32)
    specs = []
    for d in range(2):
        specs += [mat_spec(d), mat_spec(d), seq_spec(d), seq_spec(d)]
    of, ob = pl.pallas_call(
        kern,
        out_shape=[out, out],
        grid=(nch,),
        in_specs=specs,
        out_specs=[pl.BlockSpec((nb, C, 256), lambda j: (0, j, 0)),
                   pl.BlockSpec((nb, C, 256), lambda j: (0, bwd(j), 0))],
        scratch_shapes=[pltpu.VMEM((2, nb, 4 * C, 256), F32)],
        compiler_params=_cparams(("arbitrary",)),
        name="rwkv_scan",
    )(g, hh, rh, yh, g, hh, rh, yh)
    return of, ob


def _merge_kernel(x_ref, mod_ref, oa_ref, oc_ref, od_ref, yf_ref, yb_ref, bon_ref, sg_ref, mg_ref,
                  bw_ref, ow_ref, mb_ref, gnw_ref, gnb_ref, lng_ref, lnb_ref, o_ref,
                  *, lc, tm, nb, row_off, alpha):
    b = pl.program_id(0)
    i = pl.program_id(1)

    def pair_cat(ref):
        return jnp.concatenate([ref[0, 0], ref[0, 1]], axis=1)

    def per_head(fn, x):
        return jnp.concatenate([fn(x[:, 0:LANES]), fn(x[:, LANES:2 * LANES])], axis=1)

    y = yf_ref[0] + yb_ref[0]
    mu = per_head(_half_sums, y) * (1.0 / RWKV_N)
    yc = y - mu
    var = per_head(_half_sums, yc * yc) * (1.0 / RWKV_N)
    ob = yc * lax.rsqrt(var + RWKV_GN_EPS) * gnw_ref[...] + gnb_ref[...] + bon_ref[0]

    outs = (pair_cat(oa_ref), ob, pair_cat(oc_ref), pair_cat(od_ref))
    acc = None
    for k in range(N_BRANCH):
        sg = sg_ref[0, :, k * BRANCH_W:(k + 1) * BRANCH_W].astype(F32)
        u = outs[k] * (sg * jax.nn.sigmoid(sg))
        z = _dot(u.astype(BF16), bw_ref[k])
        gate = jax.nn.sigmoid(mg_ref[0, :, k * D_MODEL:(k + 1) * D_MODEL].astype(F32)
                              + mb_ref[:, k * D_MODEL:(k + 1) * D_MODEL])
        acc = gate * z if acc is None else acc + gate * z
    y2 = _dot(acc.astype(BF16), ow_ref[...])

    row = row_off + i * tm + _iota((tm, D_MODEL), 0)
    gl = mod_ref[pl.ds(b, 1), 2 * D_MODEL:3 * D_MODEL]
    gc = mod_ref[pl.ds(nb, 1), 2 * D_MODEL:3 * D_MODEL]
    h = alpha * x_ref[0] + jnp.where(row < lc, gc, gl) * y2
    m = jnp.mean(h, -1, keepdims=True)
    hc = h - m
    var = jnp.mean(hc * hc, -1, keepdims=True)
    o_ref[0] = hc * lax.rsqrt(var + 1e-5) * lng_ref[...] + lnb_ref[...]


def _merge_call(xx, mod, oa, oc, od, yf, yb, bon, p, bw, ow, mb, gnw, gnb, lng, lnb, *, lc, row_off, alpha):
    nb, t, _ = xx.shape
    nrows = t - row_off
    tm = _pick_tile(math.gcd(nrows, row_off) if row_off else nrows, 256, 16)
    ro = row_off // tm
    kern = functools.partial(_merge_kernel, lc=lc, tm=tm, nb=nb, row_off=row_off, alpha=alpha)

    def full(a):
        return pl.BlockSpec(a.shape, lambda b, i: (0,) * a.ndim)

    pair_spec = pl.BlockSpec((1, 2, tm, LANES), lambda b, i: (b, 0, i + ro, 0))
    return pl.pallas_call(
        kern,
        out_shape=jax.ShapeDtypeStruct((nb, nrows, D_MODEL), F32),
        grid=(nb, nrows // tm),
        in_specs=[
            pl.BlockSpec((1, tm, D_MODEL), lambda b, i: (b, i + ro, 0)),
            full(mod), pair_spec, pair_spec, pair_spec,
            pl.BlockSpec((1, tm, 256), lambda b, i: (b, i + ro, 0)),
            pl.BlockSpec((1, tm, 256), lambda b, i: (b, i + ro, 0)),
            pl.BlockSpec((1, tm, 256), lambda b, i: (b, i + ro, 0)),
            pl.BlockSpec((1, tm, D_MODEL), lambda b, i: (b, i + ro, P_SILU // D_MODEL)),
            pl.BlockSpec((1, tm, N_BRANCH * D_MODEL), lambda b, i: (b, i + ro, P_MERGE // (N_BRANCH * D_MODEL))),
            full(bw), full(ow), full(mb), full(gnw), full(gnb), full(lng), full(lnb),
        ],
        out_specs=pl.BlockSpec((1, tm, D_MODEL), lambda b, i: (b, i, 0)),
        compiler_params=_cparams(("arbitrary", "arbitrary")),
        name="merge",
    )(xx, mod, oa, oc, od, yf, yb, bon, p, p, bw, ow, mb, gnw, gnb, lng, lnb)


def _pack_in_w(w):
    z = lambda n: jnp.zeros((D_MODEL, n), w.dtype)
    a0 = 0
    b0 = 672
    c0 = b0 + 1280
    d0 = c0 + 768
    m0 = d0 + 1024
    cols = [
        w[:, a0:a0 + 256], w[:, a0 + 256:a0 + 384],
        z(64), w[:, a0 + 384:a0 + 416], z(32),
        w[:, c0:c0 + 256], w[:, c0 + 256:c0 + 384], w[:, c0 + 384:c0 + 512],
        w[:, d0:d0 + 256], w[:, d0 + 256:d0 + 512], w[:, d0 + 512:d0 + 768],
        z(256),
        w[:, b0:b0 + 1024],
        w[:, a0 + 416:a0 + 672], w[:, b0 + 1024:b0 + 1280], w[:, c0 + 512:c0 + 768], w[:, d0 + 768:d0 + 1024],
        w[:, m0:m0 + 4096],
    ]
    out = jnp.concatenate(cols, axis=1)
    assert out.shape[1] == P_WIDTH
    return out.astype(BF16)


def _pack_mla(w_uq, w_ukv):
    zq = jnp.zeros((MLA_Q_LORA, LANES - MLA_NOPE - MLA_ROPE), w_uq.dtype)
    zk = jnp.zeros((MLA_KV_LORA, LANES - MLA_NOPE), w_ukv.dtype)
    zv = jnp.zeros((MLA_KV_LORA, MLA_V), w_ukv.dtype)
    qc, kc, vc = [], [], []
    for h in range(MLA_HEADS):
        qc += [w_uq[:, h * 96:(h + 1) * 96], zq]
        kc += [w_ukv[:, h * 128:h * 128 + 64], zk]
        vh = w_ukv[:, h * 128 + 64:(h + 1) * 128]
        vc += [vh, zv] if h % 2 == 0 else [zv, vh]
    cat = lambda xs: jnp.concatenate(xs, axis=1).astype(BF16)
    return cat(qc), cat(kc), cat(vc)


def _rope_tables(row, col, rot_dim, lc, pattern):
    quarter = rot_dim // 4
    inv_freq = ROPE_BASE ** (-jnp.arange(quarter, dtype=F32) / quarter)
    ang = jnp.concatenate([row[:, None] * inv_freq, col[:, None] * inv_freq], axis=-1)
    cos, sin = jnp.cos(ang), jnp.sin(ang)
    n = ang.shape[0]
    ones = lambda w: jnp.ones((n, w), F32)
    zeros = lambda w: jnp.zeros((n, w), F32)
    if pattern == "mla":
        c = jnp.concatenate([ones(64), cos, cos, ones(32)], axis=1)
        s = jnp.concatenate([zeros(64), -sin, sin, zeros(32)], axis=1)
    else:
        reps = LANES // rot_dim
        c = jnp.concatenate([cos, cos] * reps, axis=1)
        s = jnp.concatenate([-sin, sin] * reps, axis=1)
    c = jnp.concatenate([jnp.ones((lc, LANES), F32), c], axis=0)
    s = jnp.concatenate([jnp.zeros((lc, LANES), F32), s], axis=0)
    return c, s


def kernel(x, c, ctx, c_ctx, ada_w, ada_b, in_w, mla_q_norm, mla_w_uq, mla_kv_norm, mla_w_ukv, rwkv_mu, rwkv_w0, rwkv_w_up, rwkv_a0, rwkv_a_up, rwkv_k_k, rwkv_k_a, rwkv_r_k, rwkv_gn_w, rwkv_gn_b, gqa_q_norm, gqa_k_norm, diff_lambda, diff_subln, merge_b, branch_w, out_w, ln_g, ln_b):
    nb, ll, _ = x.shape
    lc = ctx.shape[1]
    depth = ada_w.shape[0]
    t = lc + ll
    alpha = (2 * depth) ** 0.25

    rows = ll // GRID_W
    row = jnp.repeat(jnp.arange(rows), GRID_W).astype(F32)
    col = jnp.tile(jnp.arange(GRID_W), rows).astype(F32)
    tabs = (_rope_tables(row, col, MLA_ROPE, lc, "mla") + _rope_tables(row, col, HEAD_DIM, lc, "tile")
            + _rope_tables(row, col, DIFF_D, lc, "tile"))

    crows = -(-(nb + 1) // 8) * 8
    cvec = jnp.concatenate([c, c_ctx[None, :], jnp.zeros((crows - nb - 1, D_MODEL), F32)], axis=0)
    mods = _ada_call(cvec, ada_w, ada_b)

    xx = jnp.concatenate([ctx, x], axis=1)
    tile2 = lambda a: jnp.concatenate([a, a])[None, :]
    for l in range(depth):
        last = l == depth - 1
        lam_init = 0.8 - 0.6 * math.exp(-0.3 * l)
        mod = mods[l]
        p = _inproj_call(xx, mod, _pack_in_w(in_w[l]), lc)

        wuq, wk, wv = _pack_mla(mla_w_uq[l], mla_w_ukv[l])
        qa, ka, va, qc, kc, vc, qd, kd, vd = _prep_call(
            p, tabs, (mla_q_norm[l][None, :], wuq, mla_kv_norm[l][None, :], wk, wv,
                      tile2(gqa_q_norm[l]), tile2(gqa_k_norm[l])))
        lam_p = jnp.zeros((8, LANES), F32).at[0:4, 0:DIFF_D].set(diff_lambda[l])
        sub = tile2(diff_subln[l])
        att = lambda q, k, v, **kw: _attn_call(q, k, v, lk=t, q_off=lc, nq_rows=ll, **kw)
        oa = att(qa, ka, va, n_maps=1, shared_k=False)
        oc = att(qc, kc, vc, n_maps=1, shared_k=True)
        od = att(qd, kd, vd, n_maps=2, shared_k=False, extra=(lam_p, sub), lam_init=lam_init)
        if not last:
            catt = lambda q, k, v, **kw: _attn_call(q, k, v, lk=lc, q_off=0, nq_rows=lc, **kw)
            merge_rows = lambda full_o, ctx_o: lax.dynamic_update_slice(full_o, ctx_o, (0, 0, 0, 0))
            oa = merge_rows(oa, catt(qa, ka, va, n_maps=1, shared_k=False))
            oc = merge_rows(oc, catt(qc, kc, vc, n_maps=1, shared_k=True))
            od = merge_rows(od, catt(qd, kd, vd, n_maps=2, shared_k=False, extra=(lam_p, sub), lam_init=lam_init))

        wl = jnp.zeros((256, 1024), F32)
        for d in range(2):
            wl = wl.at[d * 64:(d + 1) * 64, d * 256:(d + 1) * 256].set(rwkv_w_up[l, d])
            wl = wl.at[128 + d * 64:128 + (d + 1) * 64, 512 + d * 256:512 + (d + 1) * 256].set(rwkv_a_up[l, d])
        bias = jnp.concatenate([rwkv_w0[l, 0], rwkv_w0[l, 1], rwkv_a0[l, 0], rwkv_a0[l, 1]])[None, :]
        shared, dirs, bon = _rwkv_prep_call(p, rwkv_mu[l], wl.astype(BF16), bias, rwkv_k_k[l][None, :],
                                            rwkv_k_a[l][None, :], rwkv_r_k[l][None, :], lc)
        g, hh, rh, yh = _rwkv_local_call(shared, dirs)
        yf, yb = _rwkv_scan_call(g, hh, rh, yh, lc)

        xx = _merge_call(xx, mod, oa, oc, od, yf, yb, bon, p,
                         branch_w[l].astype(BF16), out_w[l].astype(BF16), merge_b[l][None, :],
                         rwkv_gn_w[l][None, :], rwkv_gn_b[l][None, :], ln_g[l][None, :], ln_b[l][None, :],
                         lc=lc, row_off=lc if last else 0, alpha=alpha)
    return xx
```

```python
import functools
import math

import jax
import jax.numpy as jnp
import numpy as np
from jax import lax
from jax.experimental import pallas as pl
from jax.experimental.pallas import tpu as pltpu

F32 = jnp.float32
BF16 = jnp.bfloat16

D_MODEL = 1024
GRID_W = 64
ROPE_BASE = 10000.0
HEAD_DIM = 64
N_BRANCH = 4
BRANCH_W = 256
MLA_HEADS = 4
MLA_Q_LORA = 256
MLA_KV_LORA = 128
MLA_NOPE = 64
MLA_ROPE = 32
MLA_V = 64
MLA_SCALE = (MLA_NOPE + MLA_ROPE) ** -0.5
RWKV_N = 64
RWKV_GN_EPS = 64e-5
RWKV_DECAY_SCALE = math.exp(-0.5)
GQA_SCALE = HEAD_DIM ** -0.5
DIFF_D = 32
DIFF_SCALE = DIFF_D ** -0.5

LANES = 128
VMEM_LIMIT = 56 * 1024 * 1024

P_AQ, P_AKV, P_AKR = 0, 256, 384
P_CQ, P_CK, P_CV = 512, 768, 896
P_DQ, P_DK, P_DV = 1024, 1280, 1536
P_RWKV = 2048
P_SILU = 3072
P_MERGE = 4096
P_WIDTH = 8192
ATTN_W = 2048

RWKV_CHUNK = 64


def _cparams(sem):
    return pltpu.CompilerParams(dimension_semantics=sem, vmem_limit_bytes=VMEM_LIMIT)


def _split3(x):
    h = x.astype(BF16)
    r = x - h.astype(F32)
    m = r.astype(BF16)
    l = (r - m.astype(F32)).astype(BF16)
    return h, m, l


def _dot(a, b):
    return jnp.dot(a, b, preferred_element_type=F32)


def _dot_hi(a, b):
    ah, am, _ = _split3(a)
    bh, bm, _ = _split3(b)
    return _dot(ah, bh) + (_dot(ah, bm) + _dot(am, bh))


def _dot_exact_lhs(a_bf16, b):
    bh, bm, bl = _split3(b)
    return _dot(a_bf16, bh) + (_dot(a_bf16, bm) + _dot(a_bf16, bl))


def _iota(shape, dim):
    return lax.broadcasted_iota(jnp.int32, shape, dim)


def _ada_kernel(c_ref, w_ref, b_ref, o_ref):
    c = c_ref[...]
    s = c * jax.nn.sigmoid(c)
    o_ref[0] = _dot_hi(s, w_ref[0]) + b_ref[0]


def _ada_call(cvec, ada_w, ada_b):
    depth = ada_w.shape[0]
    rows = cvec.shape[0]
    nblk = 3
    return pl.pallas_call(
        _ada_kernel,
        out_shape=jax.ShapeDtypeStruct((depth, rows, 3 * D_MODEL), F32),
        grid=(depth, nblk),
        in_specs=[
            pl.BlockSpec((rows, D_MODEL), lambda l, j: (0, 0)),
            pl.BlockSpec((1, D_MODEL, D_MODEL), lambda l, j: (l, 0, j)),
            pl.BlockSpec((1, 1, D_MODEL), lambda l, j: (l, 0, j)),
        ],
        out_specs=pl.BlockSpec((1, rows, D_MODEL), lambda l, j: (l, 0, j)),
        compiler_params=_cparams(("arbitrary", "arbitrary")),
        name="ada",
    )(cvec, ada_w, ada_b.reshape(depth, 1, 3 * D_MODEL))


def _inproj_kernel(x_ref, mod_ref, w_ref, o_ref, xm_ref, *, lc, tm, nb):
    b = pl.program_id(0)
    i = pl.program_id(1)
    j = pl.program_id(2)

    @pl.when(j == 0)
    def _():
        x = x_ref[0]
        mu = jnp.mean(x, -1, keepdims=True)
        xc = x - mu
        var = jnp.mean(xc * xc, -1, keepdims=True)
        xn = xc * lax.rsqrt(var + 1e-6)
        row = i * tm + _iota((tm, D_MODEL), 0)
        is_ctx = row < lc
        ml = mod_ref[pl.ds(b, 1), :]
        mc = mod_ref[pl.ds(nb, 1), :]
        shift = jnp.where(is_ctx, mc[:, 0:D_MODEL], ml[:, 0:D_MODEL])
        scale = jnp.where(is_ctx, mc[:, D_MODEL:2 * D_MODEL], ml[:, D_MODEL:2 * D_MODEL])
        xm_ref[...] = (xn * (1.0 + scale) + shift).astype(BF16)

    o_ref[0] = _dot(xm_ref[...], w_ref[...]).astype(BF16)


def _inproj_call(xx, mod, w_packed, lc):
    nb, t, _ = xx.shape
    tm = _pick_tile(t, 1088, 16)
    tn = 1024
    kern = functools.partial(_inproj_kernel, lc=lc, tm=tm, nb=nb)
    return pl.pallas_call(
        kern,
        out_shape=jax.ShapeDtypeStruct((nb, t, P_WIDTH), BF16),
        grid=(nb, t // tm, P_WIDTH // tn),
        in_specs=[
            pl.BlockSpec((1, tm, D_MODEL), lambda b, i, j: (b, i, 0)),
            pl.BlockSpec(mod.shape, lambda b, i, j: (0, 0)),
            pl.BlockSpec((D_MODEL, tn), lambda b, i, j: (0, j)),
        ],
        out_specs=pl.BlockSpec((1, tm, tn), lambda b, i, j: (b, i, j)),
        scratch_shapes=[pltpu.VMEM((tm, D_MODEL), BF16)],
        compiler_params=_cparams(("arbitrary", "arbitrary", "arbitrary")),
        name="inproj",
    )(xx, mod, w_packed)


def _pick_tile(n, cap, mult):
    best = None
    for d in range(mult, min(n, cap) + 1, mult):
        if n % d == 0:
            best = d
    assert best is not None, (n, cap, mult)
    return best


def _swap_halves(x, half):
    n = x.shape[-1]
    first = (_iota(x.shape, 1) & (2 * half - 1)) < half
    up = pltpu.roll(x, n - half, 1)
    dn = pltpu.roll(x, half, 1)
    return jnp.where(first, up, dn)


def _half_sums(x):
    lo = _iota(x.shape, 1) < HEAD_DIM
    s_lo = jnp.sum(jnp.where(lo, x, 0.0), -1, keepdims=True)
    s_hi = jnp.sum(jnp.where(lo, 0.0, x), -1, keepdims=True)
    return jnp.where(lo, s_lo, s_hi)


def _prep_kernel(p_ref, cosa_ref, sina_ref, cosc_ref, sinc_ref, cosd_ref, sind_ref,
                 qn_ref, wuq_ref, kvn_ref, wk_ref, wv_ref, gq_ref, gk_ref,
                 qa_ref, ka_ref, va_ref, qc_ref, kc_ref, vc_ref, qd_ref, kd_ref, vd_ref):
    def seg(off, w):
        return p_ref[0, :, off:off + w].astype(F32)

    lane = _iota((p_ref.shape[1], LANES), 1)
    lo = lane < HEAD_DIM

    cosa, sina = cosa_ref[...], sina_ref[...]
    ql = seg(P_AQ, MLA_Q_LORA)
    ql = ql * lax.rsqrt(jnp.mean(ql * ql, -1, keepdims=True) + 1e-6) * qn_ref[...]
    q = _dot(ql.astype(BF16), wuq_ref[...])
    kvl = seg(P_AKV, MLA_KV_LORA)
    kvl = (kvl * lax.rsqrt(jnp.mean(kvl * kvl, -1, keepdims=True) + 1e-6) * kvn_ref[...]).astype(BF16)
    kn = _dot(kvl, wk_ref[...])
    vv = _dot(kvl, wv_ref[...])
    kr = seg(P_AKR, LANES)
    kr = kr * cosa + _swap_halves(kr, MLA_ROPE // 2) * sina
    for h in range(MLA_HEADS):
        qh = q[:, h * LANES:(h + 1) * LANES]
        qh = qh * cosa + _swap_halves(qh, MLA_ROPE // 2) * sina
        qa_ref[0, h] = (qh * MLA_SCALE).astype(BF16)
        ka_ref[0, h] = (kn[:, h * LANES:(h + 1) * LANES] + kr).astype(BF16)
        va_ref[0, h] = vv[:, h * LANES:(h + 1) * LANES].astype(BF16)

    cosc, sinc = cosc_ref[...], sinc_ref[...]

    def norm_rope(x, g):
        x = x * lax.rsqrt(_half_sums(x * x) * (1.0 / HEAD_DIM) + 1e-6) * g
        return x * cosc + _swap_halves(x, HEAD_DIM // 2) * sinc

    def split_heads(blk):
        return jnp.where(lo, blk, 0.0), jnp.where(lo, pltpu.roll(blk, HEAD_DIM, 1), 0.0)

    for pr in range(2):
        qb = norm_rope(seg(P_CQ + pr * LANES, LANES), gq_ref[...]) * GQA_SCALE
        q0, q1 = split_heads(qb)
        qc_ref[0, 2 * pr] = q0.astype(BF16)
        qc_ref[0, 2 * pr + 1] = q1.astype(BF16)
    k0, k1 = split_heads(norm_rope(seg(P_CK, LANES), gk_ref[...]))
    kc_ref[0, 0] = k0.astype(BF16)
    kc_ref[0, 1] = k1.astype(BF16)
    vb = seg(P_CV, LANES)
    v_g0 = jnp.where(lo, vb, 0.0)
    v_g1 = jnp.where(lo, 0.0, vb)
    vc_ref[0, 0] = v_g0.astype(BF16)
    vc_ref[0, 1] = pltpu.roll(v_g0, HEAD_DIM, 1).astype(BF16)
    vc_ref[0, 2] = pltpu.roll(v_g1, HEAD_DIM, 1).astype(BF16)
    vc_ref[0, 3] = v_g1.astype(BF16)

    cosd, sind = cosd_ref[...], sind_ref[...]
    piece = lane < DIFF_D
    for pr in range(2):
        qb = seg(P_DQ + pr * LANES, LANES)
        qb = (qb * cosd + _swap_halves(qb, DIFF_D // 2) * sind) * DIFF_SCALE
        kb = seg(P_DK + pr * LANES, LANES)
        kb = kb * cosd + _swap_halves(kb, DIFF_D // 2) * sind
        vb = seg(P_DV + pr * LANES, LANES)
        for hh in range(2):
            for m in range(2):
                off = hh * HEAD_DIM + m * DIFF_D
                idx = (2 * pr + hh) * 2 + m
                qs = qb if off == 0 else pltpu.roll(qb, LANES - off, 1)
                ks = kb if off == 0 else pltpu.roll(kb, LANES - off, 1)
                qd_ref[0, idx] = jnp.where(piece, qs, 0.0).astype(BF16)
                kd_ref[0, idx] = jnp.where(piece, ks, 0.0).astype(BF16)
        vd_ref[0, 2 * pr] = jnp.where(lo, vb, 0.0).astype(BF16)
        vd_ref[0, 2 * pr + 1] = jnp.where(lo, 0.0, vb).astype(BF16)


def _prep_call(p, tabs, wts):
    nb, t, _ = p.shape
    tm = _pick_tile(t, 256, 16)
    row_spec = pl.BlockSpec((tm, LANES), lambda b, i: (i, 0))

    def full(a):
        return pl.BlockSpec(a.shape, lambda b, i: (0,) * a.ndim)

    def head_out(nh):
        return (jax.ShapeDtypeStruct((nb, nh, t, LANES), BF16),
                pl.BlockSpec((1, nh, tm, LANES), lambda b, i: (b, 0, i, 0)))

    outs = [head_out(n) for n in (4, 4, 4, 4, 2, 4, 8, 8, 4)]
    return pl.pallas_call(
        _prep_kernel,
        out_shape=[o[0] for o in outs],
        grid=(nb, t // tm),
        in_specs=[pl.BlockSpec((1, tm, ATTN_W), lambda b, i: (b, i, 0))]
        + [row_spec] * 6 + [full(w) for w in wts],
        out_specs=[o[1] for o in outs],
        compiler_params=_cparams(("arbitrary", "arbitrary")),
        name="attn_prep",
    )(p, *tabs, *wts)


def _attn_kernel(*refs, hp, n_maps, shared_k, lk, tk, lam_init):
    if n_maps == 2:
        q_ref, k_ref, v_ref, lam_ref, sub_ref, o_ref, s_scr = refs
    else:
        q_ref, k_ref, v_ref, o_ref, s_scr = refs
    tq = q_ref.shape[2]
    nchunks = lk // tk
    nfold = tk // LANES

    def fold(x, op):
        r = x[:, 0:LANES]
        for u in range(1, nfold):
            r = op(r, x[:, u * LANES:(u + 1) * LANES])
        return r

    def one_map(qi, ki, vi, buf):
        q = q_ref[0, qi]
        mx = None
        for c in range(nchunks):
            kc = k_ref[0, ki, c * tk:(c + 1) * tk, :]
            s = lax.dot_general(q, kc, (((1,), (1,)), ((), ())), preferred_element_type=F32)
            s_scr[buf, c] = s
            f = fold(s, jnp.maximum)
            mx = f if mx is None else jnp.maximum(mx, f)
        m = jnp.max(mx, -1, keepdims=True)
        ls = acc = None
        for c in range(nchunks):
            p = jnp.exp(s_scr[buf, c] - m)
            pv = _dot(p.astype(BF16), v_ref[0, vi, c * tk:(c + 1) * tk, :])
            f = fold(p, jnp.add)
            ls, acc = (f, pv) if ls is None else (ls + f, acc + pv)
        return acc / jnp.sum(ls, -1, keepdims=True)

    total = None
    for hh in range(hp):
        ki = 0 if shared_k else hh * n_maps
        o = one_map(hh * n_maps, ki, hh, hh % 2 if n_maps == 1 else 0)
        if n_maps == 2:
            lp = lam_ref[...]
            lam = (jnp.exp(jnp.sum(lp[0:1] * lp[1:2], keepdims=True))
                   - jnp.exp(jnp.sum(lp[2:3] * lp[3:4], keepdims=True)) + lam_init)
            o = o - lam * one_map(hh * n_maps + 1, ki + 1, hh, 1)
            ms = jnp.sum(o * o, -1, keepdims=True) * (1.0 / HEAD_DIM)
            o = o * lax.rsqrt(ms + 1e-5) * sub_ref[...] * (1.0 - lam_init)
        total = o if total is None else total + o
    o_ref[0, 0] = total


def _attn_call(q, k, v, *, n_maps, shared_k, lk, q_off, nq_rows, extra=(), lam_init=0.0):
    nb = q.shape[0]
    t = q_off + nq_rows
    hp = 2
    tq = _pick_tile(math.gcd(nq_rows, q_off) if q_off else nq_rows, 256, 16)
    tk = 256 if lk % 256 == 0 else _pick_tile(lk, 256, LANES)
    qo = q_off // tq
    kh = 1 if shared_k else hp * n_maps
    kern = functools.partial(_attn_kernel, hp=hp, n_maps=n_maps, shared_k=shared_k, lk=lk, tk=tk,
                             lam_init=lam_init)
    in_specs = [
        pl.BlockSpec((1, hp * n_maps, tq, LANES), lambda b, p, i: (b, p, i + qo, 0)),
        pl.BlockSpec((1, kh, lk, LANES), lambda b, p, i: (b, p, 0, 0)),
        pl.BlockSpec((1, hp, lk, LANES), lambda b, p, i: (b, p, 0, 0)),
    ] + [pl.BlockSpec(e.shape, lambda b, p, i: (0, 0)) for e in extra]
    return pl.pallas_call(
        kern,
        out_shape=jax.ShapeDtypeStruct((nb, 2, t, LANES), F32),
        grid=(nb, 2, nq_rows // tq),
        in_specs=in_specs,
        out_specs=pl.BlockSpec((1, 1, tq, LANES), lambda b, p, i: (b, p, i + qo, 0)),
        scratch_shapes=[pltpu.VMEM((2, lk // tk, tq, tk), F32)],
        compiler_params=_cparams(("arbitrary", "arbitrary", "arbitrary")),
        name="attn",
    )(q, k, v, *extra)


def _rwkv_prep_kernel(cur_ref, prv_ref, nxt_ref, mu_ref, wl_ref, bias_ref, kk_ref, ka_ref, rk_ref,
                      sh_ref, dr_ref, bon_ref, *, lc, t_total, tm, halo):
    i = pl.program_id(1)
    cur = cur_ref[0]
    r_i = _iota((tm, tm), 0)
    c_i = _iota((tm, tm), 1)
    s_dn = jnp.where(c_i == r_i - 1, 1.0, 0.0).astype(BF16)
    s_up = jnp.where(c_i == r_i + 1, 1.0, 0.0).astype(BF16)
    row = _iota((tm, D_MODEL), 0)
    tg = i * tm + row
    prv = jnp.where(row == 0, prv_ref[0].astype(F32)[halo - 1:halo, :], _dot(s_dn, cur))
    nxt = jnp.where(row == tm - 1, nxt_ref[0].astype(F32)[0:1, :], _dot(s_up, cur))
    prv = jnp.where((tg == 0) | (tg == lc), 0.0, prv)
    nxt = jnp.where((tg == lc - 1) | (tg == t_total - 1), 0.0, nxt)
    x = cur.astype(F32)
    sh = x + mu_ref[0:1, :] * (prv - x) + mu_ref[1:2, :] * (nxt - x)

    r = sh[:, 0:256]
    k = sh[:, 256:512]
    v = sh[:, 512:768]
    lora = sh[:, 768:1024]
    lora = jnp.where(_iota(lora.shape, 1) < 2 * RWKV_N, jnp.tanh(lora), lora)
    z = _dot(lora.astype(BF16), wl_ref[...]) + bias_ref[...]

    def per_head(fn, x):
        return jnp.concatenate([fn(x[:, 0:LANES]), fn(x[:, LANES:2 * LANES])], axis=1)

    kq = k * kk_ref[...]
    kk = kq * lax.rsqrt(per_head(_half_sums, kq * kq) + 1e-12)
    sh_ref[0, :, 0:256] = r
    sh_ref[0, :, 256:512] = v
    sh_ref[0, :, 512:768] = kk
    hs = None
    for d in range(2):
        logw = -RWKV_DECAY_SCALE * jax.nn.sigmoid(z[:, d * 256:(d + 1) * 256])
        a = jax.nn.sigmoid(z[:, 512 + d * 256:512 + (d + 1) * 256])
        kd = k * (1.0 + (a - 1.0) * ka_ref[...])
        dr_ref[0, d, :, 0:256] = logw
        dr_ref[0, d, :, 256:512] = kd
        dr_ref[0, d, :, 512:768] = a * kk
        s = per_head(_half_sums, r * kd * rk_ref[...])
        hs = s if hs is None else hs + s
    bon_ref[0] = hs * v


def _rwkv_prep_call(p, mu, wl, bias, k_k, k_a, r_k, lc):
    nb, t, _ = p.shape
    tm = _pick_tile(t, 256, 16)
    halo = 16
    cb = P_RWKV // D_MODEL
    hb = tm // halo
    last = t // halo - 1
    kern = functools.partial(_rwkv_prep_kernel, lc=lc, t_total=t, tm=tm, halo=halo)

    def full(a):
        return pl.BlockSpec(a.shape, lambda b, i: (0,) * a.ndim)

    return pl.pallas_call(
        kern,
        out_shape=[jax.ShapeDtypeStruct((nb, t, 768), F32),
                   jax.ShapeDtypeStruct((nb, 2, t, 768), F32),
                   jax.ShapeDtypeStruct((nb, t, 256), F32)],
        grid=(nb, t // tm),
        in_specs=[
            pl.BlockSpec((1, tm, D_MODEL), lambda b, i: (b, i, cb)),
            pl.BlockSpec((1, halo, D_MODEL), lambda b, i: (b, jnp.maximum(i * hb - 1, 0), cb)),
            pl.BlockSpec((1, halo, D_MODEL), lambda b, i: (b, jnp.minimum((i + 1) * hb, last), cb)),
            full(mu), full(wl), full(bias), full(k_k), full(k_a), full(r_k),
        ],
        out_specs=[pl.BlockSpec((1, tm, 768), lambda b, i: (b, i, 0)),
                   pl.BlockSpec((1, 2, tm, 768), lambda b, i: (b, 0, i, 0)),
                   pl.BlockSpec((1, tm, 256), lambda b, i: (b, i, 0))],
        compiler_params=_cparams(("arbitrary", "arbitrary")),
        name="rwkv_prep",
    )(p, p, p, mu, wl, bias, k_k, k_a, r_k)


def _expand(x, bd):
    return jnp.where(bd, jnp.concatenate([x] * 4, axis=0), 0.0)


def _contract(x):
    c = x.shape[0] // 4
    return (x[0:c] + x[c:2 * c]) + (x[2 * c:3 * c] + x[3 * c:4 * c])


def _rwkv_masks():
    C = RWKV_CHUNK
    n = 4 * C
    rr, cc = np.indices((n, n))
    bd = (rr // C) == (cc // C)
    tr, tc = rr % C, cc % C
    const = np.stack([bd, rr == cc]).astype(np.float32)
    dirm = np.stack([np.stack([bd & (tc < tr), bd & (tc <= tr)]),
                     np.stack([bd & (tc > tr), bd & (tc >= tr)])]).astype(np.float32)
    lev = [(rr >> 1) == (cc >> 1)]
    s = 1
    while (1 << s) < C:
        lev.append(((rr >> s) != (cc >> s)) & ((rr >> (s + 1)) == (cc >> (s + 1))))
        s += 1
    r1, c1 = np.indices((C, C))
    incl1 = np.stack([c1 <= r1, c1 >= r1]).astype(np.float32)
    return (jnp.asarray(const), jnp.asarray(dirm), jnp.asarray(np.stack(lev).astype(np.float32), BF16),
            jnp.asarray(incl1, BF16))


def _rwkv_local_kernel(sh_ref, dr_ref, const_ref, dirm_ref, lev_ref, incl1_ref,
                       g_ref, hh_ref, rh_ref, yh_ref, *, nck):
    C = RWKV_CHUNK
    n = 4 * C
    bd = const_ref[0]
    eye = const_ref[1]
    bd_b = bd.astype(BF16)
    strict = dirm_ref[0, 0]
    incl = dirm_ref[0, 1]
    incl1 = incl1_ref[0]

    def expand_b(x):
        return jnp.concatenate([x.astype(BF16)] * 4, axis=0) * bd_b

    for ck in range(nck):
        rows = slice(ck * C, (ck + 1) * C)
        r = sh_ref[0, rows, 0:256]
        v = sh_ref[0, rows, 256:512]
        kk = sh_ref[0, rows, 512:768]
        lw = dr_ref[0, 0, rows, 0:256]
        kd = dr_ref[0, 0, rows, 256:512]
        b = dr_ref[0, 0, rows, 512:768]

        lp = _dot_exact_lhs(incl1, lw)
        ltot = jnp.sum(lw, 0, keepdims=True)
        e_neg = jnp.exp(-lp)
        kap = kk * jnp.exp(lp - lw)
        rt = r * jnp.exp(lp)
        bt = b * e_neg
        kt = kd * e_neg
        e_rem = jnp.exp(ltot - lp)
        bh = b * e_rem
        kh = kd * e_rem

        kap_e = expand_b(kap)
        v_e = expand_b(v)
        lhs = jnp.concatenate([kap_e, expand_b(rt)], axis=0)
        rhs = jnp.concatenate([bt.astype(BF16)] * 4 + [kt.astype(BF16)] * 4, axis=0)
        gram = lax.dot_general(lhs, rhs, (((1,), (1,)), ((), ())), preferred_element_type=F32)
        mb = gram[0:n, 0:n] * strict
        mk = gram[0:n, n:2 * n] * strict
        mrb = gram[n:2 * n, 0:n] * incl
        mrk = gram[n:2 * n, n:2 * n] * incl

        mb_b = mb.astype(BF16)
        inv = eye - (mb_b * lev_ref[0]).astype(F32)
        for lv in range(1, lev_ref.shape[0]):
            inv_b = inv.astype(BF16)
            inv = inv - _dot(_dot(inv_b, mb_b * lev_ref[lv]).astype(BF16), inv_b)

        mkv = _dot(mk.astype(BF16), v_e)
        wu = _dot(inv.astype(BF16), jnp.concatenate([kap_e, mkv.astype(BF16)], axis=1))
        wu_b = wu.astype(BF16)
        mw = _dot(mrb.astype(BF16), wu_b)
        rh = rt - _contract(mw[:, 0:n])
        yh = _contract(_dot(mrk.astype(BF16), v_e) - mw[:, n:2 * n])
        bw = _dot((jnp.concatenate([bh] * 4, axis=0) * bd).T.astype(BF16), wu_b)
        kv = _dot((jnp.concatenate([kh] * 4, axis=0) * bd).T.astype(BF16), v_e)
        g = eye * jnp.exp(ltot) - bw[:, 0:n]
        hh = kv - bw[:, n:2 * n]

        g_ref[0, 0, ck] = _contract(g)
        hh_ref[0, 0, ck] = _contract(hh)
        rh_ref[0, 0, rows, :] = rh
        yh_ref[0, 0, rows, :] = yh


def _rwkv_local_call(shared, dirs):
    nb, t, _ = shared.shape
    C = RWKV_CHUNK
    nch = t // C
    nck = 4 if nch % 4 == 0 else (2 if nch % 2 == 0 else 1)
    tm = nck * C
    n = 4 * C
    const, dirm, lev, incl1 = _rwkv_masks()
    kern = functools.partial(_rwkv_local_kernel, nck=nck)
    mat = jax.ShapeDtypeStruct((nb, 2, nch, C, 256), F32)
    seq = jax.ShapeDtypeStruct((nb, 2, t, 256), F32)
    mat_spec = pl.BlockSpec((1, 1, nck, C, 256), lambda b, d, i: (b, d, i, 0, 0))
    seq_spec = pl.BlockSpec((1, 1, tm, 256), lambda b, d, i: (b, d, i, 0))
    return pl.pallas_call(
        kern,
        out_shape=[mat, mat, seq, seq],
        grid=(nb, 2, nch // nck),
        in_specs=[pl.BlockSpec((1, tm, 768), lambda b, d, i: (b, i, 0)),
                  pl.BlockSpec((1, 1, tm, 768), lambda b, d, i: (b, d, i, 0)),
                  pl.BlockSpec(const.shape, lambda b, d, i: (0, 0, 0)),
                  pl.BlockSpec((1, 2, n, n), lambda b, d, i: (d, 0, 0, 0)),
                  pl.BlockSpec(lev.shape, lambda b, d, i: (0, 0, 0)),
                  pl.BlockSpec((1, C, C), lambda b, d, i: (d, 0, 0))],
        out_specs=[mat_spec, mat_spec, seq_spec, seq_spec],
        compiler_params=_cparams(("arbitrary", "arbitrary", "arbitrary")),
        name="rwkv_local",
    )(shared, dirs, const, dirm, lev, incl1)


def _rwkv_scan_kernel(gf_ref, hf_ref, rf_ref, yf_ref, gb_ref, hb_ref, rb_ref, yb_ref,
                      of_ref, ob_ref, st_ref, *, nb):
    C = RWKV_CHUNK
    n = 4 * C
    j = pl.program_id(0)

    @pl.when(j == 0)
    def _():
        st_ref[...] = jnp.zeros(st_ref.shape, F32)

    rr = _iota((n, n), 0)
    cc = _iota((n, n), 1)
    bd = (rr >> 6) == (cc >> 6)

    def step(d, b, g_ref, h_ref, r_ref, y_ref, o_ref):
        sh_, sm_, _ = _split3(st_ref[d, b])
        rh_, rm_, _ = _split3(_expand(r_ref[b, 0], bd))
        y = _dot(rh_, sh_) + (_dot(rh_, sm_) + _dot(rm_, sh_))
        o_ref[b] = _contract(y) + y_ref[b, 0]
        gh_, gm_, _ = _split3(_expand(g_ref[b, 0, 0], bd))
        new = _dot(gh_, sh_) + (_dot(gh_, sm_) + _dot(gm_, sh_))
        st_ref[d, b] = new + _expand(h_ref[b, 0, 0], bd)

    def body(b, carry):
        step(0, b, gf_ref, hf_ref, rf_ref, yf_ref, of_ref)
        step(1, b, gb_ref, hb_ref, rb_ref, yb_ref, ob_ref)
        return carry

    lax.fori_loop(0, nb, body, 0)


def _rwkv_scan_call(g, hh, rh, yh, lc):
    nb, _, nch, C, _ = g.shape
    t = nch * C
    ncc = lc // C

    def bwd(j):
        return jnp.where(j < ncc, ncc - 1 - j, nch - 1 + ncc - j)

    def mat_spec(d):
        if d == 0:
            return pl.BlockSpec((nb, 1, 1, C, 256), lambda j: (0, 0, j, 0, 0))
        return pl.BlockSpec((nb, 1, 1, C, 256), lambda j: (0, 1, bwd(j), 0, 0))

    def seq_spec(d):
        if d == 0:
            return pl.BlockSpec((nb, 1, C, 256), lambda j: (0, 0, j, 0))
        return pl.BlockSpec((nb, 1, C, 256), lambda j: (0, 1, bwd(j), 0))

    kern = functools.partial(_rwkv_scan_kernel, nb=nb)
    out = jax.ShapeDtypeStruct((nb, t, 256), F32)
    specs = []
    for d in range(2):
        specs += [mat_spec(d), mat_spec(d), seq_spec(d), seq_spec(d)]
    of, ob = pl.pallas_call(
        kern,
        out_shape=[out, out],
        grid=(nch,),
        in_specs=specs,
        out_specs=[pl.BlockSpec((nb, C, 256), lambda j: (0, j, 0)),
                   pl.BlockSpec((nb, C, 256), lambda j: (0, bwd(j), 0))],
        scratch_shapes=[pltpu.VMEM((2, nb, 4 * C, 256), F32)],
        compiler_params=_cparams(("arbitrary",)),
        name="rwkv_scan",
    )(g, hh, rh, yh, g, hh, rh, yh)
    return of, ob


def _merge_kernel(x_ref, mod_ref, oa_ref, oc_ref, od_ref, yf_ref, yb_ref, bon_ref, sg_ref, mg_ref,
                  bw_ref, ow_ref, mb_ref, gnw_ref, gnb_ref, lng_ref, lnb_ref, o_ref,
                  *, lc, tm, nb, row_off, alpha):
    b = pl.program_id(0)
    i = pl.program_id(1)

    def pair_cat(ref):
        return jnp.concatenate([ref[0, 0], ref[0, 1]], axis=1)

    def per_head(fn, x):
        return jnp.concatenate([fn(x[:, 0:LANES]), fn(x[:, LANES:2 * LANES])], axis=1)

    y = yf_ref[0] + yb_ref[0]
    mu = per_head(_half_sums, y) * (1.0 / RWKV_N)
    yc = y - mu
    var = per_head(_half_sums, yc * yc) * (1.0 / RWKV_N)
    ob = yc * lax.rsqrt(var + RWKV_GN_EPS) * gnw_ref[...] + gnb_ref[...] + bon_ref[0]

    outs = (pair_cat(oa_ref), ob, pair_cat(oc_ref), pair_cat(od_ref))
    acc = None
    for k in range(N_BRANCH):
        sg = sg_ref[0, :, k * BRANCH_W:(k + 1) * BRANCH_W].astype(F32)
        u = outs[k] * (sg * jax.nn.sigmoid(sg))
        z = _dot(u.astype(BF16), bw_ref[k])
        gate = jax.nn.sigmoid(mg_ref[0, :, k * D_MODEL:(k + 1) * D_MODEL].astype(F32)
                              + mb_ref[:, k * D_MODEL:(k + 1) * D_MODEL])
        acc = gate * z if acc is None else acc + gate * z
    y2 = _dot(acc.astype(BF16), ow_ref[...])

    row = row_off + i * tm + _iota((tm, D_MODEL), 0)
    gl = mod_ref[pl.ds(b, 1), 2 * D_MODEL:3 * D_MODEL]
    gc = mod_ref[pl.ds(nb, 1), 2 * D_MODEL:3 * D_MODEL]
    h = alpha * x_ref[0] + jnp.where(row < lc, gc, gl) * y2
    m = jnp.mean(h, -1, keepdims=True)
    hc = h - m
    var = jnp.mean(hc * hc, -1, keepdims=True)
    o_ref[0] = hc * lax.rsqrt(var + 1e-5) * lng_ref[...] + lnb_ref[...]


def _merge_call(xx, mod, oa, oc, od, yf, yb, bon, p, bw, ow, mb, gnw, gnb, lng, lnb, *, lc, row_off, alpha):
    nb, t, _ = xx.shape
    nrows = t - row_off
    tm = _pick_tile(math.gcd(nrows, row_off) if row_off else nrows, 256, 16)
    ro = row_off // tm
    kern = functools.partial(_merge_kernel, lc=lc, tm=tm, nb=nb, row_off=row_off, alpha=alpha)

    def full(a):
        return pl.BlockSpec(a.shape, lambda b, i: (0,) * a.ndim)

    pair_spec = pl.BlockSpec((1, 2, tm, LANES), lambda b, i: (b, 0, i + ro, 0))
    return pl.pallas_call(
        kern,
        out_shape=jax.ShapeDtypeStruct((nb, nrows, D_MODEL), F32),
        grid=(nb, nrows // tm),
        in_specs=[
            pl.BlockSpec((1, tm, D_MODEL), lambda b, i: (b, i + ro, 0)),
            full(mod), pair_spec, pair_spec, pair_spec,
            pl.BlockSpec((1, tm, 256), lambda b, i: (b, i + ro, 0)),
            pl.BlockSpec((1, tm, 256), lambda b, i: (b, i + ro, 0)),
            pl.BlockSpec((1, tm, 256), lambda b, i: (b, i + ro, 0)),
            pl.BlockSpec((1, tm, D_MODEL), lambda b, i: (b, i + ro, P_SILU // D_MODEL)),
            pl.BlockSpec((1, tm, N_BRANCH * D_MODEL), lambda b, i: (b, i + ro, P_MERGE // (N_BRANCH * D_MODEL))),
            full(bw), full(ow), full(mb), full(gnw), full(gnb), full(lng), full(lnb),
        ],
        out_specs=pl.BlockSpec((1, tm, D_MODEL), lambda b, i: (b, i, 0)),
        compiler_params=_cparams(("arbitrary", "arbitrary")),
        name="merge",
    )(xx, mod, oa, oc, od, yf, yb, bon, p, p, bw, ow, mb, gnw, gnb, lng, lnb)


def _pack_in_w(w):
    z = lambda n: jnp.zeros((D_MODEL, n), w.dtype)
    a0 = 0
    b0 = 672
    c0 = b0 + 1280
    d0 = c0 + 768
    m0 = d0 + 1024
    cols = [
        w[:, a0:a0 + 256], w[:, a0 + 256:a0 + 384],
        z(64), w[:, a0 + 384:a0 + 416], z(32),
        w[:, c0:c0 + 256], w[:, c0 + 256:c0 + 384], w[:, c0 + 384:c0 + 512],
        w[:, d0:d0 + 256], w[:, d0 + 256:d0 + 512], w[:, d0 + 512:d0 + 768],
        z(256),
        w[:, b0:b0 + 1024],
        w[:, a0 + 416:a0 + 672], w[:, b0 + 1024:b0 + 1280], w[:, c0 + 512:c0 + 768], w[:, d0 + 768:d0 + 1024],
        w[:, m0:m0 + 4096],
    ]
    out = jnp.concatenate(cols, axis=1)
    assert out.shape[1] == P_WIDTH
    return out.astype(BF16)


def _pack_mla(w_uq, w_ukv):
    zq = jnp.zeros((MLA_Q_LORA, LANES - MLA_NOPE - MLA_ROPE), w_uq.dtype)
    zk = jnp.zeros((MLA_KV_LORA, LANES - MLA_NOPE), w_ukv.dtype)
    zv = jnp.zeros((MLA_KV_LORA, MLA_V), w_ukv.dtype)
    qc, kc, vc = [], [], []
    for h in range(MLA_HEADS):
        qc += [w_uq[:, h * 96:(h + 1) * 96], zq]
        kc += [w_ukv[:, h * 128:h * 128 + 64], zk]
        vh = w_ukv[:, h * 128 + 64:(h + 1) * 128]
        vc += [vh, zv] if h % 2 == 0 else [zv, vh]
    cat = lambda xs: jnp.concatenate(xs, axis=1).astype(BF16)
    return cat(qc), cat(kc), cat(vc)


def _rope_tables(row, col, rot_dim, lc, pattern):
    quarter = rot_dim // 4
    inv_freq = ROPE_BASE ** (-jnp.arange(quarter, dtype=F32) / quarter)
    ang = jnp.concatenate([row[:, None] * inv_freq, col[:, None] * inv_freq], axis=-1)
    cos, sin = jnp.cos(ang), jnp.sin(ang)
    n = ang.shape[0]
    ones = lambda w: jnp.ones((n, w), F32)
    zeros = lambda w: jnp.zeros((n, w), F32)
    if pattern == "mla":
        c = jnp.concatenate([ones(64), cos, cos, ones(32)], axis=1)
        s = jnp.concatenate([zeros(64), -sin, sin, zeros(32)], axis=1)
    else:
        reps = LANES // rot_dim
        c = jnp.concatenate([cos, cos] * reps, axis=1)
        s = jnp.concatenate([-sin, sin] * reps, axis=1)
    c = jnp.concatenate([jnp.ones((lc, LANES), F32), c], axis=0)
    s = jnp.concatenate([jnp.zeros((lc, LANES), F32), s], axis=0)
    return c, s


def kernel(x, c, ctx, c_ctx, ada_w, ada_b, in_w, mla_q_norm, mla_w_uq, mla_kv_norm, mla_w_ukv, rwkv_mu, rwkv_w0, rwkv_w_up, rwkv_a0, rwkv_a_up, rwkv_k_k, rwkv_k_a, rwkv_r_k, rwkv_gn_w, rwkv_gn_b, gqa_q_norm, gqa_k_norm, diff_lambda, diff_subln, merge_b, branch_w, out_w, ln_g, ln_b):
    nb, ll, _ = x.shape
    lc = ctx.shape[1]
    depth = ada_w.shape[0]
    t = lc + ll
    alpha = (2 * depth) ** 0.25

    rows = ll // GRID_W
    row = jnp.repeat(jnp.arange(rows), GRID_W).astype(F32)
    col = jnp.tile(jnp.arange(GRID_W), rows).astype(F32)
    tabs = (_rope_tables(row, col, MLA_ROPE, lc, "mla") + _rope_tables(row, col, HEAD_DIM, lc, "tile")
            + _rope_tables(row, col, DIFF_D, lc, "tile"))

    crows = -(-(nb + 1) // 8) * 8
    cvec = jnp.concatenate([c, c_ctx[None, :], jnp.zeros((crows - nb - 1, D_MODEL), F32)], axis=0)
    mods = _ada_call(cvec, ada_w, ada_b)

    xx = jnp.concatenate([ctx, x], axis=1)
    tile2 = lambda a: jnp.concatenate([a, a])[None, :]
    for l in range(depth):
        last = l == depth - 1
        lam_init = 0.8 - 0.6 * math.exp(-0.3 * l)
        mod = mods[l]
        p = _inproj_call(xx, mod, _pack_in_w(in_w[l]), lc)

        wuq, wk, wv = _pack_mla(mla_w_uq[l], mla_w_ukv[l])
        qa, ka, va, qc, kc, vc, qd, kd, vd = _prep_call(
            p, tabs, (mla_q_norm[l][None, :], wuq, mla_kv_norm[l][None, :], wk, wv,
                      tile2(gqa_q_norm[l]), tile2(gqa_k_norm[l])))
        lam_p = jnp.zeros((8, LANES), F32).at[0:4, 0:DIFF_D].set(diff_lambda[l])
        sub = tile2(diff_subln[l])
        att = lambda q, k, v, **kw: _attn_call(q, k, v, lk=t, q_off=lc, nq_rows=ll, **kw)
        oa = att(qa, ka, va, n_maps=1, shared_k=False)
        oc = att(qc, kc, vc, n_maps=1, shared_k=True)
        od = att(qd, kd, vd, n_maps=2, shared_k=False, extra=(lam_p, sub), lam_init=lam_init)
        if not last:
            catt = lambda q, k, v, **kw: _attn_call(q, k, v, lk=lc, q_off=0, nq_rows=lc, **kw)
            merge_rows = lambda full_o, ctx_o: lax.dynamic_update_slice(full_o, ctx_o, (0, 0, 0, 0))
            oa = merge_rows(oa, catt(qa, ka, va, n_maps=1, shared_k=False))
            oc = merge_rows(oc, catt(qc, kc, vc, n_maps=1, shared_k=True))
            od = merge_rows(od, catt(qd, kd, vd, n_maps=2, shared_k=False, extra=(lam_p, sub), lam_init=lam_init))

        wl = jnp.zeros((256, 1024), F32)
        for d in range(2):
            wl = wl.at[d * 64:(d + 1) * 64, d * 256:(d + 1) * 256].set(rwkv_w_up[l, d])
            wl = wl.at[128 + d * 64:128 + (d + 1) * 64, 512 + d * 256:512 + (d + 1) * 256].set(rwkv_a_up[l, d])
        bias = jnp.concatenate([rwkv_w0[l, 0], rwkv_w0[l, 1], rwkv_a0[l, 0], rwkv_a0[l, 1]])[None, :]
        shared, dirs, bon = _rwkv_prep_call(p, rwkv_mu[l], wl.astype(BF16), bias, rwkv_k_k[l][None, :],
                                            rwkv_k_a[l][None, :], rwkv_r_k[l][None, :], lc)
        g, hh, rh, yh = _rwkv_local_call(shared, dirs)
        yf, yb = _rwkv_scan_call(g, hh, rh, yh, lc)

        xx = _merge_call(xx, mod, oa, oc, od, yf, yb, bon, p,
                         branch_w[l].astype(BF16), out_w[l].astype(BF16), merge_b[l][None, :],
                         rwkv_gn_w[l][None, :], rwkv_gn_b[l][None, :], ln_g[l][None, :], ln_b[l][None, :],
                         lc=lc, row_off=lc if last else 0, alpha=alpha)
    return xx
```

```python
import functools
import math

import jax
import jax.numpy as jnp
import numpy as np
from jax import lax
from jax.experimental import pallas as pl
from jax.experimental.pallas import tpu as pltpu

F32 = jnp.float32
BF16 = jnp.bfloat16

D_MODEL = 1024
GRID_W = 64
ROPE_BASE = 10000.0
HEAD_DIM = 64
N_BRANCH = 4
BRANCH_W = 256
MLA_HEADS = 4
MLA_Q_LORA = 256
MLA_KV_LORA = 128
MLA_NOPE = 64
MLA_ROPE = 32
MLA_V = 64
MLA_SCALE = (MLA_NOPE + MLA_ROPE) ** -0.5
RWKV_N = 64
RWKV_GN_EPS = 64e-5
RWKV_DECAY_SCALE = math.exp(-0.5)
GQA_SCALE = HEAD_DIM ** -0.5
DIFF_D = 32
DIFF_SCALE = DIFF_D ** -0.5

LANES = 128
VMEM_LIMIT = 56 * 1024 * 1024

P_AQ, P_AKV, P_AKR = 0, 256, 384
P_CQ, P_CK, P_CV = 512, 768, 896
P_DQ, P_DK, P_DV = 1024, 1280, 1536
P_RWKV = 2048
P_SILU = 3072
P_MERGE = 4096
P_WIDTH = 8192
ATTN_W = 2048

RWKV_CHUNK = 64
ATTN_TQ = 256


def _cparams(sem):
    return pltpu.CompilerParams(dimension_semantics=sem, vmem_limit_bytes=VMEM_LIMIT)


def _split3(x):
    h = x.astype(BF16)
    r = x - h.astype(F32)
    m = r.astype(BF16)
    l = (r - m.astype(F32)).astype(BF16)
    return h, m, l


def _dot(a, b):
    return jnp.dot(a, b, preferred_element_type=F32)


def _dot_hi(a, b):
    ah, am, _ = _split3(a)
    bh, bm, _ = _split3(b)
    return _dot(ah, bh) + (_dot(ah, bm) + _dot(am, bh))


def _dot_exact_lhs(a_bf16, b):
    bh, bm, bl = _split3(b)
    return _dot(a_bf16, bh) + (_dot(a_bf16, bm) + _dot(a_bf16, bl))


def _iota(shape, dim):
    return lax.broadcasted_iota(jnp.int32, shape, dim)


def _ada_kernel(c_ref, w_ref, b_ref, o_ref):
    c = c_ref[...]
    s = c * jax.nn.sigmoid(c)
    o_ref[0] = _dot_hi(s, w_ref[0]) + b_ref[0]


def _ada_call(cvec, ada_w, ada_b):
    depth = ada_w.shape[0]
    rows = cvec.shape[0]
    nblk = 3
    return pl.pallas_call(
        _ada_kernel,
        out_shape=jax.ShapeDtypeStruct((depth, rows, 3 * D_MODEL), F32),
        grid=(depth, nblk),
        in_specs=[
            pl.BlockSpec((rows, D_MODEL), lambda l, j: (0, 0)),
            pl.BlockSpec((1, D_MODEL, D_MODEL), lambda l, j: (l, 0, j)),
            pl.BlockSpec((1, 1, D_MODEL), lambda l, j: (l, 0, j)),
        ],
        out_specs=pl.BlockSpec((1, rows, D_MODEL), lambda l, j: (l, 0, j)),
        compiler_params=_cparams(("arbitrary", "arbitrary")),
        name="ada",
    )(cvec, ada_w, ada_b.reshape(depth, 1, 3 * D_MODEL))


def _inproj_kernel(x_ref, mod_ref, w_ref, o_ref, xm_ref, *, lc, tm, nb):
    b = pl.program_id(0)
    i = pl.program_id(1)
    j = pl.program_id(2)

    @pl.when(j == 0)
    def _():
        x = x_ref[0]
        mu = jnp.mean(x, -1, keepdims=True)
        xc = x - mu
        var = jnp.mean(xc * xc, -1, keepdims=True)
        xn = xc * lax.rsqrt(var + 1e-6)
        row = i * tm + _iota((tm, D_MODEL), 0)
        is_ctx = row < lc
        ml = mod_ref[pl.ds(b, 1), :]
        mc = mod_ref[pl.ds(nb, 1), :]
        shift = jnp.where(is_ctx, mc[:, 0:D_MODEL], ml[:, 0:D_MODEL])
        scale = jnp.where(is_ctx, mc[:, D_MODEL:2 * D_MODEL], ml[:, D_MODEL:2 * D_MODEL])
        xm_ref[...] = (xn * (1.0 + scale) + shift).astype(BF16)

    o_ref[0] = _dot(xm_ref[...], w_ref[...]).astype(BF16)


def _inproj_call(xx, mod, w_packed, lc):
    nb, t, _ = xx.shape
    tm = _pick_tile(t, 1088, 16)
    tn = 1024
    kern = functools.partial(_inproj_kernel, lc=lc, tm=tm, nb=nb)
    return pl.pallas_call(
        kern,
        out_shape=jax.ShapeDtypeStruct((nb, t, P_WIDTH), BF16),
        grid=(nb, t // tm, P_WIDTH // tn),
        in_specs=[
            pl.BlockSpec((1, tm, D_MODEL), lambda b, i, j: (b, i, 0)),
            pl.BlockSpec(mod.shape, lambda b, i, j: (0, 0)),
            pl.BlockSpec((D_MODEL, tn), lambda b, i, j: (0, j)),
        ],
        out_specs=pl.BlockSpec((1, tm, tn), lambda b, i, j: (b, i, j)),
        scratch_shapes=[pltpu.VMEM((tm, D_MODEL), BF16)],
        compiler_params=_cparams(("arbitrary", "arbitrary", "arbitrary")),
        name="inproj",
    )(xx, mod, w_packed)


def _pick_tile(n, cap, mult):
    best = None
    for d in range(mult, min(n, cap) + 1, mult):
        if n % d == 0:
            best = d
    assert best is not None, (n, cap, mult)
    return best


def _swap_halves(x, half):
    n = x.shape[-1]
    first = (_iota(x.shape, 1) & (2 * half - 1)) < half
    up = pltpu.roll(x, n - half, 1)
    dn = pltpu.roll(x, half, 1)
    return jnp.where(first, up, dn)


def _half_sums(x):
    lo = _iota(x.shape, 1) < HEAD_DIM
    s_lo = jnp.sum(jnp.where(lo, x, 0.0), -1, keepdims=True)
    s_hi = jnp.sum(jnp.where(lo, 0.0, x), -1, keepdims=True)
    return jnp.where(lo, s_lo, s_hi)


def _prep_kernel(p_ref, cosa_ref, sina_ref, cosc_ref, sinc_ref, cosd_ref, sind_ref,
                 qn_ref, wuq_ref, kvn_ref, wk_ref, wv_ref, gq_ref, gk_ref,
                 qa_ref, ka_ref, va_ref, qc_ref, kc_ref, vc_ref, qd_ref, kd_ref, vd_ref):
    def seg(off, w):
        return p_ref[0, :, off:off + w].astype(F32)

    lane = _iota((p_ref.shape[1], LANES), 1)
    lo = lane < HEAD_DIM

    cosa, sina = cosa_ref[...], sina_ref[...]
    ql = seg(P_AQ, MLA_Q_LORA)
    ql = ql * lax.rsqrt(jnp.mean(ql * ql, -1, keepdims=True) + 1e-6) * qn_ref[...]
    q = _dot(ql.astype(BF16), wuq_ref[...])
    kvl = seg(P_AKV, MLA_KV_LORA)
    kvl = (kvl * lax.rsqrt(jnp.mean(kvl * kvl, -1, keepdims=True) + 1e-6) * kvn_ref[...]).astype(BF16)
    kn = _dot(kvl, wk_ref[...])
    vv = _dot(kvl, wv_ref[...])
    kr = seg(P_AKR, LANES)
    kr = kr * cosa + _swap_halves(kr, MLA_ROPE // 2) * sina
    for h in range(MLA_HEADS):
        qh = q[:, h * LANES:(h + 1) * LANES]
        qh = qh * cosa + _swap_halves(qh, MLA_ROPE // 2) * sina
        qa_ref[0, h] = (qh * MLA_SCALE).astype(BF16)
        ka_ref[0, h] = (kn[:, h * LANES:(h + 1) * LANES] + kr).astype(BF16)
        va_ref[0, h] = vv[:, h * LANES:(h + 1) * LANES].astype(BF16)

    cosc, sinc = cosc_ref[...], sinc_ref[...]

    def norm_rope(x, g):
        x = x * lax.rsqrt(_half_sums(x * x) * (1.0 / HEAD_DIM) + 1e-6) * g
        return x * cosc + _swap_halves(x, HEAD_DIM // 2) * sinc

    def split_heads(blk):
        return jnp.where(lo, blk, 0.0), jnp.where(lo, pltpu.roll(blk, HEAD_DIM, 1), 0.0)

    for pr in range(2):
        qb = norm_rope(seg(P_CQ + pr * LANES, LANES), gq_ref[...]) * GQA_SCALE
        q0, q1 = split_heads(qb)
        qc_ref[0, 2 * pr] = q0.astype(BF16)
        qc_ref[0, 2 * pr + 1] = q1.astype(BF16)
    k0, k1 = split_heads(norm_rope(seg(P_CK, LANES), gk_ref[...]))
    kc_ref[0, 0] = k0.astype(BF16)
    kc_ref[0, 1] = k1.astype(BF16)
    vb = seg(P_CV, LANES)
    v_g0 = jnp.where(lo, vb, 0.0)
    v_g1 = jnp.where(lo, 0.0, vb)
    vc_ref[0, 0] = v_g0.astype(BF16)
    vc_ref[0, 1] = pltpu.roll(v_g0, HEAD_DIM, 1).astype(BF16)
    vc_ref[0, 2] = pltpu.roll(v_g1, HEAD_DIM, 1).astype(BF16)
    vc_ref[0, 3] = v_g1.astype(BF16)

    cosd, sind = cosd_ref[...], sind_ref[...]
    piece = lane < DIFF_D
    for pr in range(2):
        qb = seg(P_DQ + pr * LANES, LANES)
        qb = (qb * cosd + _swap_halves(qb, DIFF_D // 2) * sind) * DIFF_SCALE
        kb = seg(P_DK + pr * LANES, LANES)
        kb = kb * cosd + _swap_halves(kb, DIFF_D // 2) * sind
        vb = seg(P_DV + pr * LANES, LANES)
        for hh in range(2):
            for m in range(2):
                off = hh * HEAD_DIM + m * DIFF_D
                idx = (2 * pr + hh) * 2 + m
                qs = qb if off == 0 else pltpu.roll(qb, LANES - off, 1)
                ks = kb if off == 0 else pltpu.roll(kb, LANES - off, 1)
                qd_ref[0, idx] = jnp.where(piece, qs, 0.0).astype(BF16)
                kd_ref[0, idx] = jnp.where(piece, ks, 0.0).astype(BF16)
        vd_ref[0, 2 * pr] = jnp.where(lo, vb, 0.0).astype(BF16)
        vd_ref[0, 2 * pr + 1] = jnp.where(lo, 0.0, vb).astype(BF16)


def _prep_call(p, tabs, wts):
    nb, t, _ = p.shape
    tm = _pick_tile(t, 256, 16)
    row_spec = pl.BlockSpec((tm, LANES), lambda b, i: (i, 0))

    def full(a):
        return pl.BlockSpec(a.shape, lambda b, i: (0,) * a.ndim)

    def head_out(nh):
        return (jax.ShapeDtypeStruct((nb, nh, t, LANES), BF16),
                pl.BlockSpec((1, nh, tm, LANES), lambda b, i: (b, 0, i, 0)))

    outs = [head_out(n) for n in (4, 4, 4, 4, 2, 4, 8, 8, 4)]
    return pl.pallas_call(
        _prep_kernel,
        out_shape=[o[0] for o in outs],
        grid=(nb, t // tm),
        in_specs=[pl.BlockSpec((1, tm, ATTN_W), lambda b, i: (b, i, 0))]
        + [row_spec] * 6 + [full(w) for w in wts],
        out_specs=[o[1] for o in outs],
        compiler_params=_cparams(("arbitrary", "arbitrary")),
        name="attn_prep",
    )(p, *tabs, *wts)


def _attn_kernel(*refs, hp, n_maps, shared_k, lk, tk, lam_init):
    if n_maps == 2:
        q_ref, k_ref, v_ref, lam_ref, sub_ref, o_ref, s_scr = refs
    else:
        q_ref, k_ref, v_ref, o_ref, s_scr = refs
    tq = q_ref.shape[2]
    nchunks = lk // tk
    nfold = tk // LANES

    def fold(x, op):
        r = x[:, 0:LANES]
        for u in range(1, nfold):
            r = op(r, x[:, u * LANES:(u + 1) * LANES])
        return r

    def one_map(qi, ki, vi, buf):
        q = q_ref[0, qi]
        mx = None
        for c in range(nchunks):
            kc = k_ref[0, ki, c * tk:(c + 1) * tk, :]
            s = lax.dot_general(q, kc, (((1,), (1,)), ((), ())), preferred_element_type=F32)
            s_scr[buf, c] = s
            f = fold(s, jnp.maximum)
            mx = f if mx is None else jnp.maximum(mx, f)
        m = jnp.max(mx, -1, keepdims=True)
        ls = acc = None
        for c in range(nchunks):
            p = jnp.exp(s_scr[buf, c] - m)
            pv = _dot(p.astype(BF16), v_ref[0, vi, c * tk:(c + 1) * tk, :])
            f = fold(p, jnp.add)
            ls, acc = (f, pv) if ls is None else (ls + f, acc + pv)
        return acc / jnp.sum(ls, -1, keepdims=True)

    total = None
    for hh in range(hp):
        ki = 0 if shared_k else hh * n_maps
        o = one_map(hh * n_maps, ki, hh, hh % 2 if n_maps == 1 else 0)
        if n_maps == 2:
            lp = lam_ref[...]
            lam = (jnp.exp(jnp.sum(lp[0:1] * lp[1:2], keepdims=True))
                   - jnp.exp(jnp.sum(lp[2:3] * lp[3:4], keepdims=True)) + lam_init)
            o = o - lam * one_map(hh * n_maps + 1, ki + 1, hh, 1)
            ms = jnp.sum(o * o, -1, keepdims=True) * (1.0 / HEAD_DIM)
            o = o * lax.rsqrt(ms + 1e-5) * sub_ref[...] * (1.0 - lam_init)
        total = o if total is None else total + o
    o_ref[0, 0] = total


def _attn_call(q, k, v, *, n_maps, shared_k, lk, q_off, nq_rows, extra=(), lam_init=0.0):
    nb = q.shape[0]
    hp = 2
    tq = _pick_tile(nq_rows, ATTN_TQ, 16)
    tk = 256 if lk % 256 == 0 else _pick_tile(lk, 256, LANES)
    kh = 1 if shared_k else hp * n_maps
    kern = functools.partial(_attn_kernel, hp=hp, n_maps=n_maps, shared_k=shared_k, lk=lk, tk=tk,
                             lam_init=lam_init)
    in_specs = [
        pl.BlockSpec((pl.Element(1), pl.Element(hp * n_maps), pl.Element(tq), pl.Element(LANES)),
                     lambda b, p, i: (b, p * hp * n_maps, pl.multiple_of(i * tq + q_off, 16), 0)),
        pl.BlockSpec((1, kh, lk, LANES), lambda b, p, i: (b, p, 0, 0)),
        pl.BlockSpec((1, hp, lk, LANES), lambda b, p, i: (b, p, 0, 0)),
    ] + [pl.BlockSpec(e.shape, lambda b, p, i: (0, 0)) for e in extra]
    return pl.pallas_call(
        kern,
        out_shape=jax.ShapeDtypeStruct((nb, 2, nq_rows, LANES), F32),
        grid=(nb, 2, nq_rows // tq),
        in_specs=in_specs,
        out_specs=pl.BlockSpec((1, 1, tq, LANES), lambda b, p, i: (b, p, i, 0)),
        scratch_shapes=[pltpu.VMEM((2, lk // tk, tq, tk), F32)],
        compiler_params=_cparams(("arbitrary", "arbitrary", "arbitrary")),
        name="attn",
    )(q, k, v, *extra)


def _rwkv_prep_kernel(cur_ref, prv_ref, nxt_ref, mu_ref, wl_ref, bias_ref, kk_ref, ka_ref, rk_ref,
                      sh_ref, dr_ref, bon_ref, *, lc, t_total, tm, halo):
    i = pl.program_id(1)
    cur = cur_ref[0]
    r_i = _iota((tm, tm), 0)
    c_i = _iota((tm, tm), 1)
    s_dn = jnp.where(c_i == r_i - 1, 1.0, 0.0).astype(BF16)
    s_up = jnp.where(c_i == r_i + 1, 1.0, 0.0).astype(BF16)
    row = _iota((tm, D_MODEL), 0)
    tg = i * tm + row
    prv = jnp.where(row == 0, prv_ref[0].astype(F32)[halo - 1:halo, :], _dot(s_dn, cur))
    nxt = jnp.where(row == tm - 1, nxt_ref[0].astype(F32)[0:1, :], _dot(s_up, cur))
    prv = jnp.where((tg == 0) | (tg == lc), 0.0, prv)
    nxt = jnp.where((tg == lc - 1) | (tg == t_total - 1), 0.0, nxt)
    x = cur.astype(F32)
    sh = x + mu_ref[0:1, :] * (prv - x) + mu_ref[1:2, :] * (nxt - x)

    r = sh[:, 0:256]
    k = sh[:, 256:512]
    v = sh[:, 512:768]
    lora = sh[:, 768:1024]
    lora = jnp.where(_iota(lora.shape, 1) < 2 * RWKV_N, jnp.tanh(lora), lora)
    z = _dot(lora.astype(BF16), wl_ref[...]) + bias_ref[...]

    def per_head(fn, x):
        return jnp.concatenate([fn(x[:, 0:LANES]), fn(x[:, LANES:2 * LANES])], axis=1)

    kq = k * kk_ref[...]
    kk = kq * lax.rsqrt(per_head(_half_sums, kq * kq) + 1e-12)
    sh_ref[0, :, 0:256] = r
    sh_ref[0, :, 256:512] = v
    sh_ref[0, :, 512:768] = kk
    hs = None
    for d in range(2):
        logw = -RWKV_DECAY_SCALE * jax.nn.sigmoid(z[:, d * 256:(d + 1) * 256])
        a = jax.nn.sigmoid(z[:, 512 + d * 256:512 + (d + 1) * 256])
        kd = k * (1.0 + (a - 1.0) * ka_ref[...])
        dr_ref[0, d, :, 0:256] = logw
        dr_ref[0, d, :, 256:512] = kd
        dr_ref[0, d, :, 512:768] = a * kk
        s = per_head(_half_sums, r * kd * rk_ref[...])
        hs = s if hs is None else hs + s
    bon_ref[0] = hs * v


def _rwkv_prep_call(p, mu, wl, bias, k_k, k_a, r_k, lc):
    nb, t, _ = p.shape
    tm = _pick_tile(t, 256, 16)
    halo = 16
    cb = P_RWKV // D_MODEL
    hb = tm // halo
    last = t // halo - 1
    kern = functools.partial(_rwkv_prep_kernel, lc=lc, t_total=t, tm=tm, halo=halo)

    def full(a):
        return pl.BlockSpec(a.shape, lambda b, i: (0,) * a.ndim)

    return pl.pallas_call(
        kern,
        out_shape=[jax.ShapeDtypeStruct((nb, t, 768), F32),
                   jax.ShapeDtypeStruct((nb, 2, t, 768), F32),
                   jax.ShapeDtypeStruct((nb, t, 256), F32)],
        grid=(nb, t // tm),
        in_specs=[
            pl.BlockSpec((1, tm, D_MODEL), lambda b, i: (b, i, cb)),
            pl.BlockSpec((1, halo, D_MODEL), lambda b, i: (b, jnp.maximum(i * hb - 1, 0), cb)),
            pl.BlockSpec((1, halo, D_MODEL), lambda b, i: (b, jnp.minimum((i + 1) * hb, last), cb)),
            full(mu), full(wl), full(bias), full(k_k), full(k_a), full(r_k),
        ],
        out_specs=[pl.BlockSpec((1, tm, 768), lambda b, i: (b, i, 0)),
                   pl.BlockSpec((1, 2, tm, 768), lambda b, i: (b, 0, i, 0)),
                   pl.BlockSpec((1, tm, 256), lambda b, i: (b, i, 0))],
        compiler_params=_cparams(("arbitrary", "arbitrary")),
        name="rwkv_prep",
    )(p, p, p, mu, wl, bias, k_k, k_a, r_k)


def _expand(x, bd):
    return jnp.where(bd, jnp.concatenate([x] * 4, axis=0), 0.0)


def _contract(x):
    c = x.shape[0] // 4
    return (x[0:c] + x[c:2 * c]) + (x[2 * c:3 * c] + x[3 * c:4 * c])


def _rwkv_masks():
    C = RWKV_CHUNK
    n = 4 * C
    rr, cc = np.indices((n, n))
    bd = ((rr // C) == (cc // C)).astype(np.float32)
    tr, tc = np.indices((C, n))
    tc = tc % C
    eye = (tr == tc).astype(np.float32)
    dirm = np.stack([np.stack([tc < tr, tc <= tr]), np.stack([tc > tr, tc >= tr])]).astype(np.float32)
    lev = [(tr >> 1) == (tc >> 1)]
    s = 1
    while (1 << s) < C:
        lev.append(((tr >> s) != (tc >> s)) & ((tr >> (s + 1)) == (tc >> (s + 1))))
        s += 1
    r1, c1 = np.indices((C, C))
    incl1 = np.stack([c1 <= r1, c1 >= r1]).astype(np.float32)
    return (jnp.asarray(bd), jnp.asarray(eye), jnp.asarray(dirm),
            jnp.asarray(np.stack(lev).astype(np.float32), BF16), jnp.asarray(incl1, BF16))


def _rwkv_local_kernel(sh_ref, dr_ref, bd_ref, eye_ref, dirm_ref, lev_ref, incl1_ref,
                       g_ref, hh_ref, rh_ref, yh_ref, *, nck):
    C = RWKV_CHUNK
    n = 4 * C
    bd = bd_ref[...]
    eye = eye_ref[...]
    bd_b = bd.astype(BF16)
    strict = dirm_ref[0, 0]
    incl = dirm_ref[0, 1]
    incl1 = incl1_ref[0]
    nt = (((1,), (1,)), ((), ()))
    tn = (((0,), (0,)), ((), ()))

    def expand_b(x):
        return jnp.concatenate([x.astype(BF16)] * 4, axis=0) * bd_b

    cks = range(nck)
    rows = [slice(ck * C, (ck + 1) * C) for ck in cks]
    r = [sh_ref[0, rw, 0:256] for rw in rows]
    v = [sh_ref[0, rw, 256:512] for rw in rows]
    kk = [sh_ref[0, rw, 512:768] for rw in rows]
    lw = [dr_ref[0, 0, rw, 0:256] for rw in rows]
    kd = [dr_ref[0, 0, rw, 256:512] for rw in rows]
    b = [dr_ref[0, 0, rw, 512:768] for rw in rows]

    lp = [_dot_exact_lhs(incl1, lw[i]) for i in cks]
    ltot = [jnp.sum(lw[i], 0, keepdims=True) for i in cks]
    e_neg = [jnp.exp(-lp[i]) for i in cks]
    kap = [kk[i] * jnp.exp(lp[i] - lw[i]) for i in cks]
    rt = [r[i] * jnp.exp(lp[i]) for i in cks]
    bt = [b[i] * e_neg[i] for i in cks]
    kt = [kd[i] * e_neg[i] for i in cks]
    e_rem = [jnp.exp(ltot[i] - lp[i]) for i in cks]
    bh = [b[i] * e_rem[i] for i in cks]
    kh = [kd[i] * e_rem[i] for i in cks]

    lhs = [jnp.concatenate([kap[i], rt[i]], axis=0).astype(BF16) for i in cks]
    gram_b = [lax.dot_general(lhs[i], expand_b(bt[i]), nt, preferred_element_type=F32) for i in cks]
    gram_k = [lax.dot_general(lhs[i], expand_b(kt[i]), nt, preferred_element_type=F32) for i in cks]
    mb_b = [(gram_b[i][0:C] * strict).astype(BF16) for i in cks]
    mrb = [gram_b[i][C:2 * C] * incl for i in cks]
    mk = [gram_k[i][0:C] * strict for i in cks]
    mrk = [gram_k[i][C:2 * C] * incl for i in cks]

    inv = [eye - (mb_b[i] * lev_ref[0]).astype(F32) for i in cks]
    for lv in range(1, lev_ref.shape[0]):
        inv_b = [inv[i].astype(BF16) for i in cks]
        step = [_dot(inv_b[i], expand_b(mb_b[i] * lev_ref[lv])) for i in cks]
        inv = [inv[i] - _dot(step[i].astype(BF16), expand_b(inv_b[i])) for i in cks]

    mv = [_dot(jnp.concatenate([mk[i], mrk[i]], axis=0).astype(BF16), expand_b(v[i])) for i in cks]
    wu_b = [_dot(inv[i].astype(BF16),
                 jnp.concatenate([expand_b(kap[i]), expand_b(mv[i][0:C])], axis=1)).astype(BF16)
            for i in cks]
    mw = [_dot(mrb[i].astype(BF16),
               jnp.concatenate([expand_b(wu_b[i][:, 0:n]), expand_b(wu_b[i][:, n:2 * n])], axis=1))
          for i in cks]
    bw = [lax.dot_general(bh[i].astype(BF16), wu_b[i], tn, preferred_element_type=F32) for i in cks]
    kv = [lax.dot_general(kh[i].astype(BF16), v[i].astype(BF16), tn, preferred_element_type=F32) for i in cks]
    for i in cks:
        rh_ref[0, 0, rows[i], :] = rt[i] - mw[i][:, 0:n]
        yh_ref[0, 0, rows[i], :] = mv[i][C:2 * C] - mw[i][:, n:2 * n]
        g_ref[0, 0, i] = eye * jnp.exp(ltot[i]) - _contract(bw[i][:, 0:n] * bd)
        hh_ref[0, 0, i] = _contract((kv[i] - bw[i][:, n:2 * n]) * bd)


def _rwkv_local_call(shared, dirs):
    nb, t, _ = shared.shape
    C = RWKV_CHUNK
    nch = t // C
    nck = 4 if nch % 4 == 0 else (2 if nch % 2 == 0 else 1)
    tm = nck * C
    n = 4 * C
    bd, eye, dirm, lev, incl1 = _rwkv_masks()
    kern = functools.partial(_rwkv_local_kernel, nck=nck)
    mat = jax.ShapeDtypeStruct((nb, 2, nch, C, 256), F32)
    seq = jax.ShapeDtypeStruct((nb, 2, t, 256), F32)
    mat_spec = pl.BlockSpec((1, 1, nck, C, 256), lambda b, d, i: (b, d, i, 0, 0))
    seq_spec = pl.BlockSpec((1, 1, tm, 256), lambda b, d, i: (b, d, i, 0))
    return pl.pallas_call(
        kern,
        out_shape=[mat, mat, seq, seq],
        grid=(nb, 2, nch // nck),
        in_specs=[pl.BlockSpec((1, tm, 768), lambda b, d, i: (b, i, 0)),
                  pl.BlockSpec((1, 1, tm, 768), lambda b, d, i: (b, d, i, 0)),
                  pl.BlockSpec(bd.shape, lambda b, d, i: (0, 0)),
                  pl.BlockSpec(eye.shape, lambda b, d, i: (0, 0)),
                  pl.BlockSpec((1, 2, C, n), lambda b, d, i: (d, 0, 0, 0)),
                  pl.BlockSpec(lev.shape, lambda b, d, i: (0, 0, 0)),
                  pl.BlockSpec((1, C, C), lambda b, d, i: (d, 0, 0))],
        out_specs=[mat_spec, mat_spec, seq_spec, seq_spec],
        compiler_params=_cparams(("arbitrary", "arbitrary", "arbitrary")),
        name="rwkv_local",
    )(shared, dirs, bd, eye, dirm, lev, incl1)


def _rwkv_scan_kernel(gf_ref, hf_ref, rf_ref, yf_ref, gb_ref, hb_ref, rb_ref, yb_ref,
                      of_ref, ob_ref, st_ref, *, nb):
    C = RWKV_CHUNK
    n = 4 * C
    j = pl.program_id(0)

    @pl.when(j == 0)
    def _():
        st_ref[...] = jnp.zeros(st_ref.shape, F32)

    rr = _iota((n, n), 0)
    cc = _iota((n, n), 1)
    bd = (rr >> 6) == (cc >> 6)

    def step(d, b, g_ref, h_ref, r_ref, y_ref, o_ref):
        sh_, sm_, _ = _split3(st_ref[d, b])
        rh_, rm_, _ = _split3(_expand(r_ref[b, 0], bd))
        y = _dot(rh_, sh_) + (_dot(rh_, sm_) + _dot(rm_, sh_))
        o_ref[b] = _contract(y) + y_ref[b, 0]
        gh_, gm_, _ = _split3(_expand(g_ref[b, 0, 0], bd))
        new = _dot(gh_, sh_) + (_dot(gh_, sm_) + _dot(gm_, sh_))
        st_ref[d, b] = new + _expand(h_ref[b, 0, 0], bd)

    def body(b, carry):
        step(0, b, gf_ref, hf_ref, rf_ref, yf_ref, of_ref)
        step(1, b, gb_ref, hb_ref, rb_ref, yb_ref, ob_ref)
        return carry

    lax.fori_loop(0, nb, body, 0)


def _rwkv_scan_call(g, hh, rh, yh, lc):
    nb, _, nch, C, _ = g.shape
    t = nch * C
    ncc = lc // C

    def bwd(j):
        return jnp.where(j < ncc, ncc - 1 - j, nch - 1 + ncc - j)

    def mat_spec(d):
        if d == 0:
            return pl.BlockSpec((nb, 1, 1, C, 256), lambda j: (0, 0, j, 0, 0))
        return pl.BlockSpec((nb, 1, 1, C, 256), lambda j: (0, 1, bwd(j), 0, 0))

    def seq_spec(d):
        if d == 0:
            return pl.BlockSpec((nb, 1, C, 256), lambda j: (0, 0, j, 0))
        return pl.BlockSpec((nb, 1, C, 256), lambda j: (0, 1, bwd(j), 0))

    kern = functools.partial(_rwkv_scan_kernel, nb=nb)
    out = jax.ShapeDtypeStruct((nb, t, 256), F32)
    specs = []
    for d in range(2):
        specs += [mat_spec(d), mat_spec(d), seq_spec(d), seq_spec(d)]
    of, ob = pl.pallas_call(
        kern,
        out_shape=[out, out],
        grid=(nch,),
        in_specs=specs,
        out_specs=[pl.BlockSpec((nb, C, 256), lambda j: (0, j, 0)),
                   pl.BlockSpec((nb, C, 256), lambda j: (0, bwd(j), 0))],
        scratch_shapes=[pltpu.VMEM((2, nb, 4 * C, 256), F32)],
        compiler_params=_cparams(("arbitrary",)),
        name="rwkv_scan",
    )(g, hh, rh, yh, g, hh, rh, yh)
    return of, ob


def _merge_kernel(x_ref, mod_ref, oa_ref, oc_ref, od_ref, yf_ref, yb_ref, bon_ref, sg_ref, mg_ref,
                  bw_ref, ow_ref, mb_ref, gnw_ref, gnb_ref, lng_ref, lnb_ref, o_ref,
                  *, lc, tm, nb, row_off, alpha):
    b = pl.program_id(0)
    i = pl.program_id(1)

    def pair_cat(ref):
        return jnp.concatenate([ref[0, 0], ref[0, 1]], axis=1)

    def per_head(fn, x):
        return jnp.concatenate([fn(x[:, 0:LANES]), fn(x[:, LANES:2 * LANES])], axis=1)

    y = yf_ref[0] + yb_ref[0]
    mu = per_head(_half_sums, y) * (1.0 / RWKV_N)
    yc = y - mu
    var = per_head(_half_sums, yc * yc) * (1.0 / RWKV_N)
    ob = yc * lax.rsqrt(var + RWKV_GN_EPS) * gnw_ref[...] + gnb_ref[...] + bon_ref[0]

    outs = (pair_cat(oa_ref), ob, pair_cat(oc_ref), pair_cat(od_ref))
    acc = None
    for k in range(N_BRANCH):
        sg = sg_ref[0, :, k * BRANCH_W:(k + 1) * BRANCH_W].astype(F32)
        u = outs[k] * (sg * jax.nn.sigmoid(sg))
        z = _dot(u.astype(BF16), bw_ref[k])
        gate = jax.nn.sigmoid(mg_ref[0, :, k * D_MODEL:(k + 1) * D_MODEL].astype(F32)
                              + mb_ref[:, k * D_MODEL:(k + 1) * D_MODEL])
        acc = gate * z if acc is None else acc + gate * z
    y2 = _dot(acc.astype(BF16), ow_ref[...])

    row = row_off + i * tm + _iota((tm, D_MODEL), 0)
    gl = mod_ref[pl.ds(b, 1), 2 * D_MODEL:3 * D_MODEL]
    gc = mod_ref[pl.ds(nb, 1), 2 * D_MODEL:3 * D_MODEL]
    h = alpha * x_ref[0] + jnp.where(row < lc, gc, gl) * y2
    m = jnp.mean(h, -1, keepdims=True)
    hc = h - m
    var = jnp.mean(hc * hc, -1, keepdims=True)
    o_ref[0] = hc * lax.rsqrt(var + 1e-5) * lng_ref[...] + lnb_ref[...]


def _merge_call(xx, mod, oa, oc, od, yf, yb, bon, p, bw, ow, mb, gnw, gnb, lng, lnb, *, lc, row_off, alpha):
    nb, t, _ = xx.shape
    nrows = t - row_off
    tm = _pick_tile(math.gcd(nrows, row_off) if row_off else nrows, 256, 16)
    ro = row_off // tm
    kern = functools.partial(_merge_kernel, lc=lc, tm=tm, nb=nb, row_off=row_off, alpha=alpha)

    def full(a):
        return pl.BlockSpec(a.shape, lambda b, i: (0,) * a.ndim)

    pair_spec = pl.BlockSpec((1, 2, tm, LANES), lambda b, i: (b, 0, i, 0))
    return pl.pallas_call(
        kern,
        out_shape=jax.ShapeDtypeStruct((nb, nrows, D_MODEL), F32),
        grid=(nb, nrows // tm),
        in_specs=[
            pl.BlockSpec((1, tm, D_MODEL), lambda b, i: (b, i + ro, 0)),
            full(mod), pair_spec, pair_spec, pair_spec,
            pl.BlockSpec((1, tm, 256), lambda b, i: (b, i + ro, 0)),
            pl.BlockSpec((1, tm, 256), lambda b, i: (b, i + ro, 0)),
            pl.BlockSpec((1, tm, 256), lambda b, i: (b, i + ro, 0)),
            pl.BlockSpec((1, tm, D_MODEL), lambda b, i: (b, i + ro, P_SILU // D_MODEL)),
            pl.BlockSpec((1, tm, N_BRANCH * D_MODEL), lambda b, i: (b, i + ro, P_MERGE // (N_BRANCH * D_MODEL))),
            full(bw), full(ow), full(mb), full(gnw), full(gnb), full(lng), full(lnb),
        ],
        out_specs=pl.BlockSpec((1, tm, D_MODEL), lambda b, i: (b, i, 0)),
        compiler_params=_cparams(("arbitrary", "arbitrary")),
        name="merge",
    )(xx, mod, oa, oc, od, yf, yb, bon, p, p, bw, ow, mb, gnw, gnb, lng, lnb)


def _pack_in_w(w):
    z = lambda n: jnp.zeros((D_MODEL, n), w.dtype)
    a0 = 0
    b0 = 672
    c0 = b0 + 1280
    d0 = c0 + 768
    m0 = d0 + 1024
    cols = [
        w[:, a0:a0 + 256], w[:, a0 + 256:a0 + 384],
        z(64), w[:, a0 + 384:a0 + 416], z(32),
        w[:, c0:c0 + 256], w[:, c0 + 256:c0 + 384], w[:, c0 + 384:c0 + 512],
        w[:, d0:d0 + 256], w[:, d0 + 256:d0 + 512], w[:, d0 + 512:d0 + 768],
        z(256),
        w[:, b0:b0 + 1024],
        w[:, a0 + 416:a0 + 672], w[:, b0 + 1024:b0 + 1280], w[:, c0 + 512:c0 + 768], w[:, d0 + 768:d0 + 1024],
        w[:, m0:m0 + 4096],
    ]
    out = jnp.concatenate(cols, axis=1)
    assert out.shape[1] == P_WIDTH
    return out.astype(BF16)


def _pack_mla(w_uq, w_ukv):
    zq = jnp.zeros((MLA_Q_LORA, LANES - MLA_NOPE - MLA_ROPE), w_uq.dtype)
    zk = jnp.zeros((MLA_KV_LORA, LANES - MLA_NOPE), w_ukv.dtype)
    zv = jnp.zeros((MLA_KV_LORA, MLA_V), w_ukv.dtype)
    qc, kc, vc = [], [], []
    for h in range(MLA_HEADS):
        qc += [w_uq[:, h * 96:(h + 1) * 96], zq]
        kc += [w_ukv[:, h * 128:h * 128 + 64], zk]
        vh = w_ukv[:, h * 128 + 64:(h + 1) * 128]
        vc += [vh, zv] if h % 2 == 0 else [zv, vh]
    cat = lambda xs: jnp.concatenate(xs, axis=1).astype(BF16)
    return cat(qc), cat(kc), cat(vc)


def _rope_tables(row, col, rot_dim, lc, pattern):
    quarter = rot_dim // 4
    inv_freq = ROPE_BASE ** (-jnp.arange(quarter, dtype=F32) / quarter)
    ang = jnp.concatenate([row[:, None] * inv_freq, col[:, None] * inv_freq], axis=-1)
    cos, sin = jnp.cos(ang), jnp.sin(ang)
    n = ang.shape[0]
    ones = lambda w: jnp.ones((n, w), F32)
    zeros = lambda w: jnp.zeros((n, w), F32)
    if pattern == "mla":
        c = jnp.concatenate([ones(64), cos, cos, ones(32)], axis=1)
        s = jnp.concatenate([zeros(64), -sin, sin, zeros(32)], axis=1)
    else:
        reps = LANES // rot_dim
        c = jnp.concatenate([cos, cos] * reps, axis=1)
        s = jnp.concatenate([-sin, sin] * reps, axis=1)
    c = jnp.concatenate([jnp.ones((lc, LANES), F32), c], axis=0)
    s = jnp.concatenate([jnp.zeros((lc, LANES), F32), s], axis=0)
    return c, s


def kernel(x, c, ctx, c_ctx, ada_w, ada_b, in_w, mla_q_norm, mla_w_uq, mla_kv_norm, mla_w_ukv, rwkv_mu, rwkv_w0, rwkv_w_up, rwkv_a0, rwkv_a_up, rwkv_k_k, rwkv_k_a, rwkv_r_k, rwkv_gn_w, rwkv_gn_b, gqa_q_norm, gqa_k_norm, diff_lambda, diff_subln, merge_b, branch_w, out_w, ln_g, ln_b):
    nb, ll, _ = x.shape
    lc = ctx.shape[1]
    depth = ada_w.shape[0]
    t = lc + ll
    alpha = (2 * depth) ** 0.25

    rows = ll // GRID_W
    row = jnp.repeat(jnp.arange(rows), GRID_W).astype(F32)
    col = jnp.tile(jnp.arange(GRID_W), rows).astype(F32)
    tabs = (_rope_tables(row, col, MLA_ROPE, lc, "mla") + _rope_tables(row, col, HEAD_DIM, lc, "tile")
            + _rope_tables(row, col, DIFF_D, lc, "tile"))

    crows = -(-(nb + 1) // 8) * 8
    cvec = jnp.concatenate([c, c_ctx[None, :], jnp.zeros((crows - nb - 1, D_MODEL), F32)], axis=0)
    mods = _ada_call(cvec, ada_w, ada_b)

    xx = jnp.concatenate([ctx, x], axis=1)
    tile2 = lambda a: jnp.concatenate([a, a])[None, :]
    for l in range(depth):
        last = l == depth - 1
        lam_init = 0.8 - 0.6 * math.exp(-0.3 * l)
        mod = mods[l]
        p = _inproj_call(xx, mod, _pack_in_w(in_w[l]), lc)

        wuq, wk, wv = _pack_mla(mla_w_uq[l], mla_w_ukv[l])
        qa, ka, va, qc, kc, vc, qd, kd, vd = _prep_call(
            p, tabs, (mla_q_norm[l][None, :], wuq, mla_kv_norm[l][None, :], wk, wv,
                      tile2(gqa_q_norm[l]), tile2(gqa_k_norm[l])))
        lam_p = jnp.zeros((8, LANES), F32).at[0:4, 0:DIFF_D].set(diff_lambda[l])
        sub = tile2(diff_subln[l])
        att = lambda q, k, v, **kw: _attn_call(q, k, v, lk=t, q_off=lc, nq_rows=ll, **kw)
        oa = att(qa, ka, va, n_maps=1, shared_k=False)
        oc = att(qc, kc, vc, n_maps=1, shared_k=True)
        od = att(qd, kd, vd, n_maps=2, shared_k=False, extra=(lam_p, sub), lam_init=lam_init)
        if not last:
            catt = lambda q, k, v, **kw: _attn_call(q, k, v, lk=lc, q_off=0, nq_rows=lc, **kw)
            merge_rows = lambda lat_o, ctx_o: jnp.concatenate([ctx_o, lat_o], axis=2)
            oa = merge_rows(oa, catt(qa, ka, va, n_maps=1, shared_k=False))
            oc = merge_rows(oc, catt(qc, kc, vc, n_maps=1, shared_k=True))
            od = merge_rows(od, catt(qd, kd, vd, n_maps=2, shared_k=False, extra=(lam_p, sub), lam_init=lam_init))

        wl = jnp.zeros((256, 1024), F32)
        for d in range(2):
            wl = wl.at[d * 64:(d + 1) * 64, d * 256:(d + 1) * 256].set(rwkv_w_up[l, d])
            wl = wl.at[128 + d * 64:128 + (d + 1) * 64, 512 + d * 256:512 + (d + 1) * 256].set(rwkv_a_up[l, d])
        bias = jnp.concatenate([rwkv_w0[l, 0], rwkv_w0[l, 1], rwkv_a0[l, 0], rwkv_a0[l, 1]])[None, :]
        shared, dirs, bon = _rwkv_prep_call(p, rwkv_mu[l], wl.astype(BF16), bias, rwkv_k_k[l][None, :],
                                            rwkv_k_a[l][None, :], rwkv_r_k[l][None, :], lc)
        g, hh, rh, yh = _rwkv_local_call(shared, dirs)
        yf, yb = _rwkv_scan_call(g, hh, rh, yh, lc)

        xx = _merge_call(xx, mod, oa, oc, od, yf, yb, bon, p,
                         branch_w[l].astype(BF16), out_w[l].astype(BF16), merge_b[l][None, :],
                         rwkv_gn_w[l][None, :], rwkv_gn_b[l][None, :], ln_g[l][None, :], ln_b[l][None, :],
                         lc=lc, row_off=lc if last else 0, alpha=alpha)
    return xx
```

```python
import functools
import math

import jax
import jax.numpy as jnp
import numpy as np
from jax import lax
from jax.experimental import pallas as pl
from jax.experimental.pallas import tpu as pltpu

F32 = jnp.float32
BF16 = jnp.bfloat16

D_MODEL = 1024
GRID_W = 64
ROPE_BASE = 10000.0
HEAD_DIM = 64
N_BRANCH = 4
BRANCH_W = 256
MLA_HEADS = 4
MLA_Q_LORA = 256
MLA_KV_LORA = 128
MLA_NOPE = 64
MLA_ROPE = 32
MLA_V = 64
MLA_SCALE = (MLA_NOPE + MLA_ROPE) ** -0.5
RWKV_N = 64
RWKV_GN_EPS = 64e-5
RWKV_DECAY_SCALE = math.exp(-0.5)
GQA_SCALE = HEAD_DIM ** -0.5
DIFF_D = 32
DIFF_SCALE = DIFF_D ** -0.5

LANES = 128
VMEM_LIMIT = 56 * 1024 * 1024

P_AQ, P_AKV, P_AKR = 0, 256, 384
P_CQ, P_CK, P_CV = 512, 768, 896
P_DQ, P_DK, P_DV = 1024, 1280, 1536
P_RWKV = 2048
P_SILU = 3072
P_MERGE = 4096
P_WIDTH = 8192
ATTN_W = 2048

RWKV_CHUNK = 64
ATTN_TQ = 256


def _cparams(sem):
    return pltpu.CompilerParams(dimension_semantics=sem, vmem_limit_bytes=VMEM_LIMIT)


def _split3(x):
    h = x.astype(BF16)
    r = x - h.astype(F32)
    m = r.astype(BF16)
    l = (r - m.astype(F32)).astype(BF16)
    return h, m, l


def _dot(a, b):
    return jnp.dot(a, b, preferred_element_type=F32)


def _dot_hi(a, b):
    ah, am, _ = _split3(a)
    bh, bm, _ = _split3(b)
    return _dot(ah, bh) + (_dot(ah, bm) + _dot(am, bh))


def _dot_exact_lhs(a_bf16, b):
    bh, bm, bl = _split3(b)
    return _dot(a_bf16, bh) + (_dot(a_bf16, bm) + _dot(a_bf16, bl))


def _iota(shape, dim):
    return lax.broadcasted_iota(jnp.int32, shape, dim)


def _ada_kernel(c_ref, w_ref, b_ref, o_ref):
    c = c_ref[...]
    s = c * jax.nn.sigmoid(c)
    o_ref[0] = _dot_hi(s, w_ref[0]) + b_ref[0]


def _ada_call(cvec, ada_w, ada_b):
    depth = ada_w.shape[0]
    rows = cvec.shape[0]
    nblk = 3
    return pl.pallas_call(
        _ada_kernel,
        out_shape=jax.ShapeDtypeStruct((depth, rows, 3 * D_MODEL), F32),
        grid=(depth, nblk),
        in_specs=[
            pl.BlockSpec((rows, D_MODEL), lambda l, j: (0, 0)),
            pl.BlockSpec((1, D_MODEL, D_MODEL), lambda l, j: (l, 0, j)),
            pl.BlockSpec((1, 1, D_MODEL), lambda l, j: (l, 0, j)),
        ],
        out_specs=pl.BlockSpec((1, rows, D_MODEL), lambda l, j: (l, 0, j)),
        compiler_params=_cparams(("arbitrary", "arbitrary")),
        name="ada",
    )(cvec, ada_w, ada_b.reshape(depth, 1, 3 * D_MODEL))


def _inproj_kernel(x_ref, mod_ref, w_ref, o_ref, xm_ref, *, lc, tm, nb):
    b = pl.program_id(0)
    i = pl.program_id(1)
    j = pl.program_id(2)

    @pl.when(j == 0)
    def _():
        x = x_ref[0]
        mu = jnp.mean(x, -1, keepdims=True)
        xc = x - mu
        var = jnp.mean(xc * xc, -1, keepdims=True)
        xn = xc * lax.rsqrt(var + 1e-6)
        row = i * tm + _iota((tm, D_MODEL), 0)
        is_ctx = row < lc
        ml = mod_ref[pl.ds(b, 1), :]
        mc = mod_ref[pl.ds(nb, 1), :]
        shift = jnp.where(is_ctx, mc[:, 0:D_MODEL], ml[:, 0:D_MODEL])
        scale = jnp.where(is_ctx, mc[:, D_MODEL:2 * D_MODEL], ml[:, D_MODEL:2 * D_MODEL])
        xm_ref[...] = (xn * (1.0 + scale) + shift).astype(BF16)

    o_ref[0] = _dot(xm_ref[...], w_ref[...]).astype(BF16)


def _inproj_call(xx, mod, w_packed, lc):
    nb, t, _ = xx.shape
    tm = _pick_tile(t, 1088, 16)
    tn = 1024
    kern = functools.partial(_inproj_kernel, lc=lc, tm=tm, nb=nb)
    return pl.pallas_call(
        kern,
        out_shape=jax.ShapeDtypeStruct((nb, t, P_WIDTH), BF16),
        grid=(nb, t // tm, P_WIDTH // tn),
        in_specs=[
            pl.BlockSpec((1, tm, D_MODEL), lambda b, i, j: (b, i, 0)),
            pl.BlockSpec(mod.shape, lambda b, i, j: (0, 0)),
            pl.BlockSpec((D_MODEL, tn), lambda b, i, j: (0, j)),
        ],
        out_specs=pl.BlockSpec((1, tm, tn), lambda b, i, j: (b, i, j)),
        scratch_shapes=[pltpu.VMEM((tm, D_MODEL), BF16)],
        compiler_params=_cparams(("arbitrary", "arbitrary", "arbitrary")),
        name="inproj",
    )(xx, mod, w_packed)


def _pick_tile(n, cap, mult):
    best = None
    for d in range(mult, min(n, cap) + 1, mult):
        if n % d == 0:
            best = d
    assert best is not None, (n, cap, mult)
    return best


def _swap_halves(x, half):
    n = x.shape[-1]
    first = (_iota(x.shape, 1) & (2 * half - 1)) < half
    up = pltpu.roll(x, n - half, 1)
    dn = pltpu.roll(x, half, 1)
    return jnp.where(first, up, dn)


def _half_sums(x):
    lo = _iota(x.shape, 1) < HEAD_DIM
    s_lo = jnp.sum(jnp.where(lo, x, 0.0), -1, keepdims=True)
    s_hi = jnp.sum(jnp.where(lo, 0.0, x), -1, keepdims=True)
    return jnp.where(lo, s_lo, s_hi)


def _prep_kernel(p_ref, cosa_ref, sina_ref, cosc_ref, sinc_ref, cosd_ref, sind_ref,
                 qn_ref, wuq_ref, kvn_ref, wk_ref, wv_ref, gq_ref, gk_ref,
                 qa_ref, ka_ref, va_ref, qc_ref, kc_ref, vc_ref, qd_ref, kd_ref, vd_ref):
    def seg(off, w):
        return p_ref[0, :, off:off + w].astype(F32)

    lane = _iota((p_ref.shape[1], LANES), 1)
    lo = lane < HEAD_DIM

    cosa, sina = cosa_ref[...], sina_ref[...]
    ql = seg(P_AQ, MLA_Q_LORA)
    ql = ql * lax.rsqrt(jnp.mean(ql * ql, -1, keepdims=True) + 1e-6) * qn_ref[...]
    q = _dot(ql.astype(BF16), wuq_ref[...])
    kvl = seg(P_AKV, MLA_KV_LORA)
    kvl = (kvl * lax.rsqrt(jnp.mean(kvl * kvl, -1, keepdims=True) + 1e-6) * kvn_ref[...]).astype(BF16)
    kn = _dot(kvl, wk_ref[...])
    vv = _dot(kvl, wv_ref[...])
    kr = seg(P_AKR, LANES)
    kr = kr * cosa + _swap_halves(kr, MLA_ROPE // 2) * sina
    for h in range(MLA_HEADS):
        qh = q[:, h * LANES:(h + 1) * LANES]
        qh = qh * cosa + _swap_halves(qh, MLA_ROPE // 2) * sina
        qa_ref[0, h] = (qh * MLA_SCALE).astype(BF16)
        ka_ref[0, h] = (kn[:, h * LANES:(h + 1) * LANES] + kr).astype(BF16)
        va_ref[0, h] = vv[:, h * LANES:(h + 1) * LANES].T.astype(BF16)

    cosc, sinc = cosc_ref[...], sinc_ref[...]

    def norm_rope(x, g):
        x = x * lax.rsqrt(_half_sums(x * x) * (1.0 / HEAD_DIM) + 1e-6) * g
        return x * cosc + _swap_halves(x, HEAD_DIM // 2) * sinc

    def split_heads(blk):
        return jnp.where(lo, blk, 0.0), jnp.where(lo, pltpu.roll(blk, HEAD_DIM, 1), 0.0)

    for pr in range(2):
        qb = norm_rope(seg(P_CQ + pr * LANES, LANES), gq_ref[...]) * GQA_SCALE
        q0, q1 = split_heads(qb)
        qc_ref[0, 2 * pr] = q0.astype(BF16)
        qc_ref[0, 2 * pr + 1] = q1.astype(BF16)
    k0, k1 = split_heads(norm_rope(seg(P_CK, LANES), gk_ref[...]))
    kc_ref[0, 0] = k0.astype(BF16)
    kc_ref[0, 1] = k1.astype(BF16)
    vb = seg(P_CV, LANES)
    v_g0 = jnp.where(lo, vb, 0.0)
    v_g1 = jnp.where(lo, 0.0, vb)
    vc_ref[0, 0] = v_g0.T.astype(BF16)
    vc_ref[0, 1] = pltpu.roll(v_g0, HEAD_DIM, 1).T.astype(BF16)
    vc_ref[0, 2] = pltpu.roll(v_g1, HEAD_DIM, 1).T.astype(BF16)
    vc_ref[0, 3] = v_g1.T.astype(BF16)

    cosd, sind = cosd_ref[...], sind_ref[...]
    piece = lane < DIFF_D
    for pr in range(2):
        qb = seg(P_DQ + pr * LANES, LANES)
        qb = (qb * cosd + _swap_halves(qb, DIFF_D // 2) * sind) * DIFF_SCALE
        kb = seg(P_DK + pr * LANES, LANES)
        kb = kb * cosd + _swap_halves(kb, DIFF_D // 2) * sind
        vb = seg(P_DV + pr * LANES, LANES)
        for hh in range(2):
            for m in range(2):
                off = hh * HEAD_DIM + m * DIFF_D
                idx = (2 * pr + hh) * 2 + m
                qs = qb if off == 0 else pltpu.roll(qb, LANES - off, 1)
                ks = kb if off == 0 else pltpu.roll(kb, LANES - off, 1)
                qd_ref[0, idx] = jnp.where(piece, qs, 0.0).astype(BF16)
                kd_ref[0, idx] = jnp.where(piece, ks, 0.0).astype(BF16)
        vd_ref[0, 2 * pr] = jnp.where(lo, vb, 0.0).T.astype(BF16)
        vd_ref[0, 2 * pr + 1] = jnp.where(lo, 0.0, vb).T.astype(BF16)


def _prep_call(p, tabs, wts):
    nb, t, _ = p.shape
    tm = _pick_tile(t, 256, 16)
    row_spec = pl.BlockSpec((tm, LANES), lambda b, i: (i, 0))

    def full(a):
        return pl.BlockSpec(a.shape, lambda b, i: (0,) * a.ndim)

    def head_out(nh):
        return (jax.ShapeDtypeStruct((nb, nh, t, LANES), BF16),
                pl.BlockSpec((1, nh, tm, LANES), lambda b, i: (b, 0, i, 0)))

    def head_out_t(nh):
        return (jax.ShapeDtypeStruct((nb, nh, LANES, t), BF16),
                pl.BlockSpec((1, nh, LANES, tm), lambda b, i: (b, 0, 0, i)))

    outs = [head_out(4), head_out(4), head_out_t(4), head_out(4), head_out(2), head_out_t(4),
            head_out(8), head_out(8), head_out_t(4)]
    return pl.pallas_call(
        _prep_kernel,
        out_shape=[o[0] for o in outs],
        grid=(nb, t // tm),
        in_specs=[pl.BlockSpec((1, tm, ATTN_W), lambda b, i: (b, i, 0))]
        + [row_spec] * 6 + [full(w) for w in wts],
        out_specs=[o[1] for o in outs],
        compiler_params=_cparams(("arbitrary", "arbitrary")),
        name="attn_prep",
    )(p, *tabs, *wts)


def _attn_kernel(*refs, hp, n_maps, shared_k, lk, tk, lam_init):
    if n_maps == 2:
        q_ref, k_ref, vt_ref, lam_ref, sub_ref, o_ref, s_scr, p_scr = refs
    else:
        q_ref, k_ref, vt_ref, o_ref, s_scr, p_scr = refs
    nchunks = lk // tk
    maps = [(hh * n_maps + mm, hh // 2 if shared_k else hh * n_maps + mm, hh)
            for hh in range(hp) for mm in range(n_maps)]

    def scores(j):
        qi, ki, _ = maps[j]
        s_scr[j % 2] = lax.dot_general(k_ref[0, ki], q_ref[0, qi], (((1,), (1,)), ((), ())),
                                       preferred_element_type=F32)

    def softmax_rows(j):
        buf = j % 2
        mx = None
        for c in range(nchunks):
            sc = s_scr[buf, c * tk:(c + 1) * tk, :]
            mx = sc if mx is None else jnp.maximum(mx, sc)
        m = jnp.max(mx, 0, keepdims=True)
        ls = None
        for c in range(nchunks):
            p = jnp.exp(s_scr[buf, c * tk:(c + 1) * tk, :] - m)
            p_scr[buf, c * tk:(c + 1) * tk, :] = p.astype(BF16)
            ls = p if ls is None else ls + p
        return jnp.sum(ls, 0, keepdims=True)

    def weighted_values(j, l):
        ot = _dot(vt_ref[0, maps[j][2]], p_scr[j % 2])
        return (ot / l).T

    outs = []
    scores(0)
    for j in range(len(maps)):
        if j + 1 < len(maps):
            scores(j + 1)
        outs.append(weighted_values(j, softmax_rows(j)))

    for pr in range(hp // 2):
        total = None
        for hh in (2 * pr, 2 * pr + 1):
            o = outs[hh * n_maps]
            if n_maps == 2:
                lp = lam_ref[...]
                lam = (jnp.exp(jnp.sum(lp[0:1] * lp[1:2], keepdims=True))
                       - jnp.exp(jnp.sum(lp[2:3] * lp[3:4], keepdims=True)) + lam_init)
                o = o - lam * outs[hh * n_maps + 1]
                ms = jnp.sum(o * o, -1, keepdims=True) * (1.0 / HEAD_DIM)
                o = o * lax.rsqrt(ms + 1e-5) * sub_ref[...] * (1.0 - lam_init)
            total = o if total is None else total + o
        o_ref[0, pr] = total


def _attn_call(q, k, v, *, n_maps, shared_k, lk, q_off, nq_rows, extra=(), lam_init=0.0):
    nb = q.shape[0]
    hp = 4 // n_maps
    tq = _pick_tile(nq_rows, ATTN_TQ, 16)
    tk = 256 if lk % 256 == 0 else _pick_tile(lk, 256, LANES)
    kh = hp // 2 if shared_k else hp * n_maps
    kern = functools.partial(_attn_kernel, hp=hp, n_maps=n_maps, shared_k=shared_k, lk=lk, tk=tk,
                             lam_init=lam_init)
    in_specs = [
        pl.BlockSpec((pl.Element(1), pl.Element(hp * n_maps), pl.Element(tq), pl.Element(LANES)),
                     lambda b, p, i: (b, p * hp * n_maps, pl.multiple_of(i * tq + q_off, 16), 0)),
        pl.BlockSpec((1, kh, lk, LANES), lambda b, p, i: (b, p, 0, 0)),
        pl.BlockSpec((1, hp, LANES, lk), lambda b, p, i: (b, p, 0, 0)),
    ] + [pl.BlockSpec(e.shape, lambda b, p, i: (0, 0)) for e in extra]
    return pl.pallas_call(
        kern,
        out_shape=jax.ShapeDtypeStruct((nb, 2, nq_rows, LANES), F32),
        grid=(nb, 4 // hp, nq_rows // tq),
        in_specs=in_specs,
        out_specs=pl.BlockSpec((1, hp // 2, tq, LANES), lambda b, p, i: (b, p, i, 0)),
        scratch_shapes=[pltpu.VMEM((2, lk, tq), F32), pltpu.VMEM((2, lk, tq), BF16)],
        compiler_params=_cparams(("arbitrary", "arbitrary", "arbitrary")),
        name="attn",
    )(q, k, v, *extra)


def _rwkv_prep_kernel(cur_ref, prv_ref, nxt_ref, mu_ref, wl_ref, bias_ref, kk_ref, ka_ref, rk_ref,
                      sh_ref, dr_ref, bon_ref, *, lc, t_total, tm, halo):
    i = pl.program_id(1)
    cur = cur_ref[0]
    r_i = _iota((tm, tm), 0)
    c_i = _iota((tm, tm), 1)
    s_dn = jnp.where(c_i == r_i - 1, 1.0, 0.0).astype(BF16)
    s_up = jnp.where(c_i == r_i + 1, 1.0, 0.0).astype(BF16)
    row = _iota((tm, D_MODEL), 0)
    tg = i * tm + row
    prv = jnp.where(row == 0, prv_ref[0].astype(F32)[halo - 1:halo, :], _dot(s_dn, cur))
    nxt = jnp.where(row == tm - 1, nxt_ref[0].astype(F32)[0:1, :], _dot(s_up, cur))
    prv = jnp.where((tg == 0) | (tg == lc), 0.0, prv)
    nxt = jnp.where((tg == lc - 1) | (tg == t_total - 1), 0.0, nxt)
    x = cur.astype(F32)
    sh = x + mu_ref[0:1, :] * (prv - x) + mu_ref[1:2, :] * (nxt - x)

    r = sh[:, 0:256]
    k = sh[:, 256:512]
    v = sh[:, 512:768]
    lora = sh[:, 768:1024]
    lora = jnp.where(_iota(lora.shape, 1) < 2 * RWKV_N, jnp.tanh(lora), lora)
    z = _dot(lora.astype(BF16), wl_ref[...]) + bias_ref[...]

    def per_head(fn, x):
        return jnp.concatenate([fn(x[:, 0:LANES]), fn(x[:, LANES:2 * LANES])], axis=1)

    kq = k * kk_ref[...]
    kk = kq * lax.rsqrt(per_head(_half_sums, kq * kq) + 1e-12)
    sh_ref[0, :, 0:256] = r
    sh_ref[0, :, 256:512] = v
    sh_ref[0, :, 512:768] = kk
    hs = None
    for d in range(2):
        logw = -RWKV_DECAY_SCALE * jax.nn.sigmoid(z[:, d * 256:(d + 1) * 256])
        a = jax.nn.sigmoid(z[:, 512 + d * 256:512 + (d + 1) * 256])
        kd = k * (1.0 + (a - 1.0) * ka_ref[...])
        dr_ref[0, d, :, 0:256] = logw
        dr_ref[0, d, :, 256:512] = kd
        dr_ref[0, d, :, 512:768] = a * kk
        s = per_head(_half_sums, r * kd * rk_ref[...])
        hs = s if hs is None else hs + s
    bon_ref[0] = hs * v


def _rwkv_prep_call(p, mu, wl, bias, k_k, k_a, r_k, lc):
    nb, t, _ = p.shape
    tm = _pick_tile(t, 256, 16)
    halo = 16
    cb = P_RWKV // D_MODEL
    hb = tm // halo
    last = t // halo - 1
    kern = functools.partial(_rwkv_prep_kernel, lc=lc, t_total=t, tm=tm, halo=halo)

    def full(a):
        return pl.BlockSpec(a.shape, lambda b, i: (0,) * a.ndim)

    return pl.pallas_call(
        kern,
        out_shape=[jax.ShapeDtypeStruct((nb, t, 768), F32),
                   jax.ShapeDtypeStruct((nb, 2, t, 768), F32),
                   jax.ShapeDtypeStruct((nb, t, 256), F32)],
        grid=(nb, t // tm),
        in_specs=[
            pl.BlockSpec((1, tm, D_MODEL), lambda b, i: (b, i, cb)),
            pl.BlockSpec((1, halo, D_MODEL), lambda b, i: (b, jnp.maximum(i * hb - 1, 0), cb)),
            pl.BlockSpec((1, halo, D_MODEL), lambda b, i: (b, jnp.minimum((i + 1) * hb, last), cb)),
            full(mu), full(wl), full(bias), full(k_k), full(k_a), full(r_k),
        ],
        out_specs=[pl.BlockSpec((1, tm, 768), lambda b, i: (b, i, 0)),
                   pl.BlockSpec((1, 2, tm, 768), lambda b, i: (b, 0, i, 0)),
                   pl.BlockSpec((1, tm, 256), lambda b, i: (b, i, 0))],
        compiler_params=_cparams(("arbitrary", "arbitrary")),
        name="rwkv_prep",
    )(p, p, p, mu, wl, bias, k_k, k_a, r_k)


def _expand(x, bd):
    return jnp.where(bd, jnp.concatenate([x] * 4, axis=0), 0.0)


def _contract(x):
    c = x.shape[0] // 4
    return (x[0:c] + x[c:2 * c]) + (x[2 * c:3 * c] + x[3 * c:4 * c])


def _rwkv_masks():
    C = RWKV_CHUNK
    n = 4 * C
    rr, cc = np.indices((n, n))
    bd = ((rr // C) == (cc // C)).astype(np.float32)
    tr, tc = np.indices((C, n))
    tc = tc % C
    eye = (tr == tc).astype(np.float32)
    dirm = np.stack([np.stack([tc < tr, tc <= tr]), np.stack([tc > tr, tc >= tr])]).astype(np.float32)
    lev = [(tr >> 1) == (tc >> 1)]
    s = 1
    while (1 << s) < C:
        lev.append(((tr >> s) != (tc >> s)) & ((tr >> (s + 1)) == (tc >> (s + 1))))
        s += 1
    r1, c1 = np.indices((C, C))
    incl1 = np.stack([c1 <= r1, c1 >= r1]).astype(np.float32)
    return (jnp.asarray(bd), jnp.asarray(eye), jnp.asarray(dirm),
            jnp.asarray(np.stack(lev).astype(np.float32), BF16), jnp.asarray(incl1, BF16))


def _rwkv_local_kernel(sh_ref, dr_ref, bd_ref, eye_ref, dirm_ref, lev_ref, incl1_ref,
                       g_ref, hh_ref, rh_ref, yh_ref, *, nck):
    C = RWKV_CHUNK
    n = 4 * C
    bd = bd_ref[...]
    eye = eye_ref[...]
    bd_b = bd.astype(BF16)
    strict = dirm_ref[0, 0]
    incl = dirm_ref[0, 1]
    incl1 = incl1_ref[0]
    nt = (((1,), (1,)), ((), ()))
    tn = (((0,), (0,)), ((), ()))

    def expand_b(x):
        return jnp.concatenate([x.astype(BF16)] * 4, axis=0) * bd_b

    cks = range(nck)
    rows = [slice(ck * C, (ck + 1) * C) for ck in cks]
    r = [sh_ref[0, rw, 0:256] for rw in rows]
    v = [sh_ref[0, rw, 256:512] for rw in rows]
    kk = [sh_ref[0, rw, 512:768] for rw in rows]
    lw = [dr_ref[0, 0, rw, 0:256] for rw in rows]
    kd = [dr_ref[0, 0, rw, 256:512] for rw in rows]
    b = [dr_ref[0, 0, rw, 512:768] for rw in rows]

    lp = [_dot_exact_lhs(incl1, lw[i]) for i in cks]
    ltot = [jnp.sum(lw[i], 0, keepdims=True) for i in cks]
    e_neg = [jnp.exp(-lp[i]) for i in cks]
    kap = [kk[i] * jnp.exp(lp[i] - lw[i]) for i in cks]
    rt = [r[i] * jnp.exp(lp[i]) for i in cks]
    bt = [b[i] * e_neg[i] for i in cks]
    kt = [kd[i] * e_neg[i] for i in cks]
    e_rem = [jnp.exp(ltot[i] - lp[i]) for i in cks]
    bh = [b[i] * e_rem[i] for i in cks]
    kh = [kd[i] * e_rem[i] for i in cks]

    lhs = [jnp.concatenate([kap[i], rt[i]], axis=0).astype(BF16) for i in cks]
    gram_b = [lax.dot_general(lhs[i], expand_b(bt[i]), nt, preferred_element_type=F32) for i in cks]
    gram_k = [lax.dot_general(lhs[i], expand_b(kt[i]), nt, preferred_element_type=F32) for i in cks]
    mb_b = [(gram_b[i][0:C] * strict).astype(BF16) for i in cks]
    mrb = [gram_b[i][C:2 * C] * incl for i in cks]
    mk = [gram_k[i][0:C] * strict for i in cks]
    mrk = [gram_k[i][C:2 * C] * incl for i in cks]

    inv = [eye - (mb_b[i] * lev_ref[0]).astype(F32) for i in cks]
    for lv in range(1, lev_ref.shape[0]):
        inv_b = [inv[i].astype(BF16) for i in cks]
        step = [_dot(inv_b[i], expand_b(mb_b[i] * lev_ref[lv])) for i in cks]
        inv = [inv[i] - _dot(step[i].astype(BF16), expand_b(inv_b[i])) for i in cks]

    mv = [_dot(jnp.concatenate([mk[i], mrk[i]], axis=0).astype(BF16), expand_b(v[i])) for i in cks]
    wu_b = [_dot(inv[i].astype(BF16),
                 jnp.concatenate([expand_b(kap[i]), expand_b(mv[i][0:C])], axis=1)).astype(BF16)
            for i in cks]
    mw = [_dot(mrb[i].astype(BF16),
               jnp.concatenate([expand_b(wu_b[i][:, 0:n]), expand_b(wu_b[i][:, n:2 * n])], axis=1))
          for i in cks]
    bw = [lax.dot_general(bh[i].astype(BF16), wu_b[i], tn, preferred_element_type=F32) for i in cks]
    kv = [lax.dot_general(kh[i].astype(BF16), v[i].astype(BF16), tn, preferred_element_type=F32) for i in cks]
    for i in cks:
        rh_ref[0, 0, rows[i], :] = rt[i] - mw[i][:, 0:n]
        yh_ref[0, 0, rows[i], :] = mv[i][C:2 * C] - mw[i][:, n:2 * n]
        g_ref[0, 0, i] = eye * jnp.exp(ltot[i]) - _contract(bw[i][:, 0:n] * bd)
        hh_ref[0, 0, i] = _contract((kv[i] - bw[i][:, n:2 * n]) * bd)


def _rwkv_local_call(shared, dirs):
    nb, t, _ = shared.shape
    C = RWKV_CHUNK
    nch = t // C
    nck = 4 if nch % 4 == 0 else (2 if nch % 2 == 0 else 1)
    tm = nck * C
    n = 4 * C
    bd, eye, dirm, lev, incl1 = _rwkv_masks()
    kern = functools.partial(_rwkv_local_kernel, nck=nck)
    mat = jax.ShapeDtypeStruct((nb, 2, nch, C, 256), F32)
    seq = jax.ShapeDtypeStruct((nb, 2, t, 256), F32)
    mat_spec = pl.BlockSpec((1, 1, nck, C, 256), lambda b, d, i: (b, d, i, 0, 0))
    seq_spec = pl.BlockSpec((1, 1, tm, 256), lambda b, d, i: (b, d, i, 0))
    return pl.pallas_call(
        kern,
        out_shape=[mat, mat, seq, seq],
        grid=(nb, 2, nch // nck),
        in_specs=[pl.BlockSpec((1, tm, 768), lambda b, d, i: (b, i, 0)),
                  pl.BlockSpec((1, 1, tm, 768), lambda b, d, i: (b, d, i, 0)),
                  pl.BlockSpec(bd.shape, lambda b, d, i: (0, 0)),
                  pl.BlockSpec(eye.shape, lambda b, d, i: (0, 0)),
                  pl.BlockSpec((1, 2, C, n), lambda b, d, i: (d, 0, 0, 0)),
                  pl.BlockSpec(lev.shape, lambda b, d, i: (0, 0, 0)),
                  pl.BlockSpec((1, C, C), lambda b, d, i: (d, 0, 0))],
        out_specs=[mat_spec, mat_spec, seq_spec, seq_spec],
        compiler_params=_cparams(("arbitrary", "arbitrary", "arbitrary")),
        name="rwkv_local",
    )(shared, dirs, bd, eye, dirm, lev, incl1)


def _rwkv_scan_kernel(gf_ref, hf_ref, rf_ref, yf_ref, gb_ref, hb_ref, rb_ref, yb_ref,
                      of_ref, ob_ref, st_ref, *, nb):
    C = RWKV_CHUNK
    n = 4 * C
    j = pl.program_id(0)

    @pl.when(j == 0)
    def _():
        st_ref[...] = jnp.zeros(st_ref.shape, F32)

    rr = _iota((n, n), 0)
    cc = _iota((n, n), 1)
    bd = (rr >> 6) == (cc >> 6)
    dirs = ((gf_ref, hf_ref, rf_ref, yf_ref, of_ref), (gb_ref, hb_ref, rb_ref, yb_ref, ob_ref))
    seqs = [(d, b) for b in range(nb) for d in range(2)]

    w_hi, w_mid, l_hi, l_mid = [], [], [], []
    for d, b in seqs:
        g_ref, _, r_ref, _, _ = dirs[d]
        wh, wm, _ = _split3(_expand(st_ref[d, b], bd))
        lh, lm, _ = _split3(jnp.concatenate([r_ref[b, 0], g_ref[b, 0, 0]], axis=0))
        w_hi.append(wh)
        w_mid.append(wm)
        l_hi.append(lh)
        l_mid.append(lm)
    p_hi = [_dot(jnp.concatenate([l_hi[i], l_mid[i]], axis=0), w_hi[i]) for i in range(len(seqs))]
    p_mid = [_dot(l_hi[i], w_mid[i]) for i in range(len(seqs))]
    for i, (d, b) in enumerate(seqs):
        _, h_ref, _, y_ref, o_ref = dirs[d]
        tot = p_hi[i][0:2 * C] + (p_hi[i][2 * C:4 * C] + p_mid[i])
        o_ref[b] = tot[0:C] + y_ref[b, 0]
        st_ref[d, b] = tot[C:2 * C] + h_ref[b, 0, 0]


def _rwkv_scan_call(g, hh, rh, yh, lc):
    nb, _, nch, C, _ = g.shape
    t = nch * C
    ncc = lc // C

    def bwd(j):
        return jnp.where(j < ncc, ncc - 1 - j, nch - 1 + ncc - j)

    def mat_spec(d):
        if d == 0:
            return pl.BlockSpec((nb, 1, 1, C, 256), lambda j: (0, 0, j, 0, 0))
        return pl.BlockSpec((nb, 1, 1, C, 256), lambda j: (0, 1, bwd(j), 0, 0))

    def seq_spec(d):
        if d == 0:
            return pl.BlockSpec((nb, 1, C, 256), lambda j: (0, 0, j, 0))
        return pl.BlockSpec((nb, 1, C, 256), lambda j: (0, 1, bwd(j), 0))

    kern = functools.partial(_rwkv_scan_kernel, nb=nb)
    out = jax.ShapeDtypeStruct((nb, t, 256), F32)
    specs = []
    for d in range(2):
        specs += [mat_spec(d), mat_spec(d), seq_spec(d), seq_spec(d)]
    of, ob = pl.pallas_call(
        kern,
        out_shape=[out, out],
        grid=(nch,),
        in_specs=specs,
        out_specs=[pl.BlockSpec((nb, C, 256), lambda j: (0, j, 0)),
                   pl.BlockSpec((nb, C, 256), lambda j: (0, bwd(j), 0))],
        scratch_shapes=[pltpu.VMEM((2, nb, C, 256), F32)],
        compiler_params=_cparams(("arbitrary",)),
        name="rwkv_scan",
    )(g, hh, rh, yh, g, hh, rh, yh)
    return of, ob


def _merge_kernel(x_ref, mod_ref, oa_ref, oc_ref, od_ref, yf_ref, yb_ref, bon_ref, sg_ref, mg_ref,
                  bw_ref, ow_ref, mb_ref, gnw_ref, gnb_ref, lng_ref, lnb_ref, o_ref,
                  *, lc, tm, nb, row_off, alpha):
    b = pl.program_id(0)
    i = pl.program_id(1)

    def pair_cat(ref):
        return jnp.concatenate([ref[0, 0], ref[0, 1]], axis=1)

    def per_head(fn, x):
        return jnp.concatenate([fn(x[:, 0:LANES]), fn(x[:, LANES:2 * LANES])], axis=1)

    y = yf_ref[0] + yb_ref[0]
    mu = per_head(_half_sums, y) * (1.0 / RWKV_N)
    yc = y - mu
    var = per_head(_half_sums, yc * yc) * (1.0 / RWKV_N)
    ob = yc * lax.rsqrt(var + RWKV_GN_EPS) * gnw_ref[...] + gnb_ref[...] + bon_ref[0]

    outs = (pair_cat(oa_ref), ob, pair_cat(oc_ref), pair_cat(od_ref))
    acc = None
    for k in range(N_BRANCH):
        sg = sg_ref[0, :, k * BRANCH_W:(k + 1) * BRANCH_W].astype(F32)
        u = outs[k] * (sg * jax.nn.sigmoid(sg))
        z = _dot(u.astype(BF16), bw_ref[k])
        gate = jax.nn.sigmoid(mg_ref[0, :, k * D_MODEL:(k + 1) * D_MODEL].astype(F32)
                              + mb_ref[:, k * D_MODEL:(k + 1) * D_MODEL])
        acc = gate * z if acc is None else acc + gate * z
    y2 = _dot(acc.astype(BF16), ow_ref[...])

    row = row_off + i * tm + _iota((tm, D_MODEL), 0)
    gl = mod_ref[pl.ds(b, 1), 2 * D_MODEL:3 * D_MODEL]
    gc = mod_ref[pl.ds(nb, 1), 2 * D_MODEL:3 * D_MODEL]
    h = alpha * x_ref[0] + jnp.where(row < lc, gc, gl) * y2
    m = jnp.mean(h, -1, keepdims=True)
    hc = h - m
    var = jnp.mean(hc * hc, -1, keepdims=True)
    o_ref[0] = hc * lax.rsqrt(var + 1e-5) * lng_ref[...] + lnb_ref[...]


def _merge_call(xx, mod, oa, oc, od, yf, yb, bon, p, bw, ow, mb, gnw, gnb, lng, lnb, *, lc, row_off, alpha):
    nb, t, _ = xx.shape
    nrows = t - row_off
    tm = _pick_tile(math.gcd(nrows, row_off) if row_off else nrows, 256, 16)
    ro = row_off // tm
    kern = functools.partial(_merge_kernel, lc=lc, tm=tm, nb=nb, row_off=row_off, alpha=alpha)

    def full(a):
        return pl.BlockSpec(a.shape, lambda b, i: (0,) * a.ndim)

    pair_spec = pl.BlockSpec((1, 2, tm, LANES), lambda b, i: (b, 0, i, 0))
    return pl.pallas_call(
        kern,
        out_shape=jax.ShapeDtypeStruct((nb, nrows, D_MODEL), F32),
        grid=(nb, nrows // tm),
        in_specs=[
            pl.BlockSpec((1, tm, D_MODEL), lambda b, i: (b, i + ro, 0)),
            full(mod), pair_spec, pair_spec, pair_spec,
            pl.BlockSpec((1, tm, 256), lambda b, i: (b, i + ro, 0)),
            pl.BlockSpec((1, tm, 256), lambda b, i: (b, i + ro, 0)),
            pl.BlockSpec((1, tm, 256), lambda b, i: (b, i + ro, 0)),
            pl.BlockSpec((1, tm, D_MODEL), lambda b, i: (b, i + ro, P_SILU // D_MODEL)),
            pl.BlockSpec((1, tm, N_BRANCH * D_MODEL), lambda b, i: (b, i + ro, P_MERGE // (N_BRANCH * D_MODEL))),
            full(bw), full(ow), full(mb), full(gnw), full(gnb), full(lng), full(lnb),
        ],
        out_specs=pl.BlockSpec((1, tm, D_MODEL), lambda b, i: (b, i, 0)),
        compiler_params=_cparams(("arbitrary", "arbitrary")),
        name="merge",
    )(xx, mod, oa, oc, od, yf, yb, bon, p, p, bw, ow, mb, gnw, gnb, lng, lnb)


def _pack_in_w(w):
    z = lambda n: jnp.zeros((D_MODEL, n), w.dtype)
    a0 = 0
    b0 = 672
    c0 = b0 + 1280
    d0 = c0 + 768
    m0 = d0 + 1024
    cols = [
        w[:, a0:a0 + 256], w[:, a0 + 256:a0 + 384],
        z(64), w[:, a0 + 384:a0 + 416], z(32),
        w[:, c0:c0 + 256], w[:, c0 + 256:c0 + 384], w[:, c0 + 384:c0 + 512],
        w[:, d0:d0 + 256], w[:, d0 + 256:d0 + 512], w[:, d0 + 512:d0 + 768],
        z(256),
        w[:, b0:b0 + 1024],
        w[:, a0 + 416:a0 + 672], w[:, b0 + 1024:b0 + 1280], w[:, c0 + 512:c0 + 768], w[:, d0 + 768:d0 + 1024],
        w[:, m0:m0 + 4096],
    ]
    out = jnp.concatenate(cols, axis=1)
    assert out.shape[1] == P_WIDTH
    return out.astype(BF16)


def _pack_mla(w_uq, w_ukv):
    zq = jnp.zeros((MLA_Q_LORA, LANES - MLA_NOPE - MLA_ROPE), w_uq.dtype)
    zk = jnp.zeros((MLA_KV_LORA, LANES - MLA_NOPE), w_ukv.dtype)
    zv = jnp.zeros((MLA_KV_LORA, MLA_V), w_ukv.dtype)
    qc, kc, vc = [], [], []
    for h in range(MLA_HEADS):
        qc += [w_uq[:, h * 96:(h + 1) * 96], zq]
        kc += [w_ukv[:, h * 128:h * 128 + 64], zk]
        vh = w_ukv[:, h * 128 + 64:(h + 1) * 128]
        vc += [vh, zv] if h % 2 == 0 else [zv, vh]
    cat = lambda xs: jnp.concatenate(xs, axis=1).astype(BF16)
    return cat(qc), cat(kc), cat(vc)


def _rope_tables(row, col, rot_dim, lc, pattern):
    quarter = rot_dim // 4
    inv_freq = ROPE_BASE ** (-jnp.arange(quarter, dtype=F32) / quarter)
    ang = jnp.concatenate([row[:, None] * inv_freq, col[:, None] * inv_freq], axis=-1)
    cos, sin = jnp.cos(ang), jnp.sin(ang)
    n = ang.shape[0]
    ones = lambda w: jnp.ones((n, w), F32)
    zeros = lambda w: jnp.zeros((n, w), F32)
    if pattern == "mla":
        c = jnp.concatenate([ones(64), cos, cos, ones(32)], axis=1)
        s = jnp.concatenate([zeros(64), -sin, sin, zeros(32)], axis=1)
    else:
        reps = LANES // rot_dim
        c = jnp.concatenate([cos, cos] * reps, axis=1)
        s = jnp.concatenate([-sin, sin] * reps, axis=1)
    c = jnp.concatenate([jnp.ones((lc, LANES), F32), c], axis=0)
    s = jnp.concatenate([jnp.zeros((lc, LANES), F32), s], axis=0)
    return c, s


def kernel(x, c, ctx, c_ctx, ada_w, ada_b, in_w, mla_q_norm, mla_w_uq, mla_kv_norm, mla_w_ukv, rwkv_mu, rwkv_w0, rwkv_w_up, rwkv_a0, rwkv_a_up, rwkv_k_k, rwkv_k_a, rwkv_r_k, rwkv_gn_w, rwkv_gn_b, gqa_q_norm, gqa_k_norm, diff_lambda, diff_subln, merge_b, branch_w, out_w, ln_g, ln_b):
    nb, ll, _ = x.shape
    lc = ctx.shape[1]
    depth = ada_w.shape[0]
    t = lc + ll
    alpha = (2 * depth) ** 0.25

    rows = ll // GRID_W
    row = jnp.repeat(jnp.arange(rows), GRID_W).astype(F32)
    col = jnp.tile(jnp.arange(GRID_W), rows).astype(F32)
    tabs = (_rope_tables(row, col, MLA_ROPE, lc, "mla") + _rope_tables(row, col, HEAD_DIM, lc, "tile")
            + _rope_tables(row, col, DIFF_D, lc, "tile"))

    crows = -(-(nb + 1) // 8) * 8
    cvec = jnp.concatenate([c, c_ctx[None, :], jnp.zeros((crows - nb - 1, D_MODEL), F32)], axis=0)
    mods = _ada_call(cvec, ada_w, ada_b)

    xx = jnp.concatenate([ctx, x], axis=1)
    tile2 = lambda a: jnp.concatenate([a, a])[None, :]
    for l in range(depth):
        last = l == depth - 1
        lam_init = 0.8 - 0.6 * math.exp(-0.3 * l)
        mod = mods[l]
        p = _inproj_call(xx, mod, _pack_in_w(in_w[l]), lc)

        wuq, wk, wv = _pack_mla(mla_w_uq[l], mla_w_ukv[l])
        qa, ka, va, qc, kc, vc, qd, kd, vd = _prep_call(
            p, tabs, (mla_q_norm[l][None, :], wuq, mla_kv_norm[l][None, :], wk, wv,
                      tile2(gqa_q_norm[l]), tile2(gqa_k_norm[l])))
        lam_p = jnp.zeros((8, LANES), F32).at[0:4, 0:DIFF_D].set(diff_lambda[l])
        sub = tile2(diff_subln[l])
        att = lambda q, k, v, **kw: _attn_call(q, k, v, lk=t, q_off=lc, nq_rows=ll, **kw)
        oa = att(qa, ka, va, n_maps=1, shared_k=False)
        oc = att(qc, kc, vc, n_maps=1, shared_k=True)
        od = att(qd, kd, vd, n_maps=2, shared_k=False, extra=(lam_p, sub), lam_init=lam_init)
        if not last:
            catt = lambda q, k, v, **kw: _attn_call(q, k, v, lk=lc, q_off=0, nq_rows=lc, **kw)
            merge_rows = lambda lat_o, ctx_o: jnp.concatenate([ctx_o, lat_o], axis=2)
            oa = merge_rows(oa, catt(qa, ka, va, n_maps=1, shared_k=False))
            oc = merge_rows(oc, catt(qc, kc, vc, n_maps=1, shared_k=True))
            od = merge_rows(od, catt(qd, kd, vd, n_maps=2, shared_k=False, extra=(lam_p, sub), lam_init=lam_init))

        wl = jnp.zeros((256, 1024), F32)
        for d in range(2):
            wl = wl.at[d * 64:(d + 1) * 64, d * 256:(d + 1) * 256].set(rwkv_w_up[l, d])
            wl = wl.at[128 + d * 64:128 + (d + 1) * 64, 512 + d * 256:512 + (d + 1) * 256].set(rwkv_a_up[l, d])
        bias = jnp.concatenate([rwkv_w0[l, 0], rwkv_w0[l, 1], rwkv_a0[l, 0], rwkv_a0[l, 1]])[None, :]
        shared, dirs, bon = _rwkv_prep_call(p, rwkv_mu[l], wl.astype(BF16), bias, rwkv_k_k[l][None, :],
                                            rwkv_k_a[l][None, :], rwkv_r_k[l][None, :], lc)
        g, hh, rh, yh = _rwkv_local_call(shared, dirs)
        yf, yb = _rwkv_scan_call(g, hh, rh, yh, lc)

        xx = _merge_call(xx, mod, oa, oc, od, yf, yb, bon, p,
                         branch_w[l].astype(BF16), out_w[l].astype(BF16), merge_b[l][None, :],
                         rwkv_gn_w[l][None, :], rwkv_gn_b[l][None, :], ln_g[l][None, :], ln_b[l][None, :],
                         lc=lc, row_off=lc if last else 0, alpha=alpha)
    return xx
```

```python
import functools
import math

import jax
import jax.numpy as jnp
import numpy as np
from jax import lax
from jax.experimental import pallas as pl
from jax.experimental.pallas import tpu as pltpu

F32 = jnp.float32
BF16 = jnp.bfloat16

D_MODEL = 1024
GRID_W = 64
ROPE_BASE = 10000.0
HEAD_DIM = 64
N_BRANCH = 4
BRANCH_W = 256
MLA_HEADS = 4
MLA_Q_LORA = 256
MLA_KV_LORA = 128
MLA_NOPE = 64
MLA_ROPE = 32
MLA_V = 64
MLA_SCALE = (MLA_NOPE + MLA_ROPE) ** -0.5
RWKV_N = 64
RWKV_GN_EPS = 64e-5
RWKV_DECAY_SCALE = math.exp(-0.5)
GQA_SCALE = HEAD_DIM ** -0.5
DIFF_D = 32
DIFF_SCALE = DIFF_D ** -0.5
LOG2E = 1.0 / math.log(2.0)

LANES = 128
VMEM_LIMIT = 56 * 1024 * 1024

P_AQ, P_AKV, P_AKR = 0, 256, 384
P_CQ, P_CK, P_CV = 512, 768, 896
P_DQ, P_DK, P_DV = 1024, 1280, 1536
P_RWKV = 2048
P_SILU = 3072
P_MERGE = 4096
P_WIDTH = 8192
ATTN_W = 2048

RWKV_CHUNK = 64
ATTN_TQ = 256


def _cparams(sem):
    return pltpu.CompilerParams(dimension_semantics=sem, vmem_limit_bytes=VMEM_LIMIT)


def _split3(x):
    h = x.astype(BF16)
    r = x - h.astype(F32)
    m = r.astype(BF16)
    l = (r - m.astype(F32)).astype(BF16)
    return h, m, l


def _dot(a, b):
    return jnp.dot(a, b, preferred_element_type=F32)


def _dot_hi(a, b):
    ah, am, _ = _split3(a)
    bh, bm, _ = _split3(b)
    return _dot(ah, bh) + (_dot(ah, bm) + _dot(am, bh))


def _dot_exact_lhs(a_bf16, b):
    bh, bm, bl = _split3(b)
    return _dot(a_bf16, bh) + (_dot(a_bf16, bm) + _dot(a_bf16, bl))


def _iota(shape, dim):
    return lax.broadcasted_iota(jnp.int32, shape, dim)


def _ada_kernel(c_ref, w_ref, b_ref, o_ref):
    c = c_ref[...]
    s = c * jax.nn.sigmoid(c)
    o_ref[0] = _dot_hi(s, w_ref[0]) + b_ref[0]


def _ada_call(cvec, ada_w, ada_b):
    depth = ada_w.shape[0]
    rows = cvec.shape[0]
    nblk = 3
    return pl.pallas_call(
        _ada_kernel,
        out_shape=jax.ShapeDtypeStruct((depth, rows, 3 * D_MODEL), F32),
        grid=(depth, nblk),
        in_specs=[
            pl.BlockSpec((rows, D_MODEL), lambda l, j: (0, 0)),
            pl.BlockSpec((1, D_MODEL, D_MODEL), lambda l, j: (l, 0, j)),
            pl.BlockSpec((1, 1, D_MODEL), lambda l, j: (l, 0, j)),
        ],
        out_specs=pl.BlockSpec((1, rows, D_MODEL), lambda l, j: (l, 0, j)),
        compiler_params=_cparams(("arbitrary", "arbitrary")),
        name="ada",
    )(cvec, ada_w, ada_b.reshape(depth, 1, 3 * D_MODEL))


def _inproj_kernel(x_ref, mod_ref, w_ref, bias_ref, o_ref, xm_ref, *, lc, tm, nb, tn):
    b = pl.program_id(0)
    i = pl.program_id(1)
    j = pl.program_id(2)

    @pl.when(j == 0)
    def _():
        x = x_ref[0]
        mu = jnp.mean(x, -1, keepdims=True)
        xc = x - mu
        var = jnp.mean(xc * xc, -1, keepdims=True)
        xn = xc * lax.rsqrt(var + 1e-6)
        row = i * tm + _iota((tm, D_MODEL), 0)
        is_ctx = row < lc
        ml = mod_ref[pl.ds(b, 1), :]
        mc = mod_ref[pl.ds(nb, 1), :]
        shift = jnp.where(is_ctx, mc[:, 0:D_MODEL], ml[:, 0:D_MODEL])
        scale = jnp.where(is_ctx, mc[:, D_MODEL:2 * D_MODEL], ml[:, D_MODEL:2 * D_MODEL])
        xm_ref[...] = (xn * (1.0 + scale) + shift).astype(BF16)

    acc = _dot(xm_ref[...], w_ref[...])

    @pl.when(j < P_SILU // tn)
    def _():
        o_ref[0] = acc.astype(BF16)

    @pl.when((j >= P_SILU // tn) & (j < P_MERGE // tn))
    def _():
        o_ref[0] = (acc * jax.nn.sigmoid(acc)).astype(BF16)

    @pl.when(j >= P_MERGE // tn)
    def _():
        o_ref[0] = jax.nn.sigmoid(acc + bias_ref[...]).astype(BF16)


def _inproj_call(xx, mod, w_packed, bias, lc):
    nb, t, _ = xx.shape
    tm = _pick_tile(t, 1088, 16)
    tn = 1024
    assert P_SILU % tn == 0 and P_MERGE % tn == 0
    kern = functools.partial(_inproj_kernel, lc=lc, tm=tm, nb=nb, tn=tn)
    return pl.pallas_call(
        kern,
        out_shape=jax.ShapeDtypeStruct((nb, t, P_WIDTH), BF16),
        grid=(nb, t // tm, P_WIDTH // tn),
        in_specs=[
            pl.BlockSpec((1, tm, D_MODEL), lambda b, i, j: (b, i, 0)),
            pl.BlockSpec(mod.shape, lambda b, i, j: (0, 0)),
            pl.BlockSpec((D_MODEL, tn), lambda b, i, j: (0, j)),
            pl.BlockSpec((1, tn), lambda b, i, j: (0, j)),
        ],
        out_specs=pl.BlockSpec((1, tm, tn), lambda b, i, j: (b, i, j)),
        scratch_shapes=[pltpu.VMEM((tm, D_MODEL), BF16)],
        compiler_params=_cparams(("arbitrary", "arbitrary", "arbitrary")),
        name="inproj",
    )(xx, mod, w_packed, bias)


def _pick_tile(n, cap, mult):
    best = None
    for d in range(mult, min(n, cap) + 1, mult):
        if n % d == 0:
            best = d
    assert best is not None, (n, cap, mult)
    return best


def _swap_halves(x, half):
    n = x.shape[-1]
    first = (_iota(x.shape, 1) & (2 * half - 1)) < half
    up = pltpu.roll(x, n - half, 1)
    dn = pltpu.roll(x, half, 1)
    return jnp.where(first, up, dn)


def _half_sums(x):
    lo = _iota(x.shape, 1) < HEAD_DIM
    s_lo = jnp.sum(jnp.where(lo, x, 0.0), -1, keepdims=True)
    s_hi = jnp.sum(jnp.where(lo, 0.0, x), -1, keepdims=True)
    return jnp.where(lo, s_lo, s_hi)


def _prep_kernel(p_ref, cosa_ref, sina_ref, cosc_ref, sinc_ref, cosd_ref, sind_ref,
                 qn_ref, wuq_ref, kvn_ref, wk_ref, wv_ref, gq_ref, gk_ref,
                 qa_ref, ka_ref, va_ref, qc_ref, kc_ref, vc_ref, qd_ref, kd_ref, vd_ref):
    def seg(off, w):
        return p_ref[0, :, off:off + w].astype(F32)

    lane = _iota((p_ref.shape[1], LANES), 1)
    lo = lane < HEAD_DIM

    cosa, sina = cosa_ref[...], sina_ref[...]
    ql = seg(P_AQ, MLA_Q_LORA)
    ql = ql * lax.rsqrt(jnp.mean(ql * ql, -1, keepdims=True) + 1e-6) * qn_ref[...]
    q = _dot(ql.astype(BF16), wuq_ref[...])
    kvl = seg(P_AKV, MLA_KV_LORA)
    kvl = (kvl * lax.rsqrt(jnp.mean(kvl * kvl, -1, keepdims=True) + 1e-6) * kvn_ref[...]).astype(BF16)
    kn = _dot(kvl, wk_ref[...])
    vv = _dot(kvl, wv_ref[...])
    kr = seg(P_AKR, LANES)
    kr = kr * cosa + _swap_halves(kr, MLA_ROPE // 2) * sina
    for h in range(MLA_HEADS):
        qh = q[:, h * LANES:(h + 1) * LANES]
        qh = qh * cosa + _swap_halves(qh, MLA_ROPE // 2) * sina
        qa_ref[0, h] = (qh * (MLA_SCALE * LOG2E)).astype(BF16)
        ka_ref[0, h] = (kn[:, h * LANES:(h + 1) * LANES] + kr).astype(BF16)
        va_ref[0, h] = vv[:, h * LANES:(h + 1) * LANES].T.astype(BF16)

    cosc, sinc = cosc_ref[...], sinc_ref[...]

    def norm_rope(x, g):
        x = x * lax.rsqrt(_half_sums(x * x) * (1.0 / HEAD_DIM) + 1e-6) * g
        return x * cosc + _swap_halves(x, HEAD_DIM // 2) * sinc

    def split_heads(blk):
        return jnp.where(lo, blk, 0.0), jnp.where(lo, pltpu.roll(blk, HEAD_DIM, 1), 0.0)

    for pr in range(2):
        qb = norm_rope(seg(P_CQ + pr * LANES, LANES), gq_ref[...]) * (GQA_SCALE * LOG2E)
        q0, q1 = split_heads(qb)
        qc_ref[0, 2 * pr] = q0.astype(BF16)
        qc_ref[0, 2 * pr + 1] = q1.astype(BF16)
    k0, k1 = split_heads(norm_rope(seg(P_CK, LANES), gk_ref[...]))
    kc_ref[0, 0] = k0.astype(BF16)
    kc_ref[0, 1] = k1.astype(BF16)
    vb = seg(P_CV, LANES)
    v_g0 = jnp.where(lo, vb, 0.0)
    v_g1 = jnp.where(lo, 0.0, vb)
    vc_ref[0, 0] = v_g0.T.astype(BF16)
    vc_ref[0, 1] = pltpu.roll(v_g0, HEAD_DIM, 1).T.astype(BF16)
    vc_ref[0, 2] = pltpu.roll(v_g1, HEAD_DIM, 1).T.astype(BF16)
    vc_ref[0, 3] = v_g1.T.astype(BF16)

    cosd, sind = cosd_ref[...], sind_ref[...]
    piece = lane < DIFF_D
    for pr in range(2):
        qb = seg(P_DQ + pr * LANES, LANES)
        qb = (qb * cosd + _swap_halves(qb, DIFF_D // 2) * sind) * (DIFF_SCALE * LOG2E)
        kb = seg(P_DK + pr * LANES, LANES)
        kb = kb * cosd + _swap_halves(kb, DIFF_D // 2) * sind
        vb = seg(P_DV + pr * LANES, LANES)
        for hh in range(2):
            for m in range(2):
                off = hh * HEAD_DIM + m * DIFF_D
                idx = (2 * pr + hh) * 2 + m
                qs = qb if off == 0 else pltpu.roll(qb, LANES - off, 1)
                ks = kb if off == 0 else pltpu.roll(kb, LANES - off, 1)
                qd_ref[0, idx] = jnp.where(piece, qs, 0.0).astype(BF16)
                kd_ref[0, idx] = jnp.where(piece, ks, 0.0).astype(BF16)
        vd_ref[0, 2 * pr] = jnp.where(lo, vb, 0.0).T.astype(BF16)
        vd_ref[0, 2 * pr + 1] = jnp.where(lo, 0.0, vb).T.astype(BF16)


def _prep_call(p, tabs, wts):
    nb, t, _ = p.shape
    tm = _pick_tile(t, 256, 16)
    row_spec = pl.BlockSpec((tm, LANES), lambda b, i: (i, 0))

    def full(a):
        return pl.BlockSpec(a.shape, lambda b, i: (0,) * a.ndim)

    def head_out(nh):
        return (jax.ShapeDtypeStruct((nb, nh, t, LANES), BF16),
                pl.BlockSpec((1, nh, tm, LANES), lambda b, i: (b, 0, i, 0)))

    def head_out_t(nh):
        return (jax.ShapeDtypeStruct((nb, nh, LANES, t), BF16),
                pl.BlockSpec((1, nh, LANES, tm), lambda b, i: (b, 0, 0, i)))

    outs = [head_out(4), head_out(4), head_out_t(4), head_out(4), head_out(2), head_out_t(4),
            head_out(8), head_out(8), head_out_t(4)]
    return pl.pallas_call(
        _prep_kernel,
        out_shape=[o[0] for o in outs],
        grid=(nb, t // tm),
        in_specs=[pl.BlockSpec((1, tm, ATTN_W), lambda b, i: (b, i, 0))]
        + [row_spec] * 6 + [full(w) for w in wts],
        out_specs=[o[1] for o in outs],
        compiler_params=_cparams(("arbitrary", "arbitrary")),
        name="attn_prep",
    )(p, *tabs, *wts)


def _attn_kernel(*refs, hp, n_maps, shared_k, lk, tk, lam_init):
    if n_maps == 2:
        q_ref, k_ref, vt_ref, lam_ref, sub_ref, o_ref, s_scr, p_scr = refs
    else:
        q_ref, k_ref, vt_ref, o_ref, s_scr, p_scr = refs
    nchunks = lk // tk
    maps = [(hh * n_maps + mm, hh // 2 if shared_k else hh * n_maps + mm, hh)
            for hh in range(hp) for mm in range(n_maps)]

    def scores(j):
        qi, ki, _ = maps[j]
        s_scr[j % 2] = lax.dot_general(k_ref[0, ki], q_ref[0, qi], (((1,), (1,)), ((), ())),
                                       preferred_element_type=F32)

    def softmax_rows(j):
        buf = j % 2
        mx = None
        for c in range(nchunks):
            sc = s_scr[buf, c * tk:(c + 1) * tk, :]
            mx = sc if mx is None else jnp.maximum(mx, sc)
        m = jnp.max(mx, 0, keepdims=True)
        ls = None
        for c in range(nchunks):
            p = jnp.exp2(s_scr[buf, c * tk:(c + 1) * tk, :] - m)
            p_scr[buf, c * tk:(c + 1) * tk, :] = p.astype(BF16)
            ls = p if ls is None else ls + p
        return jnp.sum(ls, 0, keepdims=True)

    def weighted_values(j, l):
        ot = _dot(vt_ref[0, maps[j][2]], p_scr[j % 2])
        return (ot / l).T

    outs = []
    scores(0)
    for j in range(len(maps)):
        if j + 1 < len(maps):
            scores(j + 1)
        outs.append(weighted_values(j, softmax_rows(j)))

    for pr in range(hp // 2):
        total = None
        for hh in (2 * pr, 2 * pr + 1):
            o = outs[hh * n_maps]
            if n_maps == 2:
                lp = lam_ref[...]
                lam = (jnp.exp(jnp.sum(lp[0:1] * lp[1:2], keepdims=True))
                       - jnp.exp(jnp.sum(lp[2:3] * lp[3:4], keepdims=True)) + lam_init)
                o = o - lam * outs[hh * n_maps + 1]
                ms = jnp.sum(o * o, -1, keepdims=True) * (1.0 / HEAD_DIM)
                o = o * lax.rsqrt(ms + 1e-5) * sub_ref[...] * (1.0 - lam_init)
            total = o if total is None else total + o
        o_ref[0, pr] = total


def _attn_call(q, k, v, *, n_maps, shared_k, lk, q_off, nq_rows, extra=(), lam_init=0.0):
    nb = q.shape[0]
    hp = 4 // n_maps
    tq = _pick_tile(nq_rows, ATTN_TQ, 16)
    tk = 256 if lk % 256 == 0 else _pick_tile(lk, 256, LANES)
    kh = hp // 2 if shared_k else hp * n_maps
    kern = functools.partial(_attn_kernel, hp=hp, n_maps=n_maps, shared_k=shared_k, lk=lk, tk=tk,
                             lam_init=lam_init)
    in_specs = [
        pl.BlockSpec((pl.Element(1), pl.Element(hp * n_maps), pl.Element(tq), pl.Element(LANES)),
                     lambda b, p, i: (b, p * hp * n_maps, pl.multiple_of(i * tq + q_off, 16), 0)),
        pl.BlockSpec((1, kh, lk, LANES), lambda b, p, i: (b, p, 0, 0)),
        pl.BlockSpec((1, hp, LANES, lk), lambda b, p, i: (b, p, 0, 0)),
    ] + [pl.BlockSpec(e.shape, lambda b, p, i: (0, 0)) for e in extra]
    return pl.pallas_call(
        kern,
        out_shape=jax.ShapeDtypeStruct((nb, 2, nq_rows, LANES), F32),
        grid=(nb, 4 // hp, nq_rows // tq),
        in_specs=in_specs,
        out_specs=pl.BlockSpec((1, hp // 2, tq, LANES), lambda b, p, i: (b, p, i, 0)),
        scratch_shapes=[pltpu.VMEM((2, lk, tq), F32), pltpu.VMEM((2, lk, tq), BF16)],
        compiler_params=_cparams(("arbitrary", "arbitrary", "arbitrary")),
        name="attn",
    )(q, k, v, *extra)


def _rwkv_prep_kernel(cur_ref, prv_ref, nxt_ref, mu_ref, wl_ref, bias_ref, kk_ref, ka_ref, rk_ref,
                      sh_ref, dr_ref, bon_ref, *, lc, t_total, tm, halo):
    i = pl.program_id(1)
    cur = cur_ref[0]
    r_i = _iota((tm, tm), 0)
    c_i = _iota((tm, tm), 1)
    s_dn = jnp.where(c_i == r_i - 1, 1.0, 0.0).astype(BF16)
    s_up = jnp.where(c_i == r_i + 1, 1.0, 0.0).astype(BF16)
    row = _iota((tm, D_MODEL), 0)
    tg = i * tm + row
    prv = jnp.where(row == 0, prv_ref[0].astype(F32)[halo - 1:halo, :], _dot(s_dn, cur))
    nxt = jnp.where(row == tm - 1, nxt_ref[0].astype(F32)[0:1, :], _dot(s_up, cur))
    prv = jnp.where((tg == 0) | (tg == lc), 0.0, prv)
    nxt = jnp.where((tg == lc - 1) | (tg == t_total - 1), 0.0, nxt)
    x = cur.astype(F32)
    sh = x + mu_ref[0:1, :] * (prv - x) + mu_ref[1:2, :] * (nxt - x)

    r = sh[:, 0:256]
    k = sh[:, 256:512]
    v = sh[:, 512:768]
    lora = sh[:, 768:1024]
    lora = jnp.where(_iota(lora.shape, 1) < 2 * RWKV_N, jnp.tanh(lora), lora)
    z = _dot(lora.astype(BF16), wl_ref[...]) + bias_ref[...]

    def per_head(fn, x):
        return jnp.concatenate([fn(x[:, 0:LANES]), fn(x[:, LANES:2 * LANES])], axis=1)

    kq = k * kk_ref[...]
    kk = kq * lax.rsqrt(per_head(_half_sums, kq * kq) + 1e-12)
    sh_ref[0, :, 0:256] = r
    sh_ref[0, :, 256:512] = v
    sh_ref[0, :, 512:768] = kk
    hs = None
    for d in range(2):
        logw = -RWKV_DECAY_SCALE * jax.nn.sigmoid(z[:, d * 256:(d + 1) * 256])
        a = jax.nn.sigmoid(z[:, 512 + d * 256:512 + (d + 1) * 256])
        kd = k * (1.0 + (a - 1.0) * ka_ref[...])
        dr_ref[0, d, :, 0:256] = logw
        dr_ref[0, d, :, 256:512] = kd
        dr_ref[0, d, :, 512:768] = a * kk
        s = per_head(_half_sums, r * kd * rk_ref[...])
        hs = s if hs is None else hs + s
    bon_ref[0] = hs * v


def _rwkv_prep_call(p, mu, wl, bias, k_k, k_a, r_k, lc):
    nb, t, _ = p.shape
    tm = _pick_tile(t, 256, 16)
    halo = 16
    cb = P_RWKV // D_MODEL
    hb = tm // halo
    last = t // halo - 1
    kern = functools.partial(_rwkv_prep_kernel, lc=lc, t_total=t, tm=tm, halo=halo)

    def full(a):
        return pl.BlockSpec(a.shape, lambda b, i: (0,) * a.ndim)

    return pl.pallas_call(
        kern,
        out_shape=[jax.ShapeDtypeStruct((nb, t, 768), F32),
                   jax.ShapeDtypeStruct((nb, 2, t, 768), F32),
                   jax.ShapeDtypeStruct((nb, t, 256), F32)],
        grid=(nb, t // tm),
        in_specs=[
            pl.BlockSpec((1, tm, D_MODEL), lambda b, i: (b, i, cb)),
            pl.BlockSpec((1, halo, D_MODEL), lambda b, i: (b, jnp.maximum(i * hb - 1, 0), cb)),
            pl.BlockSpec((1, halo, D_MODEL), lambda b, i: (b, jnp.minimum((i + 1) * hb, last), cb)),
            full(mu), full(wl), full(bias), full(k_k), full(k_a), full(r_k),
        ],
        out_specs=[pl.BlockSpec((1, tm, 768), lambda b, i: (b, i, 0)),
                   pl.BlockSpec((1, 2, tm, 768), lambda b, i: (b, 0, i, 0)),
                   pl.BlockSpec((1, tm, 256), lambda b, i: (b, i, 0))],
        compiler_params=_cparams(("arbitrary", "arbitrary")),
        name="rwkv_prep",
    )(p, p, p, mu, wl, bias, k_k, k_a, r_k)


def _expand(x, bd):
    return jnp.where(bd, jnp.concatenate([x] * 4, axis=0), 0.0)


def _contract(x):
    c = x.shape[0] // 4
    return (x[0:c] + x[c:2 * c]) + (x[2 * c:3 * c] + x[3 * c:4 * c])


def _rwkv_masks():
    C = RWKV_CHUNK
    n = 4 * C
    rr, cc = np.indices((n, n))
    bd = ((rr // C) == (cc // C)).astype(np.float32)
    tr, tc = np.indices((C, n))
    tc = tc % C
    eye = (tr == tc).astype(np.float32)
    dirm = np.stack([np.stack([tc < tr, tc <= tr]), np.stack([tc > tr, tc >= tr])]).astype(np.float32)
    lev = [(tr >> 1) == (tc >> 1)]
    s = 1
    while (1 << s) < C:
        lev.append(((tr >> s) != (tc >> s)) & ((tr >> (s + 1)) == (tc >> (s + 1))))
        s += 1
    r1, c1 = np.indices((C, C))
    incl1 = np.stack([c1 <= r1, c1 >= r1]).astype(np.float32)
    return (jnp.asarray(bd), jnp.asarray(eye), jnp.asarray(dirm),
            jnp.asarray(np.stack(lev).astype(np.float32), BF16), jnp.asarray(incl1, BF16))


def _rwkv_local_kernel(sh_ref, dr_ref, bd_ref, eye_ref, dirm_ref, lev_ref, incl1_ref,
                       g_ref, hh_ref, rh_ref, yh_ref, *, nck):
    C = RWKV_CHUNK
    n = 4 * C
    bd = bd_ref[...]
    eye = eye_ref[...]
    bd_b = bd.astype(BF16)
    strict = dirm_ref[0, 0]
    incl = dirm_ref[0, 1]
    incl1 = incl1_ref[0]
    nt = (((1,), (1,)), ((), ()))
    tn = (((0,), (0,)), ((), ()))

    def expand_b(x):
        return jnp.concatenate([x.astype(BF16)] * 4, axis=0) * bd_b

    cks = range(nck)
    rows = [slice(ck * C, (ck + 1) * C) for ck in cks]
    r = [sh_ref[0, rw, 0:256] for rw in rows]
    v = [sh_ref[0, rw, 256:512] for rw in rows]
    kk = [sh_ref[0, rw, 512:768] for rw in rows]
    lw = [dr_ref[0, 0, rw, 0:256] for rw in rows]
    kd = [dr_ref[0, 0, rw, 256:512] for rw in rows]
    b = [dr_ref[0, 0, rw, 512:768] for rw in rows]

    lp = [_dot_exact_lhs(incl1, lw[i]) for i in cks]
    ltot = [jnp.sum(lw[i], 0, keepdims=True) for i in cks]
    e_neg = [jnp.exp(-lp[i]) for i in cks]
    kap = [kk[i] * jnp.exp(lp[i] - lw[i]) for i in cks]
    rt = [r[i] * jnp.exp(lp[i]) for i in cks]
    bt = [b[i] * e_neg[i] for i in cks]
    kt = [kd[i] * e_neg[i] for i in cks]
    e_rem = [jnp.exp(ltot[i] - lp[i]) for i in cks]
    bh = [b[i] * e_rem[i] for i in cks]
    kh = [kd[i] * e_rem[i] for i in cks]

    lhs = [jnp.concatenate([kap[i], rt[i]], axis=0).astype(BF16) for i in cks]
    gram_b = [lax.dot_general(lhs[i], expand_b(bt[i]), nt, preferred_element_type=F32) for i in cks]
    gram_k = [lax.dot_general(lhs[i], expand_b(kt[i]), nt, preferred_element_type=F32) for i in cks]
    mb_b = [(gram_b[i][0:C] * strict).astype(BF16) for i in cks]
    mrb = [gram_b[i][C:2 * C] * incl for i in cks]
    mk = [gram_k[i][0:C] * strict for i in cks]
    mrk = [gram_k[i][C:2 * C] * incl for i in cks]

    inv = [eye - (mb_b[i] * lev_ref[0]).astype(F32) for i in cks]
    for lv in range(1, lev_ref.shape[0]):
        inv_b = [inv[i].astype(BF16) for i in cks]
        step = [_dot(inv_b[i], expand_b(mb_b[i] * lev_ref[lv])) for i in cks]
        inv = [inv[i] - _dot(step[i].astype(BF16), expand_b(inv_b[i])) for i in cks]

    mv = [_dot(jnp.concatenate([mk[i], mrk[i]], axis=0).astype(BF16), expand_b(v[i])) for i in cks]
    wu_b = [_dot(inv[i].astype(BF16),
                 jnp.concatenate([expand_b(kap[i]), expand_b(mv[i][0:C])], axis=1)).astype(BF16)
            for i in cks]
    mw = [_dot(mrb[i].astype(BF16),
               jnp.concatenate([expand_b(wu_b[i][:, 0:n]), expand_b(wu_b[i][:, n:2 * n])], axis=1))
          for i in cks]
    bw = [lax.dot_general(bh[i].astype(BF16), wu_b[i], tn, preferred_element_type=F32) for i in cks]
    kv = [lax.dot_general(kh[i].astype(BF16), v[i].astype(BF16), tn, preferred_element_type=F32) for i in cks]
    for i in cks:
        rh_ref[0, 0, rows[i], :] = rt[i] - mw[i][:, 0:n]
        yh_ref[0, 0, rows[i], :] = mv[i][C:2 * C] - mw[i][:, n:2 * n]
        g_ref[0, 0, i] = eye * jnp.exp(ltot[i]) - _contract(bw[i][:, 0:n] * bd)
        hh_ref[0, 0, i] = _contract((kv[i] - bw[i][:, n:2 * n]) * bd)


def _rwkv_local_call(shared, dirs):
    nb, t, _ = shared.shape
    C = RWKV_CHUNK
    nch = t // C
    nck = 4 if nch % 4 == 0 else (2 if nch % 2 == 0 else 1)
    tm = nck * C
    n = 4 * C
    bd, eye, dirm, lev, incl1 = _rwkv_masks()
    kern = functools.partial(_rwkv_local_kernel, nck=nck)
    mat = jax.ShapeDtypeStruct((nb, 2, nch, C, 256), F32)
    seq = jax.ShapeDtypeStruct((nb, 2, t, 256), F32)
    mat_spec = pl.BlockSpec((1, 1, nck, C, 256), lambda b, d, i: (b, d, i, 0, 0))
    seq_spec = pl.BlockSpec((1, 1, tm, 256), lambda b, d, i: (b, d, i, 0))
    return pl.pallas_call(
        kern,
        out_shape=[mat, mat, seq, seq],
        grid=(nb, 2, nch // nck),
        in_specs=[pl.BlockSpec((1, tm, 768), lambda b, d, i: (b, i, 0)),
                  pl.BlockSpec((1, 1, tm, 768), lambda b, d, i: (b, d, i, 0)),
                  pl.BlockSpec(bd.shape, lambda b, d, i: (0, 0)),
                  pl.BlockSpec(eye.shape, lambda b, d, i: (0, 0)),
                  pl.BlockSpec((1, 2, C, n), lambda b, d, i: (d, 0, 0, 0)),
                  pl.BlockSpec(lev.shape, lambda b, d, i: (0, 0, 0)),
                  pl.BlockSpec((1, C, C), lambda b, d, i: (d, 0, 0))],
        out_specs=[mat_spec, mat_spec, seq_spec, seq_spec],
        compiler_params=_cparams(("arbitrary", "arbitrary", "arbitrary")),
        name="rwkv_local",
    )(shared, dirs, bd, eye, dirm, lev, incl1)


def _rwkv_scan_kernel(gf_ref, hf_ref, rf_ref, yf_ref, gb_ref, hb_ref, rb_ref, yb_ref,
                      of_ref, ob_ref, st_ref, *, nb):
    C = RWKV_CHUNK
    n = 4 * C
    j = pl.program_id(0)

    @pl.when(j == 0)
    def _():
        st_ref[...] = jnp.zeros(st_ref.shape, F32)

    rr = _iota((n, n), 0)
    cc = _iota((n, n), 1)
    bd = (rr >> 6) == (cc >> 6)
    dirs = ((gf_ref, hf_ref, rf_ref, yf_ref, of_ref), (gb_ref, hb_ref, rb_ref, yb_ref, ob_ref))
    seqs = [(d, b) for b in range(nb) for d in range(2)]

    w_hi, w_mid, l_hi, l_mid = [], [], [], []
    for d, b in seqs:
        g_ref, _, r_ref, _, _ = dirs[d]
        wh, wm, _ = _split3(_expand(st_ref[d, b], bd))
        lh, lm, _ = _split3(jnp.concatenate([r_ref[b, 0], g_ref[b, 0, 0]], axis=0))
        w_hi.append(wh)
        w_mid.append(wm)
        l_hi.append(lh)
        l_mid.append(lm)
    p_hi = [_dot(jnp.concatenate([l_hi[i], l_mid[i]], axis=0), w_hi[i]) for i in range(len(seqs))]
    p_mid = [_dot(l_hi[i], w_mid[i]) for i in range(len(seqs))]
    for i, (d, b) in enumerate(seqs):
        _, h_ref, _, y_ref, o_ref = dirs[d]
        tot = p_hi[i][0:2 * C] + (p_hi[i][2 * C:4 * C] + p_mid[i])
        o_ref[b] = tot[0:C] + y_ref[b, 0]
        st_ref[d, b] = tot[C:2 * C] + h_ref[b, 0, 0]


def _rwkv_scan_call(g, hh, rh, yh, lc):
    nb, _, nch, C, _ = g.shape
    t = nch * C
    ncc = lc // C

    def bwd(j):
        return jnp.where(j < ncc, ncc - 1 - j, nch - 1 + ncc - j)

    def mat_spec(d):
        if d == 0:
            return pl.BlockSpec((nb, 1, 1, C, 256), lambda j: (0, 0, j, 0, 0))
        return pl.BlockSpec((nb, 1, 1, C, 256), lambda j: (0, 1, bwd(j), 0, 0))

    def seq_spec(d):
        if d == 0:
            return pl.BlockSpec((nb, 1, C, 256), lambda j: (0, 0, j, 0))
        return pl.BlockSpec((nb, 1, C, 256), lambda j: (0, 1, bwd(j), 0))

    kern = functools.partial(_rwkv_scan_kernel, nb=nb)
    out = jax.ShapeDtypeStruct((nb, t, 256), F32)
    specs = []
    for d in range(2):
        specs += [mat_spec(d), mat_spec(d), seq_spec(d), seq_spec(d)]
    of, ob = pl.pallas_call(
        kern,
        out_shape=[out, out],
        grid=(nch,),
        in_specs=specs,
        out_specs=[pl.BlockSpec((nb, C, 256), lambda j: (0, j, 0)),
                   pl.BlockSpec((nb, C, 256), lambda j: (0, bwd(j), 0))],
        scratch_shapes=[pltpu.VMEM((2, nb, C, 256), F32)],
        compiler_params=_cparams(("arbitrary",)),
        name="rwkv_scan",
    )(g, hh, rh, yh, g, hh, rh, yh)
    return of, ob


def _merge_kernel(x_ref, mod_ref, oa_ref, oc_ref, od_ref, yf_ref, yb_ref, bon_ref, sg_ref, mg_ref,
                  bw_ref, ow_ref, gnw_ref, gnb_ref, lng_ref, lnb_ref, o_ref,
                  *, lc, tm, nb, row_off, alpha):
    b = pl.program_id(0)
    i = pl.program_id(1)

    def pair_cat(ref):
        return jnp.concatenate([ref[0, 0], ref[0, 1]], axis=1)

    def per_head(fn, x):
        return jnp.concatenate([fn(x[:, 0:LANES]), fn(x[:, LANES:2 * LANES])], axis=1)

    y = yf_ref[0] + yb_ref[0]
    mu = per_head(_half_sums, y) * (1.0 / RWKV_N)
    yc = y - mu
    var = per_head(_half_sums, yc * yc) * (1.0 / RWKV_N)
    ob = yc * lax.rsqrt(var + RWKV_GN_EPS) * gnw_ref[...] + gnb_ref[...] + bon_ref[0]

    outs = (pair_cat(oa_ref), ob, pair_cat(oc_ref), pair_cat(od_ref))
    acc = None
    for k in range(N_BRANCH):
        u = outs[k] * sg_ref[0, :, k * BRANCH_W:(k + 1) * BRANCH_W].astype(F32)
        z = _dot(u.astype(BF16), bw_ref[k])
        gate = mg_ref[0, :, k * D_MODEL:(k + 1) * D_MODEL].astype(F32)
        acc = gate * z if acc is None else acc + gate * z
    y2 = _dot(acc.astype(BF16), ow_ref[...])

    row = row_off + i * tm + _iota((tm, D_MODEL), 0)
    gl = mod_ref[pl.ds(b, 1), 2 * D_MODEL:3 * D_MODEL]
    gc = mod_ref[pl.ds(nb, 1), 2 * D_MODEL:3 * D_MODEL]
    h = alpha * x_ref[0] + jnp.where(row < lc, gc, gl) * y2
    m = jnp.mean(h, -1, keepdims=True)
    hc = h - m
    var = jnp.mean(hc * hc, -1, keepdims=True)
    o_ref[0] = hc * lax.rsqrt(var + 1e-5) * lng_ref[...] + lnb_ref[...]


def _merge_call(xx, mod, oa, oc, od, yf, yb, bon, p, bw, ow, gnw, gnb, lng, lnb, *, lc, row_off, alpha):
    nb, t, _ = xx.shape
    nrows = t - row_off
    tm = _pick_tile(math.gcd(nrows, row_off) if row_off else nrows, 256, 16)
    ro = row_off // tm
    kern = functools.partial(_merge_kernel, lc=lc, tm=tm, nb=nb, row_off=row_off, alpha=alpha)

    def full(a):
        return pl.BlockSpec(a.shape, lambda b, i: (0,) * a.ndim)

    pair_spec = pl.BlockSpec((1, 2, tm, LANES), lambda b, i: (b, 0, i, 0))
    return pl.pallas_call(
        kern,
        out_shape=jax.ShapeDtypeStruct((nb, nrows, D_MODEL), F32),
        grid=(nb, nrows // tm),
        in_specs=[
            pl.BlockSpec((1, tm, D_MODEL), lambda b, i: (b, i + ro, 0)),
            full(mod), pair_spec, pair_spec, pair_spec,
            pl.BlockSpec((1, tm, 256), lambda b, i: (b, i + ro, 0)),
            pl.BlockSpec((1, tm, 256), lambda b, i: (b, i + ro, 0)),
            pl.BlockSpec((1, tm, 256), lambda b, i: (b, i + ro, 0)),
            pl.BlockSpec((1, tm, D_MODEL), lambda b, i: (b, i + ro, P_SILU // D_MODEL)),
            pl.BlockSpec((1, tm, N_BRANCH * D_MODEL), lambda b, i: (b, i + ro, P_MERGE // (N_BRANCH * D_MODEL))),
            full(bw), full(ow), full(gnw), full(gnb), full(lng), full(lnb),
        ],
        out_specs=pl.BlockSpec((1, tm, D_MODEL), lambda b, i: (b, i, 0)),
        compiler_params=_cparams(("arbitrary", "arbitrary")),
        name="merge",
    )(xx, mod, oa, oc, od, yf, yb, bon, p, p, bw, ow, gnw, gnb, lng, lnb)


def _pack_in_w(w):
    z = lambda n: jnp.zeros((D_MODEL, n), w.dtype)
    a0 = 0
    b0 = 672
    c0 = b0 + 1280
    d0 = c0 + 768
    m0 = d0 + 1024
    cols = [
        w[:, a0:a0 + 256], w[:, a0 + 256:a0 + 384],
        z(64), w[:, a0 + 384:a0 + 416], z(32),
        w[:, c0:c0 + 256], w[:, c0 + 256:c0 + 384], w[:, c0 + 384:c0 + 512],
        w[:, d0:d0 + 256], w[:, d0 + 256:d0 + 512], w[:, d0 + 512:d0 + 768],
        z(256),
        w[:, b0:b0 + 1024],
        w[:, a0 + 416:a0 + 672], w[:, b0 + 1024:b0 + 1280], w[:, c0 + 512:c0 + 768], w[:, d0 + 768:d0 + 1024],
        w[:, m0:m0 + 4096],
    ]
    out = jnp.concatenate(cols, axis=1)
    assert out.shape[1] == P_WIDTH
    return out.astype(BF16)


def _pack_mla(w_uq, w_ukv):
    zq = jnp.zeros((MLA_Q_LORA, LANES - MLA_NOPE - MLA_ROPE), w_uq.dtype)
    zk = jnp.zeros((MLA_KV_LORA, LANES - MLA_NOPE), w_ukv.dtype)
    zv = jnp.zeros((MLA_KV_LORA, MLA_V), w_ukv.dtype)
    qc, kc, vc = [], [], []
    for h in range(MLA_HEADS):
        qc += [w_uq[:, h * 96:(h + 1) * 96], zq]
        kc += [w_ukv[:, h * 128:h * 128 + 64], zk]
        vh = w_ukv[:, h * 128 + 64:(h + 1) * 128]
        vc += [vh, zv] if h % 2 == 0 else [zv, vh]
    cat = lambda xs: jnp.concatenate(xs, axis=1).astype(BF16)
    return cat(qc), cat(kc), cat(vc)


def _rope_tables(row, col, rot_dim, lc, pattern):
    f32 = np.float32
    quarter = rot_dim // 4
    inv_freq = (f32(ROPE_BASE) ** (-np.arange(quarter, dtype=f32) / f32(quarter))).astype(f32)
    ang = np.concatenate([row[:, None] * inv_freq, col[:, None] * inv_freq], axis=-1).astype(f32)
    cos, sin = np.cos(ang).astype(f32), np.sin(ang).astype(f32)
    n = ang.shape[0]
    ones = lambda w: np.ones((n, w), f32)
    zeros = lambda w: np.zeros((n, w), f32)
    if pattern == "mla":
        c = np.concatenate([ones(64), cos, cos, ones(32)], axis=1)
        s = np.concatenate([zeros(64), -sin, sin, zeros(32)], axis=1)
    else:
        reps = LANES // rot_dim
        c = np.concatenate([cos, cos] * reps, axis=1)
        s = np.concatenate([-sin, sin] * reps, axis=1)
    c = np.concatenate([np.ones((lc, LANES), f32), c], axis=0)
    s = np.concatenate([np.zeros((lc, LANES), f32), s], axis=0)
    return jnp.asarray(c), jnp.asarray(s)


def kernel(x, c, ctx, c_ctx, ada_w, ada_b, in_w, mla_q_norm, mla_w_uq, mla_kv_norm, mla_w_ukv, rwkv_mu, rwkv_w0, rwkv_w_up, rwkv_a0, rwkv_a_up, rwkv_k_k, rwkv_k_a, rwkv_r_k, rwkv_gn_w, rwkv_gn_b, gqa_q_norm, gqa_k_norm, diff_lambda, diff_subln, merge_b, branch_w, out_w, ln_g, ln_b):
    nb, ll, _ = x.shape
    lc = ctx.shape[1]
    depth = ada_w.shape[0]
    t = lc + ll
    alpha = (2 * depth) ** 0.25

    rows = ll // GRID_W
    row = np.repeat(np.arange(rows), GRID_W).astype(np.float32)
    col = np.tile(np.arange(GRID_W), rows).astype(np.float32)
    tabs = (_rope_tables(row, col, MLA_ROPE, lc, "mla") + _rope_tables(row, col, HEAD_DIM, lc, "tile")
            + _rope_tables(row, col, DIFF_D, lc, "tile"))

    crows = -(-(nb + 1) // 8) * 8
    cvec = jnp.concatenate([c, c_ctx[None, :], jnp.zeros((crows - nb - 1, D_MODEL), F32)], axis=0)
    mods = _ada_call(cvec, ada_w, ada_b)

    xx = jnp.concatenate([ctx, x], axis=1)
    tile2 = lambda a: jnp.concatenate([a, a])[None, :]
    for l in range(depth):
        last = l == depth - 1
        lam_init = 0.8 - 0.6 * math.exp(-0.3 * l)
        mod = mods[l]
        gate_bias = jnp.concatenate([jnp.zeros((P_MERGE,), F32), merge_b[l]])[None, :]
        p = _inproj_call(xx, mod, _pack_in_w(in_w[l]), gate_bias, lc)

        wuq, wk, wv = _pack_mla(mla_w_uq[l], mla_w_ukv[l])
        qa, ka, va, qc, kc, vc, qd, kd, vd = _prep_call(
            p, tabs, (mla_q_norm[l][None, :], wuq, mla_kv_norm[l][None, :], wk, wv,
                      tile2(gqa_q_norm[l]), tile2(gqa_k_norm[l])))
        lam_p = jnp.zeros((8, LANES), F32).at[0:4, 0:DIFF_D].set(diff_lambda[l])
        sub = tile2(diff_subln[l])
        att = lambda q, k, v, **kw: _attn_call(q, k, v, lk=t, q_off=lc, nq_rows=ll, **kw)
        oa = att(qa, ka, va, n_maps=1, shared_k=False)
        oc = att(qc, kc, vc, n_maps=1, shared_k=True)
        od = att(qd, kd, vd, n_maps=2, shared_k=False, extra=(lam_p, sub), lam_init=lam_init)
        if not last:
            catt = lambda q, k, v, **kw: _attn_call(q, k, v, lk=lc, q_off=0, nq_rows=lc, **kw)
            merge_rows = lambda lat_o, ctx_o: jnp.concatenate([ctx_o, lat_o], axis=2)
            oa = merge_rows(oa, catt(qa, ka, va, n_maps=1, shared_k=False))
            oc = merge_rows(oc, catt(qc, kc, vc, n_maps=1, shared_k=True))
            od = merge_rows(od, catt(qd, kd, vd, n_maps=2, shared_k=False, extra=(lam_p, sub), lam_init=lam_init))

        wl = jnp.zeros((256, 1024), F32)
        for d in range(2):
            wl = wl.at[d * 64:(d + 1) * 64, d * 256:(d + 1) * 256].set(rwkv_w_up[l, d])
            wl = wl.at[128 + d * 64:128 + (d + 1) * 64, 512 + d * 256:512 + (d + 1) * 256].set(rwkv_a_up[l, d])
        bias = jnp.concatenate([rwkv_w0[l, 0], rwkv_w0[l, 1], rwkv_a0[l, 0], rwkv_a0[l, 1]])[None, :]
        shared, dirs, bon = _rwkv_prep_call(p, rwkv_mu[l], wl.astype(BF16), bias, rwkv_k_k[l][None, :],
                                            rwkv_k_a[l][None, :], rwkv_r_k[l][None, :], lc)
        g, hh, rh, yh = _rwkv_local_call(shared, dirs)
        yf, yb = _rwkv_scan_call(g, hh, rh, yh, lc)

        xx = _merge_call(xx, mod, oa, oc, od, yf, yb, bon, p,
                         branch_w[l].astype(BF16), out_w[l].astype(BF16),
                         rwkv_gn_w[l][None, :], rwkv_gn_b[l][None, :], ln_g[l][None, :], ln_b[l][None, :],
                         lc=lc, row_off=lc if last else 0, alpha=alpha)
    return xx
```

```python
import functools
import math

import jax
import jax.numpy as jnp
import numpy as np
from jax import lax
from jax.experimental import pallas as pl
from jax.experimental.pallas import tpu as pltpu

F32 = jnp.float32
BF16 = jnp.bfloat16

D_MODEL = 1024
GRID_W = 64
ROPE_BASE = 10000.0
HEAD_DIM = 64
N_BRANCH = 4
BRANCH_W = 256
MLA_HEADS = 4
MLA_Q_LORA = 256
MLA_KV_LORA = 128
MLA_NOPE = 64
MLA_ROPE = 32
MLA_V = 64
MLA_SCALE = (MLA_NOPE + MLA_ROPE) ** -0.5
RWKV_N = 64
RWKV_GN_EPS = 64e-5
RWKV_DECAY_SCALE = math.exp(-0.5)
GQA_SCALE = HEAD_DIM ** -0.5
DIFF_D = 32
DIFF_SCALE = DIFF_D ** -0.5
LOG2E = 1.0 / math.log(2.0)

LANES = 128
VMEM_LIMIT = 56 * 1024 * 1024

P_AQ, P_AKV, P_AKR = 0, 256, 384
P_CQ, P_CK, P_CV = 512, 768, 896
P_DQ, P_DK, P_DV = 1024, 1280, 1536
P_RWKV = 2048
P_SILU = 3072
P_MERGE = 4096
P_WIDTH = 8192
ATTN_W = 2048

RWKV_CHUNK = 64
ATTN_TQ = 256


def _cparams(sem):
    return pltpu.CompilerParams(dimension_semantics=sem, vmem_limit_bytes=VMEM_LIMIT)


def _split3(x):
    h = x.astype(BF16)
    r = x - h.astype(F32)
    m = r.astype(BF16)
    l = (r - m.astype(F32)).astype(BF16)
    return h, m, l


def _dot(a, b):
    return jnp.dot(a, b, preferred_element_type=F32)


def _dot_hi(a, b):
    ah, am, _ = _split3(a)
    bh, bm, _ = _split3(b)
    return _dot(ah, bh) + (_dot(ah, bm) + _dot(am, bh))


def _dot_exact_lhs(a_bf16, b):
    bh, bm, bl = _split3(b)
    return _dot(a_bf16, bh) + (_dot(a_bf16, bm) + _dot(a_bf16, bl))


def _iota(shape, dim):
    return lax.broadcasted_iota(jnp.int32, shape, dim)


def _ada_kernel(c_ref, w_ref, b_ref, o_ref):
    c = c_ref[...]
    s = c * jax.nn.sigmoid(c)
    o_ref[0] = _dot_hi(s, w_ref[0]) + b_ref[0]


def _ada_call(cvec, ada_w, ada_b):
    depth = ada_w.shape[0]
    rows = cvec.shape[0]
    nblk = 3
    return pl.pallas_call(
        _ada_kernel,
        out_shape=jax.ShapeDtypeStruct((depth, rows, 3 * D_MODEL), F32),
        grid=(depth, nblk),
        in_specs=[
            pl.BlockSpec((rows, D_MODEL), lambda l, j: (0, 0)),
            pl.BlockSpec((1, D_MODEL, D_MODEL), lambda l, j: (l, 0, j)),
            pl.BlockSpec((1, 1, D_MODEL), lambda l, j: (l, 0, j)),
        ],
        out_specs=pl.BlockSpec((1, rows, D_MODEL), lambda l, j: (l, 0, j)),
        compiler_params=_cparams(("arbitrary", "arbitrary")),
        name="ada",
    )(cvec, ada_w, ada_b.reshape(depth, 1, 3 * D_MODEL))


def _inproj_kernel(x_ref, mod_ref, w_ref, o_ref, xm_ref, *, lc, tm, nb):
    b = pl.program_id(0)
    i = pl.program_id(1)
    j = pl.program_id(2)

    @pl.when(j == 0)
    def _():
        x = x_ref[0]
        mu = jnp.mean(x, -1, keepdims=True)
        xc = x - mu
        var = jnp.mean(xc * xc, -1, keepdims=True)
        xn = xc * lax.rsqrt(var + 1e-6)
        row = i * tm + _iota((tm, D_MODEL), 0)
        is_ctx = row < lc
        ml = mod_ref[pl.ds(b, 1), :]
        mc = mod_ref[pl.ds(nb, 1), :]
        shift = jnp.where(is_ctx, mc[:, 0:D_MODEL], ml[:, 0:D_MODEL])
        scale = jnp.where(is_ctx, mc[:, D_MODEL:2 * D_MODEL], ml[:, D_MODEL:2 * D_MODEL])
        xm_ref[...] = (xn * (1.0 + scale) + shift).astype(BF16)

    o_ref[0] = _dot(xm_ref[...], w_ref[...]).astype(BF16)


def _inproj_call(xx, mod, w_packed, lc):
    nb, t, _ = xx.shape
    tm = _pick_tile(t, 1088, 16)
    tn = 1024
    kern = functools.partial(_inproj_kernel, lc=lc, tm=tm, nb=nb)
    return pl.pallas_call(
        kern,
        out_shape=jax.ShapeDtypeStruct((nb, t, P_WIDTH), BF16),
        grid=(nb, t // tm, P_WIDTH // tn),
        in_specs=[
            pl.BlockSpec((1, tm, D_MODEL), lambda b, i, j: (b, i, 0)),
            pl.BlockSpec(mod.shape, lambda b, i, j: (0, 0)),
            pl.BlockSpec((D_MODEL, tn), lambda b, i, j: (0, j)),
        ],
        out_specs=pl.BlockSpec((1, tm, tn), lambda b, i, j: (b, i, j)),
        scratch_shapes=[pltpu.VMEM((tm, D_MODEL), BF16)],
        compiler_params=_cparams(("arbitrary", "arbitrary", "arbitrary")),
        name="inproj",
    )(xx, mod, w_packed)


def _pick_tile(n, cap, mult):
    best = None
    for d in range(mult, min(n, cap) + 1, mult):
        if n % d == 0:
            best = d
    assert best is not None, (n, cap, mult)
    return best


def _swap_halves(x, half):
    n = x.shape[-1]
    first = (_iota(x.shape, 1) & (2 * half - 1)) < half
    up = pltpu.roll(x, n - half, 1)
    dn = pltpu.roll(x, half, 1)
    return jnp.where(first, up, dn)


def _half_sums(x):
    lo = _iota(x.shape, 1) < HEAD_DIM
    s_lo = jnp.sum(jnp.where(lo, x, 0.0), -1, keepdims=True)
    s_hi = jnp.sum(jnp.where(lo, 0.0, x), -1, keepdims=True)
    return jnp.where(lo, s_lo, s_hi)


def _prep_kernel(p_ref, cosa_ref, sina_ref, cosc_ref, sinc_ref, cosd_ref, sind_ref,
                 qn_ref, wuq_ref, kvn_ref, wk_ref, wv_ref, gq_ref, gk_ref,
                 qa_ref, ka_ref, va_ref, qc_ref, kc_ref, vc_ref, qd_ref, kd_ref, vd_ref):
    def seg(off, w):
        return p_ref[0, :, off:off + w].astype(F32)

    lane = _iota((p_ref.shape[1], LANES), 1)
    lo = lane < HEAD_DIM

    cosa, sina = cosa_ref[...], sina_ref[...]
    ql = seg(P_AQ, MLA_Q_LORA)
    ql = ql * lax.rsqrt(jnp.mean(ql * ql, -1, keepdims=True) + 1e-6) * qn_ref[...]
    q = _dot(ql.astype(BF16), wuq_ref[...])
    kvl = seg(P_AKV, MLA_KV_LORA)
    kvl = (kvl * lax.rsqrt(jnp.mean(kvl * kvl, -1, keepdims=True) + 1e-6) * kvn_ref[...]).astype(BF16)
    kn = _dot(kvl, wk_ref[...])
    vv = _dot(kvl, wv_ref[...])
    kr = seg(P_AKR, LANES)
    kr = kr * cosa + _swap_halves(kr, MLA_ROPE // 2) * sina
    for h in range(MLA_HEADS):
        qh = q[:, h * LANES:(h + 1) * LANES]
        qh = qh * cosa + _swap_halves(qh, MLA_ROPE // 2) * sina
        qa_ref[0, h] = (qh * (MLA_SCALE * LOG2E)).astype(BF16)
        ka_ref[0, h] = (kn[:, h * LANES:(h + 1) * LANES] + kr).astype(BF16)
        va_ref[0, h] = vv[:, h * LANES:(h + 1) * LANES].T.astype(BF16)

    cosc, sinc = cosc_ref[...], sinc_ref[...]

    def norm_rope(x, g):
        x = x * lax.rsqrt(_half_sums(x * x) * (1.0 / HEAD_DIM) + 1e-6) * g
        return x * cosc + _swap_halves(x, HEAD_DIM // 2) * sinc

    def split_heads(blk):
        return jnp.where(lo, blk, 0.0), jnp.where(lo, pltpu.roll(blk, HEAD_DIM, 1), 0.0)

    for pr in range(2):
        qb = norm_rope(seg(P_CQ + pr * LANES, LANES), gq_ref[...]) * (GQA_SCALE * LOG2E)
        q0, q1 = split_heads(qb)
        qc_ref[0, 2 * pr] = q0.astype(BF16)
        qc_ref[0, 2 * pr + 1] = q1.astype(BF16)
    k0, k1 = split_heads(norm_rope(seg(P_CK, LANES), gk_ref[...]))
    kc_ref[0, 0] = k0.astype(BF16)
    kc_ref[0, 1] = k1.astype(BF16)
    vb = seg(P_CV, LANES)
    v_g0 = jnp.where(lo, vb, 0.0)
    v_g1 = jnp.where(lo, 0.0, vb)
    vc_ref[0, 0] = v_g0.T.astype(BF16)
    vc_ref[0, 1] = pltpu.roll(v_g0, HEAD_DIM, 1).T.astype(BF16)
    vc_ref[0, 2] = pltpu.roll(v_g1, HEAD_DIM, 1).T.astype(BF16)
    vc_ref[0, 3] = v_g1.T.astype(BF16)

    cosd, sind = cosd_ref[...], sind_ref[...]
    piece = lane < DIFF_D
    for pr in range(2):
        qb = seg(P_DQ + pr * LANES, LANES)
        qb = (qb * cosd + _swap_halves(qb, DIFF_D // 2) * sind) * (DIFF_SCALE * LOG2E)
        kb = seg(P_DK + pr * LANES, LANES)
        kb = kb * cosd + _swap_halves(kb, DIFF_D // 2) * sind
        vb = seg(P_DV + pr * LANES, LANES)
        for hh in range(2):
            for m in range(2):
                off = hh * HEAD_DIM + m * DIFF_D
                idx = (2 * pr + hh) * 2 + m
                qs = qb if off == 0 else pltpu.roll(qb, LANES - off, 1)
                ks = kb if off == 0 else pltpu.roll(kb, LANES - off, 1)
                qd_ref[0, idx] = jnp.where(piece, qs, 0.0).astype(BF16)
                kd_ref[0, idx] = jnp.where(piece, ks, 0.0).astype(BF16)
        vd_ref[0, 2 * pr] = jnp.where(lo, vb, 0.0).T.astype(BF16)
        vd_ref[0, 2 * pr + 1] = jnp.where(lo, 0.0, vb).T.astype(BF16)


def _prep_call(p, tabs, wts):
    nb, t, _ = p.shape
    tm = _pick_tile(t, 256, 16)
    row_spec = pl.BlockSpec((tm, LANES), lambda b, i: (i, 0))

    def full(a):
        return pl.BlockSpec(a.shape, lambda b, i: (0,) * a.ndim)

    def head_out(nh):
        return (jax.ShapeDtypeStruct((nb, nh, t, LANES), BF16),
                pl.BlockSpec((1, nh, tm, LANES), lambda b, i: (b, 0, i, 0)))

    def head_out_t(nh):
        return (jax.ShapeDtypeStruct((nb, nh, LANES, t), BF16),
                pl.BlockSpec((1, nh, LANES, tm), lambda b, i: (b, 0, 0, i)))

    outs = [head_out(4), head_out(4), head_out_t(4), head_out(4), head_out(2), head_out_t(4),
            head_out(8), head_out(8), head_out_t(4)]
    return pl.pallas_call(
        _prep_kernel,
        out_shape=[o[0] for o in outs],
        grid=(nb, t // tm),
        in_specs=[pl.BlockSpec((1, tm, ATTN_W), lambda b, i: (b, i, 0))]
        + [row_spec] * 6 + [full(w) for w in wts],
        out_specs=[o[1] for o in outs],
        compiler_params=_cparams(("arbitrary", "arbitrary")),
        name="attn_prep",
    )(p, *tabs, *wts)


def _attn_kernel(*refs, hp, n_maps, qsub, tq, shared_k, lk, tk, lam_init):
    if n_maps == 2:
        q_ref, k_ref, vt_ref, lam_ref, sub_ref, o_ref, s_scr, p_scr = refs
    else:
        q_ref, k_ref, vt_ref, o_ref, s_scr, p_scr = refs
    nchunks = lk // tk
    maps = [(qs, hh * n_maps + mm, hh // 2 if shared_k else hh * n_maps + mm, hh)
            for qs in range(qsub) for hh in range(hp) for mm in range(n_maps)]
    per_sub = hp * n_maps

    def scores(j):
        qs, qi, ki, _ = maps[j]
        s_scr[j % 2] = lax.dot_general(k_ref[0, ki], q_ref[0, qi, qs * tq:(qs + 1) * tq, :],
                                       (((1,), (1,)), ((), ())), preferred_element_type=F32)

    def softmax_rows(j):
        buf = j % 2
        mx = None
        for c in range(nchunks):
            sc = s_scr[buf, c * tk:(c + 1) * tk, :]
            mx = sc if mx is None else jnp.maximum(mx, sc)
        m = jnp.max(mx, 0, keepdims=True)
        ls = None
        for c in range(nchunks):
            p = jnp.exp2(s_scr[buf, c * tk:(c + 1) * tk, :] - m)
            p_scr[buf, c * tk:(c + 1) * tk, :] = p.astype(BF16)
            ls = p if ls is None else ls + p
        return jnp.sum(ls, 0, keepdims=True)

    def weighted_values(j, l):
        ot = _dot(vt_ref[0, maps[j][3]], p_scr[j % 2])
        return (ot / l).T

    outs = []
    scores(0)
    for j in range(len(maps)):
        if j + 1 < len(maps):
            scores(j + 1)
        outs.append(weighted_values(j, softmax_rows(j)))

    for qs in range(qsub):
        for pr in range(hp // 2):
            total = None
            for hh in (2 * pr, 2 * pr + 1):
                o = outs[qs * per_sub + hh * n_maps]
                if n_maps == 2:
                    lp = lam_ref[...]
                    lam = (jnp.exp(jnp.sum(lp[0:1] * lp[1:2], keepdims=True))
                           - jnp.exp(jnp.sum(lp[2:3] * lp[3:4], keepdims=True)) + lam_init)
                    o = o - lam * outs[qs * per_sub + hh * n_maps + 1]
                    ms = jnp.sum(o * o, -1, keepdims=True) * (1.0 / HEAD_DIM)
                    o = o * lax.rsqrt(ms + 1e-5) * sub_ref[...] * (1.0 - lam_init)
                total = o if total is None else total + o
            o_ref[0, pr, qs * tq:(qs + 1) * tq, :] = total


def _attn_call(q, k, v, *, n_maps, shared_k, lk, q_off, nq_rows, extra=(), lam_init=0.0):
    nb = q.shape[0]
    hp = 4
    tq = _pick_tile(nq_rows, ATTN_TQ, 16)
    qsub = 2 if (n_maps == 1 and nq_rows % (2 * tq) == 0) else 1
    tb = qsub * tq
    tk = 256 if lk % 256 == 0 else _pick_tile(lk, 256, LANES)
    kh = hp // 2 if shared_k else hp * n_maps
    kern = functools.partial(_attn_kernel, hp=hp, n_maps=n_maps, qsub=qsub, tq=tq, shared_k=shared_k,
                             lk=lk, tk=tk, lam_init=lam_init)
    in_specs = [
        pl.BlockSpec((pl.Element(1), pl.Element(hp * n_maps), pl.Element(tb), pl.Element(LANES)),
                     lambda b, i: (b, 0, pl.multiple_of(i * tb + q_off, 16), 0)),
        pl.BlockSpec((1, kh, lk, LANES), lambda b, i: (b, 0, 0, 0)),
        pl.BlockSpec((1, hp, LANES, lk), lambda b, i: (b, 0, 0, 0)),
    ] + [pl.BlockSpec(e.shape, lambda b, i: (0, 0)) for e in extra]
    return pl.pallas_call(
        kern,
        out_shape=jax.ShapeDtypeStruct((nb, 2, nq_rows, LANES), F32),
        grid=(nb, nq_rows // tb),
        in_specs=in_specs,
        out_specs=pl.BlockSpec((1, hp // 2, tb, LANES), lambda b, i: (b, 0, i, 0)),
        scratch_shapes=[pltpu.VMEM((2, lk, tq), F32), pltpu.VMEM((2, lk, tq), BF16)],
        compiler_params=_cparams(("arbitrary", "arbitrary")),
        name="attn",
    )(q, k, v, *extra)


def _rwkv_prep_kernel(cur_ref, prv_ref, nxt_ref, mu_ref, wl_ref, bias_ref, kk_ref, ka_ref, rk_ref,
                      sh_ref, dr_ref, bon_ref, *, lc, t_total, tm, halo):
    i = pl.program_id(1)
    cur = cur_ref[0]
    r_i = _iota((tm, tm), 0)
    c_i = _iota((tm, tm), 1)
    s_dn = jnp.where(c_i == r_i - 1, 1.0, 0.0).astype(BF16)
    s_up = jnp.where(c_i == r_i + 1, 1.0, 0.0).astype(BF16)
    row = _iota((tm, D_MODEL), 0)
    tg = i * tm + row
    prv = jnp.where(row == 0, prv_ref[0].astype(F32)[halo - 1:halo, :], _dot(s_dn, cur))
    nxt = jnp.where(row == tm - 1, nxt_ref[0].astype(F32)[0:1, :], _dot(s_up, cur))
    prv = jnp.where((tg == 0) | (tg == lc), 0.0, prv)
    nxt = jnp.where((tg == lc - 1) | (tg == t_total - 1), 0.0, nxt)
    x = cur.astype(F32)
    sh = x + mu_ref[0:1, :] * (prv - x) + mu_ref[1:2, :] * (nxt - x)

    r = sh[:, 0:256]
    k = sh[:, 256:512]
    v = sh[:, 512:768]
    lora = sh[:, 768:1024]
    lora = jnp.where(_iota(lora.shape, 1) < 2 * RWKV_N, jnp.tanh(lora), lora)
    z = _dot(lora.astype(BF16), wl_ref[...]) + bias_ref[...]

    def per_head(fn, x):
        return jnp.concatenate([fn(x[:, 0:LANES]), fn(x[:, LANES:2 * LANES])], axis=1)

    kq = k * kk_ref[...]
    kk = kq * lax.rsqrt(per_head(_half_sums, kq * kq) + 1e-12)
    sh_ref[0, :, 0:256] = r
    sh_ref[0, :, 256:512] = v
    sh_ref[0, :, 512:768] = kk
    hs = None
    for d in range(2):
        logw = -RWKV_DECAY_SCALE * jax.nn.sigmoid(z[:, d * 256:(d + 1) * 256])
        a = jax.nn.sigmoid(z[:, 512 + d * 256:512 + (d + 1) * 256])
        kd = k * (1.0 + (a - 1.0) * ka_ref[...])
        dr_ref[0, d, :, 0:256] = logw
        dr_ref[0, d, :, 256:512] = kd
        dr_ref[0, d, :, 512:768] = a * kk
        s = per_head(_half_sums, r * kd * rk_ref[...])
        hs = s if hs is None else hs + s
    bon_ref[0] = hs * v


def _rwkv_prep_call(p, mu, wl, bias, k_k, k_a, r_k, lc):
    nb, t, _ = p.shape
    tm = _pick_tile(t, 256, 16)
    halo = 16
    cb = P_RWKV // D_MODEL
    hb = tm // halo
    last = t // halo - 1
    kern = functools.partial(_rwkv_prep_kernel, lc=lc, t_total=t, tm=tm, halo=halo)

    def full(a):
        return pl.BlockSpec(a.shape, lambda b, i: (0,) * a.ndim)

    return pl.pallas_call(
        kern,
        out_shape=[jax.ShapeDtypeStruct((nb, t, 768), F32),
                   jax.ShapeDtypeStruct((nb, 2, t, 768), F32),
                   jax.ShapeDtypeStruct((nb, t, 256), F32)],
        grid=(nb, t // tm),
        in_specs=[
            pl.BlockSpec((1, tm, D_MODEL), lambda b, i: (b, i, cb)),
            pl.BlockSpec((1, halo, D_MODEL), lambda b, i: (b, jnp.maximum(i * hb - 1, 0), cb)),
            pl.BlockSpec((1, halo, D_MODEL), lambda b, i: (b, jnp.minimum((i + 1) * hb, last), cb)),
            full(mu), full(wl), full(bias), full(k_k), full(k_a), full(r_k),
        ],
        out_specs=[pl.BlockSpec((1, tm, 768), lambda b, i: (b, i, 0)),
                   pl.BlockSpec((1, 2, tm, 768), lambda b, i: (b, 0, i, 0)),
                   pl.BlockSpec((1, tm, 256), lambda b, i: (b, i, 0))],
        compiler_params=_cparams(("arbitrary", "arbitrary")),
        name="rwkv_prep",
    )(p, p, p, mu, wl, bias, k_k, k_a, r_k)


def _expand(x, bd):
    return jnp.where(bd, jnp.concatenate([x] * 4, axis=0), 0.0)


def _contract(x):
    c = x.shape[0] // 4
    return (x[0:c] + x[c:2 * c]) + (x[2 * c:3 * c] + x[3 * c:4 * c])


def _rwkv_masks():
    C = RWKV_CHUNK
    n = 4 * C
    rr, cc = np.indices((n, n))
    bd = ((rr // C) == (cc // C)).astype(np.float32)
    tr, tc = np.indices((C, n))
    tc = tc % C
    eye = (tr == tc).astype(np.float32)
    dirm = np.stack([np.stack([tc < tr, tc <= tr]), np.stack([tc > tr, tc >= tr])]).astype(np.float32)
    lev = [(tr >> 1) == (tc >> 1)]
    s = 1
    while (1 << s) < C:
        lev.append(((tr >> s) != (tc >> s)) & ((tr >> (s + 1)) == (tc >> (s + 1))))
        s += 1
    r1, c1 = np.indices((C, C))
    incl1 = np.stack([c1 <= r1, c1 >= r1]).astype(np.float32)
    return (jnp.asarray(bd), jnp.asarray(eye), jnp.asarray(dirm),
            jnp.asarray(np.stack(lev).astype(np.float32), BF16), jnp.asarray(incl1, BF16))


def _rwkv_local_kernel(sh_ref, dr_ref, bd_ref, eye_ref, dirm_ref, lev_ref, incl1_ref,
                       g_ref, hh_ref, rh_ref, yh_ref, *, nck):
    C = RWKV_CHUNK
    n = 4 * C
    bd = bd_ref[...]
    eye = eye_ref[...]
    bd_b = bd.astype(BF16)
    strict = dirm_ref[0, 0]
    incl = dirm_ref[0, 1]
    incl1 = incl1_ref[0]
    nt = (((1,), (1,)), ((), ()))
    tn = (((0,), (0,)), ((), ()))

    def expand_b(x):
        return jnp.concatenate([x.astype(BF16)] * 4, axis=0) * bd_b

    cks = range(nck)
    rows = [slice(ck * C, (ck + 1) * C) for ck in cks]
    r = [sh_ref[0, rw, 0:256] for rw in rows]
    v = [sh_ref[0, rw, 256:512] for rw in rows]
    kk = [sh_ref[0, rw, 512:768] for rw in rows]
    lw = [dr_ref[0, 0, rw, 0:256] for rw in rows]
    kd = [dr_ref[0, 0, rw, 256:512] for rw in rows]
    b = [dr_ref[0, 0, rw, 512:768] for rw in rows]

    lp = [_dot_exact_lhs(incl1, lw[i]) for i in cks]
    ltot = [jnp.sum(lw[i], 0, keepdims=True) for i in cks]
    e_neg = [jnp.exp(-lp[i]) for i in cks]
    kap = [kk[i] * jnp.exp(lp[i] - lw[i]) for i in cks]
    rt = [r[i] * jnp.exp(lp[i]) for i in cks]
    bt = [b[i] * e_neg[i] for i in cks]
    kt = [kd[i] * e_neg[i] for i in cks]
    e_rem = [jnp.exp(ltot[i] - lp[i]) for i in cks]
    bh = [b[i] * e_rem[i] for i in cks]
    kh = [kd[i] * e_rem[i] for i in cks]

    lhs = [jnp.concatenate([kap[i], rt[i]], axis=0).astype(BF16) for i in cks]
    gram_b = [lax.dot_general(lhs[i], expand_b(bt[i]), nt, preferred_element_type=F32) for i in cks]
    gram_k = [lax.dot_general(lhs[i], expand_b(kt[i]), nt, preferred_element_type=F32) for i in cks]
    mb_b = [(gram_b[i][0:C] * strict).astype(BF16) for i in cks]
    mrb = [gram_b[i][C:2 * C] * incl for i in cks]
    mk = [gram_k[i][0:C] * strict for i in cks]
    mrk = [gram_k[i][C:2 * C] * incl for i in cks]

    inv = [eye - (mb_b[i] * lev_ref[0]).astype(F32) for i in cks]
    for lv in range(1, lev_ref.shape[0]):
        inv_b = [inv[i].astype(BF16) for i in cks]
        step = [_dot(inv_b[i], expand_b(mb_b[i] * lev_ref[lv])) for i in cks]
        inv = [inv[i] - _dot(step[i].astype(BF16), expand_b(inv_b[i])) for i in cks]

    mv = [_dot(jnp.concatenate([mk[i], mrk[i]], axis=0).astype(BF16), expand_b(v[i])) for i in cks]
    wu_b = [_dot(inv[i].astype(BF16),
                 jnp.concatenate([expand_b(kap[i]), expand_b(mv[i][0:C])], axis=1)).astype(BF16)
            for i in cks]
    mw = [_dot(mrb[i].astype(BF16),
               jnp.concatenate([expand_b(wu_b[i][:, 0:n]), expand_b(wu_b[i][:, n:2 * n])], axis=1))
          for i in cks]
    bw = [lax.dot_general(bh[i].astype(BF16), wu_b[i], tn, preferred_element_type=F32) for i in cks]
    kv = [lax.dot_general(kh[i].astype(BF16), v[i].astype(BF16), tn, preferred_element_type=F32) for i in cks]
    for i in cks:
        rh_ref[0, 0, rows[i], :] = rt[i] - mw[i][:, 0:n]
        yh_ref[0, 0, rows[i], :] = mv[i][C:2 * C] - mw[i][:, n:2 * n]
        g_ref[0, 0, i] = eye * jnp.exp(ltot[i]) - _contract(bw[i][:, 0:n] * bd)
        hh_ref[0, 0, i] = _contract((kv[i] - bw[i][:, n:2 * n]) * bd)


def _rwkv_local_call(shared, dirs):
    nb, t, _ = shared.shape
    C = RWKV_CHUNK
    nch = t // C
    nck = 4 if nch % 4 == 0 else (2 if nch % 2 == 0 else 1)
    tm = nck * C
    n = 4 * C
    bd, eye, dirm, lev, incl1 = _rwkv_masks()
    kern = functools.partial(_rwkv_local_kernel, nck=nck)
    mat = jax.ShapeDtypeStruct((nb, 2, nch, C, 256), F32)
    seq = jax.ShapeDtypeStruct((nb, 2, t, 256), F32)
    mat_spec = pl.BlockSpec((1, 1, nck, C, 256), lambda b, d, i: (b, d, i, 0, 0))
    seq_spec = pl.BlockSpec((1, 1, tm, 256), lambda b, d, i: (b, d, i, 0))
    return pl.pallas_call(
        kern,
        out_shape=[mat, mat, seq, seq],
        grid=(nb, 2, nch // nck),
        in_specs=[pl.BlockSpec((1, tm, 768), lambda b, d, i: (b, i, 0)),
                  pl.BlockSpec((1, 1, tm, 768), lambda b, d, i: (b, d, i, 0)),
                  pl.BlockSpec(bd.shape, lambda b, d, i: (0, 0)),
                  pl.BlockSpec(eye.shape, lambda b, d, i: (0, 0)),
                  pl.BlockSpec((1, 2, C, n), lambda b, d, i: (d, 0, 0, 0)),
                  pl.BlockSpec(lev.shape, lambda b, d, i: (0, 0, 0)),
                  pl.BlockSpec((1, C, C), lambda b, d, i: (d, 0, 0))],
        out_specs=[mat_spec, mat_spec, seq_spec, seq_spec],
        compiler_params=_cparams(("arbitrary", "arbitrary", "arbitrary")),
        name="rwkv_local",
    )(shared, dirs, bd, eye, dirm, lev, incl1)


def _rwkv_scan_kernel(gf_ref, hf_ref, rf_ref, yf_ref, gb_ref, hb_ref, rb_ref, yb_ref,
                      of_ref, ob_ref, st_ref, *, nb):
    C = RWKV_CHUNK
    n = 4 * C
    j = pl.program_id(0)

    @pl.when(j == 0)
    def _():
        st_ref[...] = jnp.zeros(st_ref.shape, F32)

    rr = _iota((n, n), 0)
    cc = _iota((n, n), 1)
    bd = (rr >> 6) == (cc >> 6)
    dirs = ((gf_ref, hf_ref, rf_ref, yf_ref, of_ref), (gb_ref, hb_ref, rb_ref, yb_ref, ob_ref))
    seqs = [(d, b) for b in range(nb) for d in range(2)]

    w_hi, w_mid, l_hi, l_mid = [], [], [], []
    for d, b in seqs:
        g_ref, _, r_ref, _, _ = dirs[d]
        wh, wm, _ = _split3(_expand(st_ref[d, b], bd))
        lh, lm, _ = _split3(jnp.concatenate([r_ref[b, 0], g_ref[b, 0, 0]], axis=0))
        w_hi.append(wh)
        w_mid.append(wm)
        l_hi.append(lh)
        l_mid.append(lm)
    p_hi = [_dot(jnp.concatenate([l_hi[i], l_mid[i]], axis=0), w_hi[i]) for i in range(len(seqs))]
    p_mid = [_dot(l_hi[i], w_mid[i]) for i in range(len(seqs))]
    for i, (d, b) in enumerate(seqs):
        _, h_ref, _, y_ref, o_ref = dirs[d]
        tot = p_hi[i][0:2 * C] + (p_hi[i][2 * C:4 * C] + p_mid[i])
        o_ref[b] = tot[0:C] + y_ref[b, 0]
        st_ref[d, b] = tot[C:2 * C] + h_ref[b, 0, 0]


def _rwkv_scan_call(g, hh, rh, yh, lc):
    nb, _, nch, C, _ = g.shape
    t = nch * C
    ncc = lc // C

    def bwd(j):
        return jnp.where(j < ncc, ncc - 1 - j, nch - 1 + ncc - j)

    def mat_spec(d):
        if d == 0:
            return pl.BlockSpec((nb, 1, 1, C, 256), lambda j: (0, 0, j, 0, 0))
        return pl.BlockSpec((nb, 1, 1, C, 256), lambda j: (0, 1, bwd(j), 0, 0))

    def seq_spec(d):
        if d == 0:
            return pl.BlockSpec((nb, 1, C, 256), lambda j: (0, 0, j, 0))
        return pl.BlockSpec((nb, 1, C, 256), lambda j: (0, 1, bwd(j), 0))

    kern = functools.partial(_rwkv_scan_kernel, nb=nb)
    out = jax.ShapeDtypeStruct((nb, t, 256), F32)
    specs = []
    for d in range(2):
        specs += [mat_spec(d), mat_spec(d), seq_spec(d), seq_spec(d)]
    of, ob = pl.pallas_call(
        kern,
        out_shape=[out, out],
        grid=(nch,),
        in_specs=specs,
        out_specs=[pl.BlockSpec((nb, C, 256), lambda j: (0, j, 0)),
                   pl.BlockSpec((nb, C, 256), lambda j: (0, bwd(j), 0))],
        scratch_shapes=[pltpu.VMEM((2, nb, C, 256), F32)],
        compiler_params=_cparams(("arbitrary",)),
        name="rwkv_scan",
    )(g, hh, rh, yh, g, hh, rh, yh)
    return of, ob


def _merge_kernel(x_ref, mod_ref, oa_ref, oc_ref, od_ref, yf_ref, yb_ref, bon_ref, sg_ref, mg_ref,
                  bw_ref, ow_ref, mb_ref, gnw_ref, gnb_ref, lng_ref, lnb_ref, o_ref,
                  *, lc, tm, nb, row_off, alpha):
    b = pl.program_id(0)
    i = pl.program_id(1)

    def pair_cat(ref):
        return jnp.concatenate([ref[0, 0], ref[0, 1]], axis=1)

    def per_head(fn, x):
        return jnp.concatenate([fn(x[:, 0:LANES]), fn(x[:, LANES:2 * LANES])], axis=1)

    y = yf_ref[0] + yb_ref[0]
    mu = per_head(_half_sums, y) * (1.0 / RWKV_N)
    yc = y - mu
    var = per_head(_half_sums, yc * yc) * (1.0 / RWKV_N)
    ob = yc * lax.rsqrt(var + RWKV_GN_EPS) * gnw_ref[...] + gnb_ref[...] + bon_ref[0]

    outs = (pair_cat(oa_ref), ob, pair_cat(oc_ref), pair_cat(od_ref))
    acc = None
    for k in range(N_BRANCH):
        sg = sg_ref[0, :, k * BRANCH_W:(k + 1) * BRANCH_W].astype(F32)
        u = outs[k] * (sg * jax.nn.sigmoid(sg))
        z = _dot(u.astype(BF16), bw_ref[k])
        gate = jax.nn.sigmoid(mg_ref[0, :, k * D_MODEL:(k + 1) * D_MODEL].astype(F32)
                              + mb_ref[:, k * D_MODEL:(k + 1) * D_MODEL])
        acc = gate * z if acc is None else acc + gate * z
    y2 = _dot(acc.astype(BF16), ow_ref[...])

    row = row_off + i * tm + _iota((tm, D_MODEL), 0)
    gl = mod_ref[pl.ds(b, 1), 2 * D_MODEL:3 * D_MODEL]
    gc = mod_ref[pl.ds(nb, 1), 2 * D_MODEL:3 * D_MODEL]
    h = alpha * x_ref[0] + jnp.where(row < lc, gc, gl) * y2
    m = jnp.mean(h, -1, keepdims=True)
    hc = h - m
    var = jnp.mean(hc * hc, -1, keepdims=True)
    o_ref[0] = hc * lax.rsqrt(var + 1e-5) * lng_ref[...] + lnb_ref[...]


def _merge_call(xx, mod, oa, oc, od, yf, yb, bon, p, bw, ow, mb, gnw, gnb, lng, lnb, *, lc, row_off, alpha):
    nb, t, _ = xx.shape
    nrows = t - row_off
    tm = _pick_tile(math.gcd(nrows, row_off) if row_off else nrows, 256, 16)
    ro = row_off // tm
    kern = functools.partial(_merge_kernel, lc=lc, tm=tm, nb=nb, row_off=row_off, alpha=alpha)

    def full(a):
        return pl.BlockSpec(a.shape, lambda b, i: (0,) * a.ndim)

    pair_spec = pl.BlockSpec((1, 2, tm, LANES), lambda b, i: (b, 0, i, 0))
    return pl.pallas_call(
        kern,
        out_shape=jax.ShapeDtypeStruct((nb, nrows, D_MODEL), F32),
        grid=(nb, nrows // tm),
        in_specs=[
            pl.BlockSpec((1, tm, D_MODEL), lambda b, i: (b, i + ro, 0)),
            full(mod), pair_spec, pair_spec, pair_spec,
            pl.BlockSpec((1, tm, 256), lambda b, i: (b, i + ro, 0)),
            pl.BlockSpec((1, tm, 256), lambda b, i: (b, i + ro, 0)),
            pl.BlockSpec((1, tm, 256), lambda b, i: (b, i + ro, 0)),
            pl.BlockSpec((1, tm, D_MODEL), lambda b, i: (b, i + ro, P_SILU // D_MODEL)),
            pl.BlockSpec((1, tm, N_BRANCH * D_MODEL), lambda b, i: (b, i + ro, P_MERGE // (N_BRANCH * D_MODEL))),
            full(bw), full(ow), full(mb), full(gnw), full(gnb), full(lng), full(lnb),
        ],
        out_specs=pl.BlockSpec((1, tm, D_MODEL), lambda b, i: (b, i, 0)),
        compiler_params=_cparams(("arbitrary", "arbitrary")),
        name="merge",
    )(xx, mod, oa, oc, od, yf, yb, bon, p, p, bw, ow, mb, gnw, gnb, lng, lnb)


def _pack_in_w(w):
    z = lambda n: jnp.zeros((D_MODEL, n), w.dtype)
    a0 = 0
    b0 = 672
    c0 = b0 + 1280
    d0 = c0 + 768
    m0 = d0 + 1024
    cols = [
        w[:, a0:a0 + 256], w[:, a0 + 256:a0 + 384],
        z(64), w[:, a0 + 384:a0 + 416], z(32),
        w[:, c0:c0 + 256], w[:, c0 + 256:c0 + 384], w[:, c0 + 384:c0 + 512],
        w[:, d0:d0 + 256], w[:, d0 + 256:d0 + 512], w[:, d0 + 512:d0 + 768],
        z(256),
        w[:, b0:b0 + 1024],
        w[:, a0 + 416:a0 + 672], w[:, b0 + 1024:b0 + 1280], w[:, c0 + 512:c0 + 768], w[:, d0 + 768:d0 + 1024],
        w[:, m0:m0 + 4096],
    ]
    out = jnp.concatenate(cols, axis=1)
    assert out.shape[1] == P_WIDTH
    return out.astype(BF16)


def _pack_mla(w_uq, w_ukv):
    zq = jnp.zeros((MLA_Q_LORA, LANES - MLA_NOPE - MLA_ROPE), w_uq.dtype)
    zk = jnp.zeros((MLA_KV_LORA, LANES - MLA_NOPE), w_ukv.dtype)
    zv = jnp.zeros((MLA_KV_LORA, MLA_V), w_ukv.dtype)
    qc, kc, vc = [], [], []
    for h in range(MLA_HEADS):
        qc += [w_uq[:, h * 96:(h + 1) * 96], zq]
        kc += [w_ukv[:, h * 128:h * 128 + 64], zk]
        vh = w_ukv[:, h * 128 + 64:(h + 1) * 128]
        vc += [vh, zv] if h % 2 == 0 else [zv, vh]
    cat = lambda xs: jnp.concatenate(xs, axis=1).astype(BF16)
    return cat(qc), cat(kc), cat(vc)


def _rope_tables(row, col, rot_dim, lc, pattern):
    f32 = np.float32
    quarter = rot_dim // 4
    inv_freq = (f32(ROPE_BASE) ** (-np.arange(quarter, dtype=f32) / f32(quarter))).astype(f32)
    ang = np.concatenate([row[:, None] * inv_freq, col[:, None] * inv_freq], axis=-1).astype(f32)
    cos, sin = np.cos(ang).astype(f32), np.sin(ang).astype(f32)
    n = ang.shape[0]
    ones = lambda w: np.ones((n, w), f32)
    zeros = lambda w: np.zeros((n, w), f32)
    if pattern == "mla":
        c = np.concatenate([ones(64), cos, cos, ones(32)], axis=1)
        s = np.concatenate([zeros(64), -sin, sin, zeros(32)], axis=1)
    else:
        reps = LANES // rot_dim
        c = np.concatenate([cos, cos] * reps, axis=1)
        s = np.concatenate([-sin, sin] * reps, axis=1)
    c = np.concatenate([np.ones((lc, LANES), f32), c], axis=0)
    s = np.concatenate([np.zeros((lc, LANES), f32), s], axis=0)
    return jnp.asarray(c), jnp.asarray(s)


def kernel(x, c, ctx, c_ctx, ada_w, ada_b, in_w, mla_q_norm, mla_w_uq, mla_kv_norm, mla_w_ukv, rwkv_mu, rwkv_w0, rwkv_w_up, rwkv_a0, rwkv_a_up, rwkv_k_k, rwkv_k_a, rwkv_r_k, rwkv_gn_w, rwkv_gn_b, gqa_q_norm, gqa_k_norm, diff_lambda, diff_subln, merge_b, branch_w, out_w, ln_g, ln_b):
    nb, ll, _ = x.shape
    lc = ctx.shape[1]
    depth = ada_w.shape[0]
    t = lc + ll
    alpha = (2 * depth) ** 0.25

    rows = ll // GRID_W
    row = np.repeat(np.arange(rows), GRID_W).astype(np.float32)
    col = np.tile(np.arange(GRID_W), rows).astype(np.float32)
    tabs = (_rope_tables(row, col, MLA_ROPE, lc, "mla") + _rope_tables(row, col, HEAD_DIM, lc, "tile")
            + _rope_tables(row, col, DIFF_D, lc, "tile"))

    crows = -(-(nb + 1) // 8) * 8
    cvec = jnp.concatenate([c, c_ctx[None, :], jnp.zeros((crows - nb - 1, D_MODEL), F32)], axis=0)
    mods = _ada_call(cvec, ada_w, ada_b)

    xx = jnp.concatenate([ctx, x], axis=1)
    tile2 = lambda a: jnp.concatenate([a, a])[None, :]
    for l in range(depth):
        last = l == depth - 1
        lam_init = 0.8 - 0.6 * math.exp(-0.3 * l)
        mod = mods[l]
        p = _inproj_call(xx, mod, _pack_in_w(in_w[l]), lc)

        wuq, wk, wv = _pack_mla(mla_w_uq[l], mla_w_ukv[l])
        qa, ka, va, qc, kc, vc, qd, kd, vd = _prep_call(
            p, tabs, (mla_q_norm[l][None, :], wuq, mla_kv_norm[l][None, :], wk, wv,
                      tile2(gqa_q_norm[l]), tile2(gqa_k_norm[l])))
        lam_p = jnp.zeros((8, LANES), F32).at[0:4, 0:DIFF_D].set(diff_lambda[l])
        sub = tile2(diff_subln[l])
        att = lambda q, k, v, **kw: _attn_call(q, k, v, lk=t, q_off=lc, nq_rows=ll, **kw)
        oa = att(qa, ka, va, n_maps=1, shared_k=False)
        oc = att(qc, kc, vc, n_maps=1, shared_k=True)
        od = att(qd, kd, vd, n_maps=2, shared_k=False, extra=(lam_p, sub), lam_init=lam_init)
        if not last:
            catt = lambda q, k, v, **kw: _attn_call(q, k, v, lk=lc, q_off=0, nq_rows=lc, **kw)
            merge_rows = lambda lat_o, ctx_o: jnp.concatenate([ctx_o, lat_o], axis=2)
            oa = merge_rows(oa, catt(qa, ka, va, n_maps=1, shared_k=False))
            oc = merge_rows(oc, catt(qc, kc, vc, n_maps=1, shared_k=True))
            od = merge_rows(od, catt(qd, kd, vd, n_maps=2, shared_k=False, extra=(lam_p, sub), lam_init=lam_init))

        wl = jnp.zeros((256, 1024), F32)
        for d in range(2):
            wl = wl.at[d * 64:(d + 1) * 64, d * 256:(d + 1) * 256].set(rwkv_w_up[l, d])
            wl = wl.at[128 + d * 64:128 + (d + 1) * 64, 512 + d * 256:512 + (d + 1) * 256].set(rwkv_a_up[l, d])
        bias = jnp.concatenate([rwkv_w0[l, 0], rwkv_w0[l, 1], rwkv_a0[l, 0], rwkv_a0[l, 1]])[None, :]
        shared, dirs, bon = _rwkv_prep_call(p, rwkv_mu[l], wl.astype(BF16), bias, rwkv_k_k[l][None, :],
                                            rwkv_k_a[l][None, :], rwkv_r_k[l][None, :], lc)
        g, hh, rh, yh = _rwkv_local_call(shared, dirs)
        yf, yb = _rwkv_scan_call(g, hh, rh, yh, lc)

        xx = _merge_call(xx, mod, oa, oc, od, yf, yb, bon, p,
                         branch_w[l].astype(BF16), out_w[l].astype(BF16), merge_b[l][None, :],
                         rwkv_gn_w[l][None, :], rwkv_gn_b[l][None, :], ln_g[l][None, :], ln_b[l][None, :],
                         lc=lc, row_off=lc if last else 0, alpha=alpha)
    return xx
```

```python
import functools
import math

import jax
import jax.numpy as jnp
import numpy as np
from jax import lax
from jax.experimental import pallas as pl
from jax.experimental.pallas import tpu as pltpu

F32 = jnp.float32
BF16 = jnp.bfloat16

D_MODEL = 1024
GRID_W = 64
ROPE_BASE = 10000.0
HEAD_DIM = 64
N_BRANCH = 4
BRANCH_W = 256
MLA_HEADS = 4
MLA_Q_LORA = 256
MLA_KV_LORA = 128
MLA_NOPE = 64
MLA_ROPE = 32
MLA_V = 64
MLA_SCALE = (MLA_NOPE + MLA_ROPE) ** -0.5
RWKV_N = 64
RWKV_GN_EPS = 64e-5
RWKV_DECAY_SCALE = math.exp(-0.5)
GQA_SCALE = HEAD_DIM ** -0.5
DIFF_D = 32
DIFF_SCALE = DIFF_D ** -0.5
LOG2E = 1.0 / math.log(2.0)

LANES = 128
VMEM_LIMIT = 56 * 1024 * 1024

P_AQ, P_AKV, P_AKR = 0, 256, 384
P_CQ, P_CK, P_CV = 512, 768, 896
P_DQ, P_DK, P_DV = 1024, 1280, 1536
P_RWKV = 2048
P_SILU = 3072
P_MERGE = 4096
P_WIDTH = 8192
ATTN_W = 2048

RWKV_CHUNK = 64
ATTN_TQ = 256
ATTN_MAPS = 16


def _cparams(sem):
    return pltpu.CompilerParams(dimension_semantics=sem, vmem_limit_bytes=VMEM_LIMIT)


def _split3(x):
    h = x.astype(BF16)
    r = x - h.astype(F32)
    m = r.astype(BF16)
    l = (r - m.astype(F32)).astype(BF16)
    return h, m, l


def _dot(a, b):
    return jnp.dot(a, b, preferred_element_type=F32)


def _dot_hi(a, b):
    ah, am, _ = _split3(a)
    bh, bm, _ = _split3(b)
    return _dot(ah, bh) + (_dot(ah, bm) + _dot(am, bh))


def _dot_exact_lhs(a_bf16, b):
    bh, bm, bl = _split3(b)
    return _dot(a_bf16, bh) + (_dot(a_bf16, bm) + _dot(a_bf16, bl))


def _iota(shape, dim):
    return lax.broadcasted_iota(jnp.int32, shape, dim)


def _ada_kernel(c_ref, w_ref, b_ref, o_ref):
    c = c_ref[...]
    s = c * jax.nn.sigmoid(c)
    o_ref[0] = _dot_hi(s, w_ref[0]) + b_ref[0]


def _ada_call(cvec, ada_w, ada_b):
    depth = ada_w.shape[0]
    rows = cvec.shape[0]
    nblk = 3
    return pl.pallas_call(
        _ada_kernel,
        out_shape=jax.ShapeDtypeStruct((depth, rows, 3 * D_MODEL), F32),
        grid=(depth, nblk),
        in_specs=[
            pl.BlockSpec((rows, D_MODEL), lambda l, j: (0, 0)),
            pl.BlockSpec((1, D_MODEL, D_MODEL), lambda l, j: (l, 0, j)),
            pl.BlockSpec((1, 1, D_MODEL), lambda l, j: (l, 0, j)),
        ],
        out_specs=pl.BlockSpec((1, rows, D_MODEL), lambda l, j: (l, 0, j)),
        compiler_params=_cparams(("arbitrary", "arbitrary")),
        name="ada",
    )(cvec, ada_w, ada_b.reshape(depth, 1, 3 * D_MODEL))


def _inproj_kernel(x_ref, mod_ref, w_ref, o_ref, xm_ref, *, lc, tm, nb):
    b = pl.program_id(0)
    i = pl.program_id(1)
    j = pl.program_id(2)

    @pl.when(j == 0)
    def _():
        x = x_ref[0]
        mu = jnp.mean(x, -1, keepdims=True)
        xc = x - mu
        var = jnp.mean(xc * xc, -1, keepdims=True)
        xn = xc * lax.rsqrt(var + 1e-6)
        row = i * tm + _iota((tm, D_MODEL), 0)
        is_ctx = row < lc
        ml = mod_ref[pl.ds(b, 1), :]
        mc = mod_ref[pl.ds(nb, 1), :]
        shift = jnp.where(is_ctx, mc[:, 0:D_MODEL], ml[:, 0:D_MODEL])
        scale = jnp.where(is_ctx, mc[:, D_MODEL:2 * D_MODEL], ml[:, D_MODEL:2 * D_MODEL])
        xm_ref[...] = (xn * (1.0 + scale) + shift).astype(BF16)

    o_ref[0] = _dot(xm_ref[...], w_ref[...]).astype(BF16)


def _inproj_call(xx, mod, w_packed, lc):
    nb, t, _ = xx.shape
    tm = _pick_tile(t, 1088, 16)
    tn = 1024
    kern = functools.partial(_inproj_kernel, lc=lc, tm=tm, nb=nb)
    return pl.pallas_call(
        kern,
        out_shape=jax.ShapeDtypeStruct((nb, t, P_WIDTH), BF16),
        grid=(nb, t // tm, P_WIDTH // tn),
        in_specs=[
            pl.BlockSpec((1, tm, D_MODEL), lambda b, i, j: (b, i, 0)),
            pl.BlockSpec(mod.shape, lambda b, i, j: (0, 0)),
            pl.BlockSpec((D_MODEL, tn), lambda b, i, j: (0, j)),
        ],
        out_specs=pl.BlockSpec((1, tm, tn), lambda b, i, j: (b, i, j)),
        scratch_shapes=[pltpu.VMEM((tm, D_MODEL), BF16)],
        compiler_params=_cparams(("arbitrary", "arbitrary", "arbitrary")),
        name="inproj",
    )(xx, mod, w_packed)


def _pick_tile(n, cap, mult):
    best = None
    for d in range(mult, min(n, cap) + 1, mult):
        if n % d == 0:
            best = d
    assert best is not None, (n, cap, mult)
    return best


def _swap_halves(x, half):
    n = x.shape[-1]
    first = (_iota(x.shape, 1) & (2 * half - 1)) < half
    up = pltpu.roll(x, n - half, 1)
    dn = pltpu.roll(x, half, 1)
    return jnp.where(first, up, dn)


def _half_sums(x):
    lo = _iota(x.shape, 1) < HEAD_DIM
    s_lo = jnp.sum(jnp.where(lo, x, 0.0), -1, keepdims=True)
    s_hi = jnp.sum(jnp.where(lo, 0.0, x), -1, keepdims=True)
    return jnp.where(lo, s_lo, s_hi)


def _prep_kernel(p_ref, cosa_ref, sina_ref, cosc_ref, sinc_ref, cosd_ref, sind_ref,
                 qn_ref, wuq_ref, kvn_ref, wk_ref, wv_ref, gq_ref, gk_ref,
                 qa_ref, ka_ref, va_ref, qc_ref, kc_ref, vc_ref, qd_ref, kd_ref, vd_ref):
    def seg(off, w):
        return p_ref[0, :, off:off + w].astype(F32)

    lane = _iota((p_ref.shape[1], LANES), 1)
    lo = lane < HEAD_DIM

    cosa, sina = cosa_ref[...], sina_ref[...]
    ql = seg(P_AQ, MLA_Q_LORA)
    ql = ql * lax.rsqrt(jnp.mean(ql * ql, -1, keepdims=True) + 1e-6) * qn_ref[...]
    q = _dot(ql.astype(BF16), wuq_ref[...])
    kvl = seg(P_AKV, MLA_KV_LORA)
    kvl = (kvl * lax.rsqrt(jnp.mean(kvl * kvl, -1, keepdims=True) + 1e-6) * kvn_ref[...]).astype(BF16)
    kn = _dot(kvl, wk_ref[...])
    vv = _dot(kvl, wv_ref[...])
    kr = seg(P_AKR, LANES)
    kr = kr * cosa + _swap_halves(kr, MLA_ROPE // 2) * sina
    for h in range(MLA_HEADS):
        qh = q[:, h * LANES:(h + 1) * LANES]
        qh = qh * cosa + _swap_halves(qh, MLA_ROPE // 2) * sina
        qa_ref[0, h] = (qh * (MLA_SCALE * LOG2E)).astype(BF16)
        ka_ref[0, h] = (kn[:, h * LANES:(h + 1) * LANES] + kr).astype(BF16)
        va_ref[0, h] = vv[:, h * LANES:(h + 1) * LANES].T.astype(BF16)

    cosc, sinc = cosc_ref[...], sinc_ref[...]

    def norm_rope(x, g):
        x = x * lax.rsqrt(_half_sums(x * x) * (1.0 / HEAD_DIM) + 1e-6) * g
        return x * cosc + _swap_halves(x, HEAD_DIM // 2) * sinc

    def split_heads(blk):
        return jnp.where(lo, blk, 0.0), jnp.where(lo, pltpu.roll(blk, HEAD_DIM, 1), 0.0)

    for pr in range(2):
        qb = norm_rope(seg(P_CQ + pr * LANES, LANES), gq_ref[...]) * (GQA_SCALE * LOG2E)
        q0, q1 = split_heads(qb)
        qc_ref[0, 2 * pr] = q0.astype(BF16)
        qc_ref[0, 2 * pr + 1] = q1.astype(BF16)
    k0, k1 = split_heads(norm_rope(seg(P_CK, LANES), gk_ref[...]))
    kc_ref[0, 0] = k0.astype(BF16)
    kc_ref[0, 1] = k1.astype(BF16)
    vb = seg(P_CV, LANES)
    v_g0 = jnp.where(lo, vb, 0.0)
    v_g1 = jnp.where(lo, 0.0, vb)
    vc_ref[0, 0] = v_g0.T.astype(BF16)
    vc_ref[0, 1] = pltpu.roll(v_g0, HEAD_DIM, 1).T.astype(BF16)
    vc_ref[0, 2] = pltpu.roll(v_g1, HEAD_DIM, 1).T.astype(BF16)
    vc_ref[0, 3] = v_g1.T.astype(BF16)

    cosd, sind = cosd_ref[...], sind_ref[...]
    piece = lane < DIFF_D
    for pr in range(2):
        qb = seg(P_DQ + pr * LANES, LANES)
        qb = (qb * cosd + _swap_halves(qb, DIFF_D // 2) * sind) * (DIFF_SCALE * LOG2E)
        kb = seg(P_DK + pr * LANES, LANES)
        kb = kb * cosd + _swap_halves(kb, DIFF_D // 2) * sind
        vb = seg(P_DV + pr * LANES, LANES)
        for hh in range(2):
            for m in range(2):
                off = hh * HEAD_DIM + m * DIFF_D
                idx = (2 * pr + hh) * 2 + m
                qs = qb if off == 0 else pltpu.roll(qb, LANES - off, 1)
                ks = kb if off == 0 else pltpu.roll(kb, LANES - off, 1)
                qd_ref[0, idx] = jnp.where(piece, qs, 0.0).astype(BF16)
                kd_ref[0, idx] = jnp.where(piece, ks, 0.0).astype(BF16)
        vd_ref[0, 2 * pr] = jnp.where(lo, vb, 0.0).T.astype(BF16)
        vd_ref[0, 2 * pr + 1] = jnp.where(lo, 0.0, vb).T.astype(BF16)


def _prep_call(p, tabs, wts):
    nb, t, _ = p.shape
    tm = _pick_tile(t, 256, 16)
    row_spec = pl.BlockSpec((tm, LANES), lambda b, i: (i, 0))

    def full(a):
        return pl.BlockSpec(a.shape, lambda b, i: (0,) * a.ndim)

    def head_out(nh):
        return (jax.ShapeDtypeStruct((nb, nh, t, LANES), BF16),
                pl.BlockSpec((1, nh, tm, LANES), lambda b, i: (b, 0, i, 0)))

    def head_out_t(nh):
        return (jax.ShapeDtypeStruct((nb, nh, LANES, t), BF16),
                pl.BlockSpec((1, nh, LANES, tm), lambda b, i: (b, 0, 0, i)))

    outs = [head_out(4), head_out(4), head_out_t(4), head_out(4), head_out(2), head_out_t(4),
            head_out(8), head_out(8), head_out_t(4)]
    return pl.pallas_call(
        _prep_kernel,
        out_shape=[o[0] for o in outs],
        grid=(nb, t // tm),
        in_specs=[pl.BlockSpec((1, tm, ATTN_W), lambda b, i: (b, i, 0))]
        + [row_spec] * 6 + [full(w) for w in wts],
        out_specs=[o[1] for o in outs],
        compiler_params=_cparams(("arbitrary", "arbitrary")),
        name="attn_prep",
    )(p, *tabs, *wts)


def _attn_kernel(*refs, hp, n_maps, qsub, tq, shared_k, lk, tk, lam_init):
    if n_maps == 2:
        q_ref, k_ref, vt_ref, lam_ref, sub_ref, o_ref, s_scr, p_scr = refs
    else:
        q_ref, k_ref, vt_ref, o_ref, s_scr, p_scr = refs
    nchunks = lk // tk
    maps = [(qs, hh * n_maps + mm, hh // 2 if shared_k else hh * n_maps + mm, hh)
            for qs in range(qsub) for hh in range(hp) for mm in range(n_maps)]
    per_sub = hp * n_maps

    def scores(j):
        qs, qi, ki, _ = maps[j]
        s_scr[j % 2] = lax.dot_general(k_ref[0, ki], q_ref[0, qi, qs * tq:(qs + 1) * tq, :],
                                       (((1,), (1,)), ((), ())), preferred_element_type=F32)

    def softmax_rows(j):
        buf = j % 2
        mx = None
        for c in range(nchunks):
            sc = s_scr[buf, c * tk:(c + 1) * tk, :]
            mx = sc if mx is None else jnp.maximum(mx, sc)
        m = jnp.max(mx, 0, keepdims=True)
        ls = None
        for c in range(nchunks):
            p = jnp.exp2(s_scr[buf, c * tk:(c + 1) * tk, :] - m)
            p_scr[buf, c * tk:(c + 1) * tk, :] = p.astype(BF16)
            ls = p if ls is None else ls + p
        return jnp.sum(ls, 0, keepdims=True)

    def weighted_values(j, l):
        ot = _dot(vt_ref[0, maps[j][3]], p_scr[j % 2])
        return (ot / l).T

    outs = []
    scores(0)
    for j in range(len(maps)):
        if j + 1 < len(maps):
            scores(j + 1)
        outs.append(weighted_values(j, softmax_rows(j)))

    for qs in range(qsub):
        for pr in range(hp // 2):
            total = None
            for hh in (2 * pr, 2 * pr + 1):
                o = outs[qs * per_sub + hh * n_maps]
                if n_maps == 2:
                    lp = lam_ref[...]
                    lam = (jnp.exp(jnp.sum(lp[0:1] * lp[1:2], keepdims=True))
                           - jnp.exp(jnp.sum(lp[2:3] * lp[3:4], keepdims=True)) + lam_init)
                    o = o - lam * outs[qs * per_sub + hh * n_maps + 1]
                    ms = jnp.sum(o * o, -1, keepdims=True) * (1.0 / HEAD_DIM)
                    o = o * lax.rsqrt(ms + 1e-5) * sub_ref[...] * (1.0 - lam_init)
                total = o if total is None else total + o
            o_ref[0, pr, qs * tq:(qs + 1) * tq, :] = total


def _attn_call(q, k, v, *, n_maps, shared_k, lk, q_off, nq_rows, extra=(), lam_init=0.0):
    nb = q.shape[0]
    hp = 4
    tq = _pick_tile(nq_rows, ATTN_TQ, 16)
    qsub = max(1, ATTN_MAPS // (hp * n_maps))
    while nq_rows % (qsub * tq):
        qsub //= 2
    tb = qsub * tq
    tk = 256 if lk % 256 == 0 else _pick_tile(lk, 256, LANES)
    kh = hp // 2 if shared_k else hp * n_maps
    kern = functools.partial(_attn_kernel, hp=hp, n_maps=n_maps, qsub=qsub, tq=tq, shared_k=shared_k,
                             lk=lk, tk=tk, lam_init=lam_init)
    in_specs = [
        pl.BlockSpec((pl.Element(1), pl.Element(hp * n_maps), pl.Element(tb), pl.Element(LANES)),
                     lambda b, i: (b, 0, pl.multiple_of(i * tb + q_off, 16), 0)),
        pl.BlockSpec((1, kh, lk, LANES), lambda b, i: (b, 0, 0, 0)),
        pl.BlockSpec((1, hp, LANES, lk), lambda b, i: (b, 0, 0, 0)),
    ] + [pl.BlockSpec(e.shape, lambda b, i: (0, 0)) for e in extra]
    return pl.pallas_call(
        kern,
        out_shape=jax.ShapeDtypeStruct((nb, 2, nq_rows, LANES), F32),
        grid=(nb, nq_rows // tb),
        in_specs=in_specs,
        out_specs=pl.BlockSpec((1, hp // 2, tb, LANES), lambda b, i: (b, 0, i, 0)),
        scratch_shapes=[pltpu.VMEM((2, lk, tq), F32), pltpu.VMEM((2, lk, tq), BF16)],
        compiler_params=_cparams(("arbitrary", "arbitrary")),
        name="attn",
    )(q, k, v, *extra)


def _rwkv_prep_kernel(cur_ref, prv_ref, nxt_ref, mu_ref, wl_ref, bias_ref, kk_ref, ka_ref, rk_ref,
                      sh_ref, dr_ref, bon_ref, *, lc, t_total, tm, halo):
    i = pl.program_id(1)
    cur = cur_ref[0]
    r_i = _iota((tm, tm), 0)
    c_i = _iota((tm, tm), 1)
    s_dn = jnp.where(c_i == r_i - 1, 1.0, 0.0).astype(BF16)
    s_up = jnp.where(c_i == r_i + 1, 1.0, 0.0).astype(BF16)
    row = _iota((tm, D_MODEL), 0)
    tg = i * tm + row
    prv = jnp.where(row == 0, prv_ref[0].astype(F32)[halo - 1:halo, :], _dot(s_dn, cur))
    nxt = jnp.where(row == tm - 1, nxt_ref[0].astype(F32)[0:1, :], _dot(s_up, cur))
    prv = jnp.where((tg == 0) | (tg == lc), 0.0, prv)
    nxt = jnp.where((tg == lc - 1) | (tg == t_total - 1), 0.0, nxt)
    x = cur.astype(F32)
    sh = x + mu_ref[0:1, :] * (prv - x) + mu_ref[1:2, :] * (nxt - x)

    r = sh[:, 0:256]
    k = sh[:, 256:512]
    v = sh[:, 512:768]
    lora = sh[:, 768:1024]
    lora = jnp.where(_iota(lora.shape, 1) < 2 * RWKV_N, jnp.tanh(lora), lora)
    z = _dot(lora.astype(BF16), wl_ref[...]) + bias_ref[...]

    def per_head(fn, x):
        return jnp.concatenate([fn(x[:, 0:LANES]), fn(x[:, LANES:2 * LANES])], axis=1)

    kq = k * kk_ref[...]
    kk = kq * lax.rsqrt(per_head(_half_sums, kq * kq) + 1e-12)
    sh_ref[0, :, 0:256] = r
    sh_ref[0, :, 256:512] = v
    sh_ref[0, :, 512:768] = kk
    hs = None
    for d in range(2):
        logw = -RWKV_DECAY_SCALE * jax.nn.sigmoid(z[:, d * 256:(d + 1) * 256])
        a = jax.nn.sigmoid(z[:, 512 + d * 256:512 + (d + 1) * 256])
        kd = k * (1.0 + (a - 1.0) * ka_ref[...])
        dr_ref[0, d, :, 0:256] = logw
        dr_ref[0, d, :, 256:512] = kd
        dr_ref[0, d, :, 512:768] = a * kk
        s = per_head(_half_sums, r * kd * rk_ref[...])
        hs = s if hs is None else hs + s
    bon_ref[0] = hs * v


def _rwkv_prep_call(p, mu, wl, bias, k_k, k_a, r_k, lc):
    nb, t, _ = p.shape
    tm = _pick_tile(t, 256, 16)
    halo = 16
    cb = P_RWKV // D_MODEL
    hb = tm // halo
    last = t // halo - 1
    kern = functools.partial(_rwkv_prep_kernel, lc=lc, t_total=t, tm=tm, halo=halo)

    def full(a):
        return pl.BlockSpec(a.shape, lambda b, i: (0,) * a.ndim)

    return pl.pallas_call(
        kern,
        out_shape=[jax.ShapeDtypeStruct((nb, t, 768), F32),
                   jax.ShapeDtypeStruct((nb, 2, t, 768), F32),
                   jax.ShapeDtypeStruct((nb, t, 256), F32)],
        grid=(nb, t // tm),
        in_specs=[
            pl.BlockSpec((1, tm, D_MODEL), lambda b, i: (b, i, cb)),
            pl.BlockSpec((1, halo, D_MODEL), lambda b, i: (b, jnp.maximum(i * hb - 1, 0), cb)),
            pl.BlockSpec((1, halo, D_MODEL), lambda b, i: (b, jnp.minimum((i + 1) * hb, last), cb)),
            full(mu), full(wl), full(bias), full(k_k), full(k_a), full(r_k),
        ],
        out_specs=[pl.BlockSpec((1, tm, 768), lambda b, i: (b, i, 0)),
                   pl.BlockSpec((1, 2, tm, 768), lambda b, i: (b, 0, i, 0)),
                   pl.BlockSpec((1, tm, 256), lambda b, i: (b, i, 0))],
        compiler_params=_cparams(("arbitrary", "arbitrary")),
        name="rwkv_prep",
    )(p, p, p, mu, wl, bias, k_k, k_a, r_k)


def _expand(x, bd):
    return jnp.where(bd, jnp.concatenate([x] * 4, axis=0), 0.0)


def _contract(x):
    c = x.shape[0] // 4
    return (x[0:c] + x[c:2 * c]) + (x[2 * c:3 * c] + x[3 * c:4 * c])


def _rwkv_masks():
    C = RWKV_CHUNK
    n = 4 * C
    rr, cc = np.indices((n, n))
    bd = ((rr // C) == (cc // C)).astype(np.float32)
    tr, tc = np.indices((C, n))
    tc = tc % C
    eye = (tr == tc).astype(np.float32)
    dirm = np.stack([np.stack([tc < tr, tc <= tr]), np.stack([tc > tr, tc >= tr])]).astype(np.float32)
    lev = [(tr >> 1) == (tc >> 1)]
    s = 1
    while (1 << s) < C:
        lev.append(((tr >> s) != (tc >> s)) & ((tr >> (s + 1)) == (tc >> (s + 1))))
        s += 1
    r1, c1 = np.indices((C, C))
    incl1 = np.stack([c1 <= r1, c1 >= r1]).astype(np.float32)
    return (jnp.asarray(bd), jnp.asarray(eye), jnp.asarray(dirm),
            jnp.asarray(np.stack(lev).astype(np.float32), BF16), jnp.asarray(incl1, BF16))


def _rwkv_local_kernel(sh_ref, dr_ref, bd_ref, eye_ref, dirm_ref, lev_ref, incl1_ref,
                       g_ref, hh_ref, rh_ref, yh_ref, *, nck):
    C = RWKV_CHUNK
    n = 4 * C
    bd = bd_ref[...]
    eye = eye_ref[...]
    bd_b = bd.astype(BF16)
    strict = dirm_ref[0, 0]
    incl = dirm_ref[0, 1]
    incl1 = incl1_ref[0]
    nt = (((1,), (1,)), ((), ()))
    tn = (((0,), (0,)), ((), ()))

    def expand_b(x):
        return jnp.concatenate([x.astype(BF16)] * 4, axis=0) * bd_b

    cks = range(nck)
    rows = [slice(ck * C, (ck + 1) * C) for ck in cks]
    r = [sh_ref[0, rw, 0:256] for rw in rows]
    v = [sh_ref[0, rw, 256:512] for rw in rows]
    kk = [sh_ref[0, rw, 512:768] for rw in rows]
    lw = [dr_ref[0, 0, rw, 0:256] for rw in rows]
    kd = [dr_ref[0, 0, rw, 256:512] for rw in rows]
    b = [dr_ref[0, 0, rw, 512:768] for rw in rows]

    lp = [_dot_exact_lhs(incl1, lw[i]) for i in cks]
    ltot = [jnp.sum(lw[i], 0, keepdims=True) for i in cks]
    e_neg = [jnp.exp(-lp[i]) for i in cks]
    kap = [kk[i] * jnp.exp(lp[i] - lw[i]) for i in cks]
    rt = [r[i] * jnp.exp(lp[i]) for i in cks]
    bt = [b[i] * e_neg[i] for i in cks]
    kt = [kd[i] * e_neg[i] for i in cks]
    e_rem = [jnp.exp(ltot[i] - lp[i]) for i in cks]
    bh = [b[i] * e_rem[i] for i in cks]
    kh = [kd[i] * e_rem[i] for i in cks]

    lhs = [jnp.concatenate([kap[i], rt[i]], axis=0).astype(BF16) for i in cks]
    gram_b = [lax.dot_general(lhs[i], expand_b(bt[i]), nt, preferred_element_type=F32) for i in cks]
    gram_k = [lax.dot_general(lhs[i], expand_b(kt[i]), nt, preferred_element_type=F32) for i in cks]
    mb_b = [(gram_b[i][0:C] * strict).astype(BF16) for i in cks]
    mrb = [gram_b[i][C:2 * C] * incl for i in cks]
    mk = [gram_k[i][0:C] * strict for i in cks]
    mrk = [gram_k[i][C:2 * C] * incl for i in cks]

    inv = [eye - (mb_b[i] * lev_ref[0]).astype(F32) for i in cks]
    for lv in range(1, lev_ref.shape[0]):
        inv_b = [inv[i].astype(BF16) for i in cks]
        step = [_dot(inv_b[i], expand_b(mb_b[i] * lev_ref[lv])) for i in cks]
        inv = [inv[i] - _dot(step[i].astype(BF16), expand_b(inv_b[i])) for i in cks]

    mv = [_dot(jnp.concatenate([mk[i], mrk[i]], axis=0).astype(BF16), expand_b(v[i])) for i in cks]
    wu_b = [_dot(inv[i].astype(BF16),
                 jnp.concatenate([expand_b(kap[i]), expand_b(mv[i][0:C])], axis=1)).astype(BF16)
            for i in cks]
    mw = [_dot(mrb[i].astype(BF16),
               jnp.concatenate([expand_b(wu_b[i][:, 0:n]), expand_b(wu_b[i][:, n:2 * n])], axis=1))
          for i in cks]
    bw = [lax.dot_general(bh[i].astype(BF16), wu_b[i], tn, preferred_element_type=F32) for i in cks]
    kv = [lax.dot_general(kh[i].astype(BF16), v[i].astype(BF16), tn, preferred_element_type=F32) for i in cks]
    for i in cks:
        rh_ref[0, 0, rows[i], :] = rt[i] - mw[i][:, 0:n]
        yh_ref[0, 0, rows[i], :] = mv[i][C:2 * C] - mw[i][:, n:2 * n]
        g_ref[0, 0, i] = eye * jnp.exp(ltot[i]) - _contract(bw[i][:, 0:n] * bd)
        hh_ref[0, 0, i] = _contract((kv[i] - bw[i][:, n:2 * n]) * bd)


def _rwkv_local_call(shared, dirs):
    nb, t, _ = shared.shape
    C = RWKV_CHUNK
    nch = t // C
    nck = 4 if nch % 4 == 0 else (2 if nch % 2 == 0 else 1)
    tm = nck * C
    n = 4 * C
    bd, eye, dirm, lev, incl1 = _rwkv_masks()
    kern = functools.partial(_rwkv_local_kernel, nck=nck)
    mat = jax.ShapeDtypeStruct((nb, 2, nch, C, 256), F32)
    seq = jax.ShapeDtypeStruct((nb, 2, t, 256), F32)
    mat_spec = pl.BlockSpec((1, 1, nck, C, 256), lambda b, d, i: (b, d, i, 0, 0))
    seq_spec = pl.BlockSpec((1, 1, tm, 256), lambda b, d, i: (b, d, i, 0))
    return pl.pallas_call(
        kern,
        out_shape=[mat, mat, seq, seq],
        grid=(nb, 2, nch // nck),
        in_specs=[pl.BlockSpec((1, tm, 768), lambda b, d, i: (b, i, 0)),
                  pl.BlockSpec((1, 1, tm, 768), lambda b, d, i: (b, d, i, 0)),
                  pl.BlockSpec(bd.shape, lambda b, d, i: (0, 0)),
                  pl.BlockSpec(eye.shape, lambda b, d, i: (0, 0)),
                  pl.BlockSpec((1, 2, C, n), lambda b, d, i: (d, 0, 0, 0)),
                  pl.BlockSpec(lev.shape, lambda b, d, i: (0, 0, 0)),
                  pl.BlockSpec((1, C, C), lambda b, d, i: (d, 0, 0))],
        out_specs=[mat_spec, mat_spec, seq_spec, seq_spec],
        compiler_params=_cparams(("arbitrary", "arbitrary", "arbitrary")),
        name="rwkv_local",
    )(shared, dirs, bd, eye, dirm, lev, incl1)


def _rwkv_scan_kernel(gf_ref, hf_ref, rf_ref, yf_ref, gb_ref, hb_ref, rb_ref, yb_ref,
                      of_ref, ob_ref, st_ref, *, nb):
    C = RWKV_CHUNK
    n = 4 * C
    j = pl.program_id(0)

    @pl.when(j == 0)
    def _():
        st_ref[...] = jnp.zeros(st_ref.shape, F32)

    rr = _iota((n, n), 0)
    cc = _iota((n, n), 1)
    bd = (rr >> 6) == (cc >> 6)
    dirs = ((gf_ref, hf_ref, rf_ref, yf_ref, of_ref), (gb_ref, hb_ref, rb_ref, yb_ref, ob_ref))
    seqs = [(d, b) for b in range(nb) for d in range(2)]

    w_hi, w_mid, l_hi, l_mid = [], [], [], []
    for d, b in seqs:
        g_ref, _, r_ref, _, _ = dirs[d]
        wh, wm, _ = _split3(_expand(st_ref[d, b], bd))
        lh, lm, _ = _split3(jnp.concatenate([r_ref[b, 0], g_ref[b, 0, 0]], axis=0))
        w_hi.append(wh)
        w_mid.append(wm)
        l_hi.append(lh)
        l_mid.append(lm)
    p_hi = [_dot(jnp.concatenate([l_hi[i], l_mid[i]], axis=0), w_hi[i]) for i in range(len(seqs))]
    p_mid = [_dot(l_hi[i], w_mid[i]) for i in range(len(seqs))]
    for i, (d, b) in enumerate(seqs):
        _, h_ref, _, y_ref, o_ref = dirs[d]
        tot = p_hi[i][0:2 * C] + (p_hi[i][2 * C:4 * C] + p_mid[i])
        o_ref[b] = tot[0:C] + y_ref[b, 0]
        st_ref[d, b] = tot[C:2 * C] + h_ref[b, 0, 0]


def _rwkv_scan_call(g, hh, rh, yh, lc):
    nb, _, nch, C, _ = g.shape
    t = nch * C
    ncc = lc // C

    def bwd(j):
        return jnp.where(j < ncc, ncc - 1 - j, nch - 1 + ncc - j)

    def mat_spec(d):
        if d == 0:
            return pl.BlockSpec((nb, 1, 1, C, 256), lambda j: (0, 0, j, 0, 0))
        return pl.BlockSpec((nb, 1, 1, C, 256), lambda j: (0, 1, bwd(j), 0, 0))

    def seq_spec(d):
        if d == 0:
            return pl.BlockSpec((nb, 1, C, 256), lambda j: (0, 0, j, 0))
        return pl.BlockSpec((nb, 1, C, 256), lambda j: (0, 1, bwd(j), 0))

    kern = functools.partial(_rwkv_scan_kernel, nb=nb)
    out = jax.ShapeDtypeStruct((nb, t, 256), F32)
    specs = []
    for d in range(2):
        specs += [mat_spec(d), mat_spec(d), seq_spec(d), seq_spec(d)]
    of, ob = pl.pallas_call(
        kern,
        out_shape=[out, out],
        grid=(nch,),
        in_specs=specs,
        out_specs=[pl.BlockSpec((nb, C, 256), lambda j: (0, j, 0)),
                   pl.BlockSpec((nb, C, 256), lambda j: (0, bwd(j), 0))],
        scratch_shapes=[pltpu.VMEM((2, nb, C, 256), F32)],
        compiler_params=_cparams(("arbitrary",)),
        name="rwkv_scan",
    )(g, hh, rh, yh, g, hh, rh, yh)
    return of, ob


def _merge_kernel(x_ref, mod_ref, oa_ref, oc_ref, od_ref, yf_ref, yb_ref, bon_ref, sg_ref, mg_ref,
                  bw_ref, ow_ref, mb_ref, gnw_ref, gnb_ref, lng_ref, lnb_ref, o_ref,
                  *, lc, tm, nb, row_off, alpha):
    b = pl.program_id(0)
    i = pl.program_id(1)

    def pair_cat(ref):
        return jnp.concatenate([ref[0, 0], ref[0, 1]], axis=1)

    def per_head(fn, x):
        return jnp.concatenate([fn(x[:, 0:LANES]), fn(x[:, LANES:2 * LANES])], axis=1)

    y = yf_ref[0] + yb_ref[0]
    mu = per_head(_half_sums, y) * (1.0 / RWKV_N)
    yc = y - mu
    var = per_head(_half_sums, yc * yc) * (1.0 / RWKV_N)
    ob = yc * lax.rsqrt(var + RWKV_GN_EPS) * gnw_ref[...] + gnb_ref[...] + bon_ref[0]

    outs = (pair_cat(oa_ref), ob, pair_cat(oc_ref), pair_cat(od_ref))
    acc = None
    for k in range(N_BRANCH):
        sg = sg_ref[0, :, k * BRANCH_W:(k + 1) * BRANCH_W].astype(F32)
        u = outs[k] * (sg * jax.nn.sigmoid(sg))
        z = _dot(u.astype(BF16), bw_ref[k])
        gate = jax.nn.sigmoid(mg_ref[0, :, k * D_MODEL:(k + 1) * D_MODEL].astype(F32)
                              + mb_ref[:, k * D_MODEL:(k + 1) * D_MODEL])
        acc = gate * z if acc is None else acc + gate * z
    y2 = _dot(acc.astype(BF16), ow_ref[...])

    row = row_off + i * tm + _iota((tm, D_MODEL), 0)
    gl = mod_ref[pl.ds(b, 1), 2 * D_MODEL:3 * D_MODEL]
    gc = mod_ref[pl.ds(nb, 1), 2 * D_MODEL:3 * D_MODEL]
    h = alpha * x_ref[0] + jnp.where(row < lc, gc, gl) * y2
    m = jnp.mean(h, -1, keepdims=True)
    hc = h - m
    var = jnp.mean(hc * hc, -1, keepdims=True)
    o_ref[0] = hc * lax.rsqrt(var + 1e-5) * lng_ref[...] + lnb_ref[...]


def _merge_call(xx, mod, oa, oc, od, yf, yb, bon, p, bw, ow, mb, gnw, gnb, lng, lnb, *, lc, row_off, alpha):
    nb, t, _ = xx.shape
    nrows = t - row_off
    tm = _pick_tile(math.gcd(nrows, row_off) if row_off else nrows, 256, 16)
    ro = row_off // tm
    kern = functools.partial(_merge_kernel, lc=lc, tm=tm, nb=nb, row_off=row_off, alpha=alpha)

    def full(a):
        return pl.BlockSpec(a.shape, lambda b, i: (0,) * a.ndim)

    pair_spec = pl.BlockSpec((1, 2, tm, LANES), lambda b, i: (b, 0, i, 0))
    return pl.pallas_call(
        kern,
        out_shape=jax.ShapeDtypeStruct((nb, nrows, D_MODEL), F32),
        grid=(nb, nrows // tm),
        in_specs=[
            pl.BlockSpec((1, tm, D_MODEL), lambda b, i: (b, i + ro, 0)),
            full(mod), pair_spec, pair_spec, pair_spec,
            pl.BlockSpec((1, tm, 256), lambda b, i: (b, i + ro, 0)),
            pl.BlockSpec((1, tm, 256), lambda b, i: (b, i + ro, 0)),
            pl.BlockSpec((1, tm, 256), lambda b, i: (b, i + ro, 0)),
            pl.BlockSpec((1, tm, D_MODEL), lambda b, i: (b, i + ro, P_SILU // D_MODEL)),
            pl.BlockSpec((1, tm, N_BRANCH * D_MODEL), lambda b, i: (b, i + ro, P_MERGE // (N_BRANCH * D_MODEL))),
            full(bw), full(ow), full(mb), full(gnw), full(gnb), full(lng), full(lnb),
        ],
        out_specs=pl.BlockSpec((1, tm, D_MODEL), lambda b, i: (b, i, 0)),
        compiler_params=_cparams(("arbitrary", "arbitrary")),
        name="merge",
    )(xx, mod, oa, oc, od, yf, yb, bon, p, p, bw, ow, mb, gnw, gnb, lng, lnb)


def _pack_in_w(w):
    z = lambda n: jnp.zeros((D_MODEL, n), w.dtype)
    a0 = 0
    b0 = 672
    c0 = b0 + 1280
    d0 = c0 + 768
    m0 = d0 + 1024
    cols = [
        w[:, a0:a0 + 256], w[:, a0 + 256:a0 + 384],
        z(64), w[:, a0 + 384:a0 + 416], z(32),
        w[:, c0:c0 + 256], w[:, c0 + 256:c0 + 384], w[:, c0 + 384:c0 + 512],
        w[:, d0:d0 + 256], w[:, d0 + 256:d0 + 512], w[:, d0 + 512:d0 + 768],
        z(256),
        w[:, b0:b0 + 1024],
        w[:, a0 + 416:a0 + 672], w[:, b0 + 1024:b0 + 1280], w[:, c0 + 512:c0 + 768], w[:, d0 + 768:d0 + 1024],
        w[:, m0:m0 + 4096],
    ]
    out = jnp.concatenate(cols, axis=1)
    assert out.shape[1] == P_WIDTH
    return out.astype(BF16)


def _pack_mla(w_uq, w_ukv):
    zq = jnp.zeros((MLA_Q_LORA, LANES - MLA_NOPE - MLA_ROPE), w_uq.dtype)
    zk = jnp.zeros((MLA_KV_LORA, LANES - MLA_NOPE), w_ukv.dtype)
    zv = jnp.zeros((MLA_KV_LORA, MLA_V), w_ukv.dtype)
    qc, kc, vc = [], [], []
    for h in range(MLA_HEADS):
        qc += [w_uq[:, h * 96:(h + 1) * 96], zq]
        kc += [w_ukv[:, h * 128:h * 128 + 64], zk]
        vh = w_ukv[:, h * 128 + 64:(h + 1) * 128]
        vc += [vh, zv] if h % 2 == 0 else [zv, vh]
    cat = lambda xs: jnp.concatenate(xs, axis=1).astype(BF16)
    return cat(qc), cat(kc), cat(vc)


def _rope_tables(row, col, rot_dim, lc, pattern):
    f32 = np.float32
    quarter = rot_dim // 4
    inv_freq = (f32(ROPE_BASE) ** (-np.arange(quarter, dtype=f32) / f32(quarter))).astype(f32)
    ang = np.concatenate([row[:, None] * inv_freq, col[:, None] * inv_freq], axis=-1).astype(f32)
    cos, sin = np.cos(ang).astype(f32), np.sin(ang).astype(f32)
    n = ang.shape[0]
    ones = lambda w: np.ones((n, w), f32)
    zeros = lambda w: np.zeros((n, w), f32)
    if pattern == "mla":
        c = np.concatenate([ones(64), cos, cos, ones(32)], axis=1)
        s = np.concatenate([zeros(64), -sin, sin, zeros(32)], axis=1)
    else:
        reps = LANES // rot_dim
        c = np.concatenate([cos, cos] * reps, axis=1)
        s = np.concatenate([-sin, sin] * reps, axis=1)
    c = np.concatenate([np.ones((lc, LANES), f32), c], axis=0)
    s = np.concatenate([np.zeros((lc, LANES), f32), s], axis=0)
    return jnp.asarray(c), jnp.asarray(s)


def kernel(x, c, ctx, c_ctx, ada_w, ada_b, in_w, mla_q_norm, mla_w_uq, mla_kv_norm, mla_w_ukv, rwkv_mu, rwkv_w0, rwkv_w_up, rwkv_a0, rwkv_a_up, rwkv_k_k, rwkv_k_a, rwkv_r_k, rwkv_gn_w, rwkv_gn_b, gqa_q_norm, gqa_k_norm, diff_lambda, diff_subln, merge_b, branch_w, out_w, ln_g, ln_b):
    nb, ll, _ = x.shape
    lc = ctx.shape[1]
    depth = ada_w.shape[0]
    t = lc + ll
    alpha = (2 * depth) ** 0.25

    rows = ll // GRID_W
    row = np.repeat(np.arange(rows), GRID_W).astype(np.float32)
    col = np.tile(np.arange(GRID_W), rows).astype(np.float32)
    tabs = (_rope_tables(row, col, MLA_ROPE, lc, "mla") + _rope_tables(row, col, HEAD_DIM, lc, "tile")
            + _rope_tables(row, col, DIFF_D, lc, "tile"))

    crows = -(-(nb + 1) // 8) * 8
    cvec = jnp.concatenate([c, c_ctx[None, :], jnp.zeros((crows - nb - 1, D_MODEL), F32)], axis=0)
    mods = _ada_call(cvec, ada_w, ada_b)

    xx = jnp.concatenate([ctx, x], axis=1)
    tile2 = lambda a: jnp.concatenate([a, a])[None, :]
    for l in range(depth):
        last = l == depth - 1
        lam_init = 0.8 - 0.6 * math.exp(-0.3 * l)
        mod = mods[l]
        p = _inproj_call(xx, mod, _pack_in_w(in_w[l]), lc)

        wuq, wk, wv = _pack_mla(mla_w_uq[l], mla_w_ukv[l])
        qa, ka, va, qc, kc, vc, qd, kd, vd = _prep_call(
            p, tabs, (mla_q_norm[l][None, :], wuq, mla_kv_norm[l][None, :], wk, wv,
                      tile2(gqa_q_norm[l]), tile2(gqa_k_norm[l])))
        lam_p = jnp.zeros((8, LANES), F32).at[0:4, 0:DIFF_D].set(diff_lambda[l])
        sub = tile2(diff_subln[l])
        att = lambda q, k, v, **kw: _attn_call(q, k, v, lk=t, q_off=lc, nq_rows=ll, **kw)
        oa = att(qa, ka, va, n_maps=1, shared_k=False)
        oc = att(qc, kc, vc, n_maps=1, shared_k=True)
        od = att(qd, kd, vd, n_maps=2, shared_k=False, extra=(lam_p, sub), lam_init=lam_init)
        if not last:
            catt = lambda q, k, v, **kw: _attn_call(q, k, v, lk=lc, q_off=0, nq_rows=lc, **kw)
            merge_rows = lambda lat_o, ctx_o: jnp.concatenate([ctx_o, lat_o], axis=2)
            oa = merge_rows(oa, catt(qa, ka, va, n_maps=1, shared_k=False))
            oc = merge_rows(oc, catt(qc, kc, vc, n_maps=1, shared_k=True))
            od = merge_rows(od, catt(qd, kd, vd, n_maps=2, shared_k=False, extra=(lam_p, sub), lam_init=lam_init))

        wl = jnp.zeros((256, 1024), F32)
        for d in range(2):
            wl = wl.at[d * 64:(d + 1) * 64, d * 256:(d + 1) * 256].set(rwkv_w_up[l, d])
            wl = wl.at[128 + d * 64:128 + (d + 1) * 64, 512 + d * 256:512 + (d + 1) * 256].set(rwkv_a_up[l, d])
        bias = jnp.concatenate([rwkv_w0[l, 0], rwkv_w0[l, 1], rwkv_a0[l, 0], rwkv_a0[l, 1]])[None, :]
        shared, dirs, bon = _rwkv_prep_call(p, rwkv_mu[l], wl.astype(BF16), bias, rwkv_k_k[l][None, :],
                                            rwkv_k_a[l][None, :], rwkv_r_k[l][None, :], lc)
        g, hh, rh, yh = _rwkv_local_call(shared, dirs)
        yf, yb = _rwkv_scan_call(g, hh, rh, yh, lc)

        xx = _merge_call(xx, mod, oa, oc, od, yf, yb, bon, p,
                         branch_w[l].astype(BF16), out_w[l].astype(BF16), merge_b[l][None, :],
                         rwkv_gn_w[l][None, :], rwkv_gn_b[l][None, :], ln_g[l][None, :], ln_b[l][None, :],
                         lc=lc, row_off=lc if last else 0, alpha=alpha)
    return xx
```

```python
import functools
import math

import jax
import jax.numpy as jnp
import numpy as np
from jax import lax
from jax.experimental import pallas as pl
from jax.experimental.pallas import tpu as pltpu

F32 = jnp.float32
BF16 = jnp.bfloat16

D_MODEL = 1024
GRID_W = 64
ROPE_BASE = 10000.0
HEAD_DIM = 64
N_BRANCH = 4
BRANCH_W = 256
MLA_HEADS = 4
MLA_Q_LORA = 256
MLA_KV_LORA = 128
MLA_NOPE = 64
MLA_ROPE = 32
MLA_V = 64
MLA_SCALE = (MLA_NOPE + MLA_ROPE) ** -0.5
RWKV_N = 64
RWKV_GN_EPS = 64e-5
RWKV_DECAY_SCALE = math.exp(-0.5)
GQA_SCALE = HEAD_DIM ** -0.5
DIFF_D = 32
DIFF_SCALE = DIFF_D ** -0.5
LOG2E = 1.0 / math.log(2.0)

LANES = 128
VMEM_LIMIT = 56 * 1024 * 1024

P_AQ, P_AKV, P_AKR = 0, 256, 384
P_CQ, P_CK, P_CV = 512, 768, 896
P_DQ, P_DK, P_DV = 1024, 1280, 1536
P_RWKV = 2048
P_SILU = 3072
P_MERGE = 4096
P_WIDTH = 8192
ATTN_W = 2048

RWKV_CHUNK = 64
ATTN_TQ = 256
ATTN_MAPS = 16


def _cparams(sem):
    return pltpu.CompilerParams(dimension_semantics=sem, vmem_limit_bytes=VMEM_LIMIT)


def _split3(x):
    h = x.astype(BF16)
    r = x - h.astype(F32)
    m = r.astype(BF16)
    l = (r - m.astype(F32)).astype(BF16)
    return h, m, l


def _dot(a, b):
    return jnp.dot(a, b, preferred_element_type=F32)


def _dot_hi(a, b):
    ah, am, _ = _split3(a)
    bh, bm, _ = _split3(b)
    return _dot(ah, bh) + (_dot(ah, bm) + _dot(am, bh))


def _dot_exact_lhs(a_bf16, b):
    bh, bm, bl = _split3(b)
    return _dot(a_bf16, bh) + (_dot(a_bf16, bm) + _dot(a_bf16, bl))


def _iota(shape, dim):
    return lax.broadcasted_iota(jnp.int32, shape, dim)


def _ada_kernel(c_ref, w_ref, b_ref, o_ref):
    c = c_ref[...]
    s = c * jax.nn.sigmoid(c)
    o_ref[0] = _dot_hi(s, w_ref[0]) + b_ref[0]


def _ada_call(cvec, ada_w, ada_b):
    depth = ada_w.shape[0]
    rows = cvec.shape[0]
    nblk = 3
    return pl.pallas_call(
        _ada_kernel,
        out_shape=jax.ShapeDtypeStruct((depth, rows, 3 * D_MODEL), F32),
        grid=(depth, nblk),
        in_specs=[
            pl.BlockSpec((rows, D_MODEL), lambda l, j: (0, 0)),
            pl.BlockSpec((1, D_MODEL, D_MODEL), lambda l, j: (l, 0, j)),
            pl.BlockSpec((1, 1, D_MODEL), lambda l, j: (l, 0, j)),
        ],
        out_specs=pl.BlockSpec((1, rows, D_MODEL), lambda l, j: (l, 0, j)),
        compiler_params=_cparams(("arbitrary", "arbitrary")),
        name="ada",
    )(cvec, ada_w, ada_b.reshape(depth, 1, 3 * D_MODEL))


def _inproj_kernel(x_ref, mod_ref, w_ref, o_ref, xm_ref, *, lc, tm, nb):
    b = pl.program_id(0)
    i = pl.program_id(1)
    j = pl.program_id(2)

    @pl.when(j == 0)
    def _():
        x = x_ref[0]
        mu = jnp.mean(x, -1, keepdims=True)
        xc = x - mu
        var = jnp.mean(xc * xc, -1, keepdims=True)
        xn = xc * lax.rsqrt(var + 1e-6)
        row = i * tm + _iota((tm, D_MODEL), 0)
        is_ctx = row < lc
        ml = mod_ref[pl.ds(b, 1), :]
        mc = mod_ref[pl.ds(nb, 1), :]
        shift = jnp.where(is_ctx, mc[:, 0:D_MODEL], ml[:, 0:D_MODEL])
        scale = jnp.where(is_ctx, mc[:, D_MODEL:2 * D_MODEL], ml[:, D_MODEL:2 * D_MODEL])
        xm_ref[...] = (xn * (1.0 + scale) + shift).astype(BF16)

    o_ref[0] = _dot(xm_ref[...], w_ref[...]).astype(BF16)


def _inproj_call(xx, mod, w_packed, lc):
    nb, t, _ = xx.shape
    tm = _pick_tile(t, 1088, 16)
    tn = 1024
    kern = functools.partial(_inproj_kernel, lc=lc, tm=tm, nb=nb)
    return pl.pallas_call(
        kern,
        out_shape=jax.ShapeDtypeStruct((nb, t, P_WIDTH), BF16),
        grid=(nb, t // tm, P_WIDTH // tn),
        in_specs=[
            pl.BlockSpec((1, tm, D_MODEL), lambda b, i, j: (b, i, 0)),
            pl.BlockSpec(mod.shape, lambda b, i, j: (0, 0)),
            pl.BlockSpec((D_MODEL, tn), lambda b, i, j: (0, j)),
        ],
        out_specs=pl.BlockSpec((1, tm, tn), lambda b, i, j: (b, i, j)),
        scratch_shapes=[pltpu.VMEM((tm, D_MODEL), BF16)],
        compiler_params=_cparams(("arbitrary", "arbitrary", "arbitrary")),
        name="inproj",
    )(xx, mod, w_packed)


def _pick_tile(n, cap, mult):
    best = None
    for d in range(mult, min(n, cap) + 1, mult):
        if n % d == 0:
            best = d
    assert best is not None, (n, cap, mult)
    return best


def _swap_halves(x, half):
    n = x.shape[-1]
    first = (_iota(x.shape, 1) & (2 * half - 1)) < half
    up = pltpu.roll(x, n - half, 1)
    dn = pltpu.roll(x, half, 1)
    return jnp.where(first, up, dn)


def _half_sums(x):
    lo = _iota(x.shape, 1) < HEAD_DIM
    s_lo = jnp.sum(jnp.where(lo, x, 0.0), -1, keepdims=True)
    s_hi = jnp.sum(jnp.where(lo, 0.0, x), -1, keepdims=True)
    return jnp.where(lo, s_lo, s_hi)


def _prep_kernel(p_ref, cosa_ref, sina_ref, cosc_ref, sinc_ref, cosd_ref, sind_ref,
                 qn_ref, wuq_ref, kvn_ref, wk_ref, wv_ref, gq_ref, gk_ref,
                 qa_ref, ka_ref, va_ref, qc_ref, kc_ref, vc_ref, qd_ref, kd_ref, vd_ref):
    def seg(off, w):
        return p_ref[0, :, off:off + w].astype(F32)

    lane = _iota((p_ref.shape[1], LANES), 1)
    lo = lane < HEAD_DIM

    cosa, sina = cosa_ref[...], sina_ref[...]
    ql = seg(P_AQ, MLA_Q_LORA)
    ql = ql * lax.rsqrt(jnp.mean(ql * ql, -1, keepdims=True) + 1e-6) * qn_ref[...]
    q = _dot(ql.astype(BF16), wuq_ref[...])
    kvl = seg(P_AKV, MLA_KV_LORA)
    kvl = (kvl * lax.rsqrt(jnp.mean(kvl * kvl, -1, keepdims=True) + 1e-6) * kvn_ref[...]).astype(BF16)
    kn = _dot(kvl, wk_ref[...])
    vv = _dot(kvl, wv_ref[...])
    kr = seg(P_AKR, LANES)
    kr = kr * cosa + _swap_halves(kr, MLA_ROPE // 2) * sina
    for h in range(MLA_HEADS):
        qh = q[:, h * LANES:(h + 1) * LANES]
        qh = qh * cosa + _swap_halves(qh, MLA_ROPE // 2) * sina
        qa_ref[0, h] = (qh * (MLA_SCALE * LOG2E)).astype(BF16)
        ka_ref[0, h] = (kn[:, h * LANES:(h + 1) * LANES] + kr).astype(BF16)
        va_ref[0, h] = vv[:, h * LANES:(h + 1) * LANES].T.astype(BF16)

    cosc, sinc = cosc_ref[...], sinc_ref[...]

    def norm_rope(x, g):
        x = x * lax.rsqrt(_half_sums(x * x) * (1.0 / HEAD_DIM) + 1e-6) * g
        return x * cosc + _swap_halves(x, HEAD_DIM // 2) * sinc

    def split_heads(blk):
        return jnp.where(lo, blk, 0.0), jnp.where(lo, pltpu.roll(blk, HEAD_DIM, 1), 0.0)

    for pr in range(2):
        qb = norm_rope(seg(P_CQ + pr * LANES, LANES), gq_ref[...]) * (GQA_SCALE * LOG2E)
        q0, q1 = split_heads(qb)
        qc_ref[0, 2 * pr] = q0.astype(BF16)
        qc_ref[0, 2 * pr + 1] = q1.astype(BF16)
    k0, k1 = split_heads(norm_rope(seg(P_CK, LANES), gk_ref[...]))
    kc_ref[0, 0] = k0.astype(BF16)
    kc_ref[0, 1] = k1.astype(BF16)
    vb = seg(P_CV, LANES)
    v_g0 = jnp.where(lo, vb, 0.0)
    v_g1 = jnp.where(lo, 0.0, vb)
    vc_ref[0, 0] = v_g0.T.astype(BF16)
    vc_ref[0, 1] = pltpu.roll(v_g0, HEAD_DIM, 1).T.astype(BF16)
    vc_ref[0, 2] = pltpu.roll(v_g1, HEAD_DIM, 1).T.astype(BF16)
    vc_ref[0, 3] = v_g1.T.astype(BF16)

    cosd, sind = cosd_ref[...], sind_ref[...]
    piece = lane < DIFF_D
    for pr in range(2):
        qb = seg(P_DQ + pr * LANES, LANES)
        qb = (qb * cosd + _swap_halves(qb, DIFF_D // 2) * sind) * (DIFF_SCALE * LOG2E)
        kb = seg(P_DK + pr * LANES, LANES)
        kb = kb * cosd + _swap_halves(kb, DIFF_D // 2) * sind
        vb = seg(P_DV + pr * LANES, LANES)
        for hh in range(2):
            for m in range(2):
                off = hh * HEAD_DIM + m * DIFF_D
                idx = (2 * pr + hh) * 2 + m
                qs = qb if off == 0 else pltpu.roll(qb, LANES - off, 1)
                ks = kb if off == 0 else pltpu.roll(kb, LANES - off, 1)
                qd_ref[0, idx] = jnp.where(piece, qs, 0.0).astype(BF16)
                kd_ref[0, idx] = jnp.where(piece, ks, 0.0).astype(BF16)
        vd_ref[0, 2 * pr] = jnp.where(lo, vb, 0.0).T.astype(BF16)
        vd_ref[0, 2 * pr + 1] = jnp.where(lo, 0.0, vb).T.astype(BF16)


def _prep_call(p, tabs, wts):
    nb, t, _ = p.shape
    tm = _pick_tile(t, 256, 16)
    row_spec = pl.BlockSpec((tm, LANES), lambda b, i: (i, 0))

    def full(a):
        return pl.BlockSpec(a.shape, lambda b, i: (0,) * a.ndim)

    def head_out(nh):
        return (jax.ShapeDtypeStruct((nb, nh, t, LANES), BF16),
                pl.BlockSpec((1, nh, tm, LANES), lambda b, i: (b, 0, i, 0)))

    def head_out_t(nh):
        return (jax.ShapeDtypeStruct((nb, nh, LANES, t), BF16),
                pl.BlockSpec((1, nh, LANES, tm), lambda b, i: (b, 0, 0, i)))

    outs = [head_out(4), head_out(4), head_out_t(4), head_out(4), head_out(2), head_out_t(4),
            head_out(8), head_out(8), head_out_t(4)]
    return pl.pallas_call(
        _prep_kernel,
        out_shape=[o[0] for o in outs],
        grid=(nb, t // tm),
        in_specs=[pl.BlockSpec((1, tm, ATTN_W), lambda b, i: (b, i, 0))]
        + [row_spec] * 6 + [full(w) for w in wts],
        out_specs=[o[1] for o in outs],
        compiler_params=_cparams(("arbitrary", "arbitrary")),
        name="attn_prep",
    )(p, *tabs, *wts)


def _attn_kernel(*refs, hp, n_maps, qsub, tq, shared_k, lk, tk, lam_init):
    if n_maps == 2:
        q_ref, k_ref, vt_ref, lam_ref, sub_ref, o_ref, s_scr, p_scr = refs
    else:
        q_ref, k_ref, vt_ref, o_ref, s_scr, p_scr = refs
    nchunks = lk // tk
    maps = [(qs, hh * n_maps + mm, hh // 2 if shared_k else hh * n_maps + mm, hh)
            for qs in range(qsub) for hh in range(hp) for mm in range(n_maps)]
    per_sub = hp * n_maps

    def scores(j):
        qs, qi, ki, _ = maps[j]
        s_scr[j % 2] = lax.dot_general(k_ref[0, ki], q_ref[0, qi, qs * tq:(qs + 1) * tq, :],
                                       (((1,), (1,)), ((), ())), preferred_element_type=F32)

    def softmax_rows(j):
        buf = j % 2
        mx = None
        for c in range(nchunks):
            sc = s_scr[buf, c * tk:(c + 1) * tk, :]
            mx = sc if mx is None else jnp.maximum(mx, sc)
        m = jnp.max(mx, 0, keepdims=True)
        ls = None
        for c in range(nchunks):
            p = jnp.exp2(s_scr[buf, c * tk:(c + 1) * tk, :] - m)
            p_scr[buf, c * tk:(c + 1) * tk, :] = p.astype(BF16)
            ls = p if ls is None else ls + p
        return jnp.sum(ls, 0, keepdims=True)

    def weighted_values(j, l):
        ot = _dot(vt_ref[0, maps[j][3]], p_scr[j % 2])
        return (ot / l).T

    outs = []
    scores(0)
    for j in range(len(maps)):
        if j + 1 < len(maps):
            scores(j + 1)
        outs.append(weighted_values(j, softmax_rows(j)))

    for qs in range(qsub):
        for pr in range(hp // 2):
            total = None
            for hh in (2 * pr, 2 * pr + 1):
                o = outs[qs * per_sub + hh * n_maps]
                if n_maps == 2:
                    lp = lam_ref[...]
                    lam = (jnp.exp(jnp.sum(lp[0:1] * lp[1:2], keepdims=True))
                           - jnp.exp(jnp.sum(lp[2:3] * lp[3:4], keepdims=True)) + lam_init)
                    o = o - lam * outs[qs * per_sub + hh * n_maps + 1]
                    ms = jnp.sum(o * o, -1, keepdims=True) * (1.0 / HEAD_DIM)
                    o = o * lax.rsqrt(ms + 1e-5) * sub_ref[...] * (1.0 - lam_init)
                total = o if total is None else total + o
            o_ref[0, pr, qs * tq:(qs + 1) * tq, :] = total


def _attn_call(q, k, v, *, n_maps, shared_k, lk, q_off, nq_rows, extra=(), lam_init=0.0):
    nb = q.shape[0]
    hp = 4
    tq = _pick_tile(nq_rows, ATTN_TQ, 16)
    qsub = max(1, ATTN_MAPS // (hp * n_maps))
    while nq_rows % (qsub * tq):
        qsub //= 2
    tb = qsub * tq
    tk = 256 if lk % 256 == 0 else _pick_tile(lk, 256, LANES)
    kh = hp // 2 if shared_k else hp * n_maps
    kern = functools.partial(_attn_kernel, hp=hp, n_maps=n_maps, qsub=qsub, tq=tq, shared_k=shared_k,
                             lk=lk, tk=tk, lam_init=lam_init)
    in_specs = [
        pl.BlockSpec((pl.Element(1), pl.Element(hp * n_maps), pl.Element(tb), pl.Element(LANES)),
                     lambda b, i: (b, 0, pl.multiple_of(i * tb + q_off, 16), 0)),
        pl.BlockSpec((1, kh, lk, LANES), lambda b, i: (b, 0, 0, 0)),
        pl.BlockSpec((1, hp, LANES, lk), lambda b, i: (b, 0, 0, 0)),
    ] + [pl.BlockSpec(e.shape, lambda b, i: (0, 0)) for e in extra]
    return pl.pallas_call(
        kern,
        out_shape=jax.ShapeDtypeStruct((nb, 2, nq_rows, LANES), F32),
        grid=(nb, nq_rows // tb),
        in_specs=in_specs,
        out_specs=pl.BlockSpec((1, hp // 2, tb, LANES), lambda b, i: (b, 0, i, 0)),
        scratch_shapes=[pltpu.VMEM((2, lk, tq), F32), pltpu.VMEM((2, lk, tq), BF16)],
        compiler_params=_cparams(("arbitrary", "arbitrary")),
        name="attn",
    )(q, k, v, *extra)


def _rwkv_prep_kernel(cur_ref, prv_ref, nxt_ref, mu_ref, wl_ref, bias_ref, kk_ref, ka_ref, rk_ref,
                      sh_ref, dr_ref, bon_ref, *, lc, t_total, tm, halo):
    i = pl.program_id(1)
    cur = cur_ref[0]
    r_i = _iota((tm, tm), 0)
    c_i = _iota((tm, tm), 1)
    s_dn = jnp.where(c_i == r_i - 1, 1.0, 0.0).astype(BF16)
    s_up = jnp.where(c_i == r_i + 1, 1.0, 0.0).astype(BF16)
    row = _iota((tm, D_MODEL), 0)
    tg = i * tm + row
    prv = jnp.where(row == 0, prv_ref[0].astype(F32)[halo - 1:halo, :], _dot(s_dn, cur))
    nxt = jnp.where(row == tm - 1, nxt_ref[0].astype(F32)[0:1, :], _dot(s_up, cur))
    prv = jnp.where((tg == 0) | (tg == lc), 0.0, prv)
    nxt = jnp.where((tg == lc - 1) | (tg == t_total - 1), 0.0, nxt)
    x = cur.astype(F32)
    sh = x + mu_ref[0:1, :] * (prv - x) + mu_ref[1:2, :] * (nxt - x)

    r = sh[:, 0:256]
    k = sh[:, 256:512]
    v = sh[:, 512:768]
    lora = sh[:, 768:1024]
    lora = jnp.where(_iota(lora.shape, 1) < 2 * RWKV_N, jnp.tanh(lora), lora)
    z = _dot(lora.astype(BF16), wl_ref[...]) + bias_ref[...]

    def per_head(fn, x):
        return jnp.concatenate([fn(x[:, 0:LANES]), fn(x[:, LANES:2 * LANES])], axis=1)

    kq = k * kk_ref[...]
    kk = kq * lax.rsqrt(per_head(_half_sums, kq * kq) + 1e-12)
    sh_ref[0, :, 0:256] = r
    sh_ref[0, :, 256:512] = v
    sh_ref[0, :, 512:768] = kk
    hs = None
    for d in range(2):
        logw = -RWKV_DECAY_SCALE * jax.nn.sigmoid(z[:, d * 256:(d + 1) * 256])
        a = jax.nn.sigmoid(z[:, 512 + d * 256:512 + (d + 1) * 256])
        kd = k * (1.0 + (a - 1.0) * ka_ref[...])
        dr_ref[0, d, :, 0:256] = logw
        dr_ref[0, d, :, 256:512] = kd
        dr_ref[0, d, :, 512:768] = a * kk
        s = per_head(_half_sums, r * kd * rk_ref[...])
        hs = s if hs is None else hs + s
    bon_ref[0] = hs * v


def _rwkv_prep_call(p, mu, wl, bias, k_k, k_a, r_k, lc):
    nb, t, _ = p.shape
    tm = _pick_tile(t, 256, 16)
    halo = 16
    cb = P_RWKV // D_MODEL
    hb = tm // halo
    last = t // halo - 1
    kern = functools.partial(_rwkv_prep_kernel, lc=lc, t_total=t, tm=tm, halo=halo)

    def full(a):
        return pl.BlockSpec(a.shape, lambda b, i: (0,) * a.ndim)

    return pl.pallas_call(
        kern,
        out_shape=[jax.ShapeDtypeStruct((nb, t, 768), F32),
                   jax.ShapeDtypeStruct((nb, 2, t, 768), F32),
                   jax.ShapeDtypeStruct((nb, t, 256), F32)],
        grid=(nb, t // tm),
        in_specs=[
            pl.BlockSpec((1, tm, D_MODEL), lambda b, i: (b, i, cb)),
            pl.BlockSpec((1, halo, D_MODEL), lambda b, i: (b, jnp.maximum(i * hb - 1, 0), cb)),
            pl.BlockSpec((1, halo, D_MODEL), lambda b, i: (b, jnp.minimum((i + 1) * hb, last), cb)),
            full(mu), full(wl), full(bias), full(k_k), full(k_a), full(r_k),
        ],
        out_specs=[pl.BlockSpec((1, tm, 768), lambda b, i: (b, i, 0)),
                   pl.BlockSpec((1, 2, tm, 768), lambda b, i: (b, 0, i, 0)),
                   pl.BlockSpec((1, tm, 256), lambda b, i: (b, i, 0))],
        compiler_params=_cparams(("arbitrary", "arbitrary")),
        name="rwkv_prep",
    )(p, p, p, mu, wl, bias, k_k, k_a, r_k)


def _expand(x, bd):
    return jnp.where(bd, jnp.concatenate([x] * 4, axis=0), 0.0)


def _contract(x):
    c = x.shape[0] // 4
    return (x[0:c] + x[c:2 * c]) + (x[2 * c:3 * c] + x[3 * c:4 * c])


def _rwkv_masks():
    C = RWKV_CHUNK
    n = 4 * C
    rr, cc = np.indices((n, n))
    bd = ((rr // C) == (cc // C)).astype(np.float32)
    tr, tc = np.indices((C, n))
    tc = tc % C
    eye = (tr == tc).astype(np.float32)
    dirm = np.stack([np.stack([tc < tr, tc <= tr]), np.stack([tc > tr, tc >= tr])]).astype(np.float32)
    lev = [(tr >> 1) == (tc >> 1)]
    s = 1
    while (1 << s) < C:
        lev.append(((tr >> s) != (tc >> s)) & ((tr >> (s + 1)) == (tc >> (s + 1))))
        s += 1
    r1, c1 = np.indices((C, C))
    incl1 = np.stack([c1 <= r1, c1 >= r1]).astype(np.float32)
    return (jnp.asarray(bd), jnp.asarray(eye), jnp.asarray(dirm),
            jnp.asarray(np.stack(lev).astype(np.float32), BF16), jnp.asarray(incl1, BF16))


def _rwkv_local_kernel(sh_ref, dr_ref, bd_ref, eye_ref, dirm_ref, lev_ref, incl1_ref,
                       g_ref, hh_ref, rh_ref, yh_ref, *, nck):
    C = RWKV_CHUNK
    n = 4 * C
    bd = bd_ref[...]
    eye = eye_ref[...]
    bd_b = bd.astype(BF16)
    nt = (((1,), (1,)), ((), ()))
    tn = (((0,), (0,)), ((), ()))

    def expand_b(x):
        return jnp.concatenate([x.astype(BF16)] * 4, axis=0) * bd_b

    items = [(d, ck) for d in range(2) for ck in range(nck)]
    cks = range(len(items))
    rows = [slice(ck * C, (ck + 1) * C) for _, ck in items]
    strict = [dirm_ref[d, 0] for d, _ in items]
    incl = [dirm_ref[d, 1] for d, _ in items]
    r = [sh_ref[0, rw, 0:256] for rw in rows]
    v = [sh_ref[0, rw, 256:512] for rw in rows]
    kk = [sh_ref[0, rw, 512:768] for rw in rows]
    lw = [dr_ref[0, d, rows[i], 0:256] for i, (d, _) in enumerate(items)]
    kd = [dr_ref[0, d, rows[i], 256:512] for i, (d, _) in enumerate(items)]
    b = [dr_ref[0, d, rows[i], 512:768] for i, (d, _) in enumerate(items)]

    lp = [_dot_exact_lhs(incl1_ref[items[i][0]], lw[i]) for i in cks]
    ltot = [jnp.sum(lw[i], 0, keepdims=True) for i in cks]
    e_neg = [jnp.exp(-lp[i]) for i in cks]
    kap = [kk[i] * jnp.exp(lp[i] - lw[i]) for i in cks]
    rt = [r[i] * jnp.exp(lp[i]) for i in cks]
    bt = [b[i] * e_neg[i] for i in cks]
    kt = [kd[i] * e_neg[i] for i in cks]
    e_rem = [jnp.exp(ltot[i] - lp[i]) for i in cks]
    bh = [b[i] * e_rem[i] for i in cks]
    kh = [kd[i] * e_rem[i] for i in cks]

    lhs = [jnp.concatenate([kap[i], rt[i]], axis=0).astype(BF16) for i in cks]
    gram_b = [lax.dot_general(lhs[i], expand_b(bt[i]), nt, preferred_element_type=F32) for i in cks]
    gram_k = [lax.dot_general(lhs[i], expand_b(kt[i]), nt, preferred_element_type=F32) for i in cks]
    mb_b = [(gram_b[i][0:C] * strict[i]).astype(BF16) for i in cks]
    mrb = [gram_b[i][C:2 * C] * incl[i] for i in cks]
    mk = [gram_k[i][0:C] * strict[i] for i in cks]
    mrk = [gram_k[i][C:2 * C] * incl[i] for i in cks]

    inv = [eye - (mb_b[i] * lev_ref[0]).astype(F32) for i in cks]
    for lv in range(1, lev_ref.shape[0]):
        inv_b = [inv[i].astype(BF16) for i in cks]
        step = [_dot(inv_b[i], expand_b(mb_b[i] * lev_ref[lv])) for i in cks]
        inv = [inv[i] - _dot(step[i].astype(BF16), expand_b(inv_b[i])) for i in cks]

    mv = [_dot(jnp.concatenate([mk[i], mrk[i]], axis=0).astype(BF16), expand_b(v[i])) for i in cks]
    wu_b = [_dot(inv[i].astype(BF16),
                 jnp.concatenate([expand_b(kap[i]), expand_b(mv[i][0:C])], axis=1)).astype(BF16)
            for i in cks]
    mw = [_dot(mrb[i].astype(BF16),
               jnp.concatenate([expand_b(wu_b[i][:, 0:n]), expand_b(wu_b[i][:, n:2 * n])], axis=1))
          for i in cks]
    bw = [lax.dot_general(bh[i].astype(BF16), wu_b[i], tn, preferred_element_type=F32) for i in cks]
    kv = [lax.dot_general(kh[i].astype(BF16), v[i].astype(BF16), tn, preferred_element_type=F32) for i in cks]
    for i, (d, ck) in enumerate(items):
        rh_ref[0, d, rows[i], :] = rt[i] - mw[i][:, 0:n]
        yh_ref[0, d, rows[i], :] = mv[i][C:2 * C] - mw[i][:, n:2 * n]
        g_ref[0, d, ck] = eye * jnp.exp(ltot[i]) - _contract(bw[i][:, 0:n] * bd)
        hh_ref[0, d, ck] = _contract((kv[i] - bw[i][:, n:2 * n]) * bd)


def _rwkv_local_call(shared, dirs):
    nb, t, _ = shared.shape
    C = RWKV_CHUNK
    nch = t // C
    nck = 4 if nch % 4 == 0 else (2 if nch % 2 == 0 else 1)
    tm = nck * C
    n = 4 * C
    bd, eye, dirm, lev, incl1 = _rwkv_masks()
    kern = functools.partial(_rwkv_local_kernel, nck=nck)
    mat = jax.ShapeDtypeStruct((nb, 2, nch, C, 256), F32)
    seq = jax.ShapeDtypeStruct((nb, 2, t, 256), F32)
    mat_spec = pl.BlockSpec((1, 2, nck, C, 256), lambda b, i: (b, 0, i, 0, 0))
    seq_spec = pl.BlockSpec((1, 2, tm, 256), lambda b, i: (b, 0, i, 0))
    return pl.pallas_call(
        kern,
        out_shape=[mat, mat, seq, seq],
        grid=(nb, nch // nck),
        in_specs=[pl.BlockSpec((1, tm, 768), lambda b, i: (b, i, 0)),
                  pl.BlockSpec((1, 2, tm, 768), lambda b, i: (b, 0, i, 0)),
                  pl.BlockSpec(bd.shape, lambda b, i: (0, 0)),
                  pl.BlockSpec(eye.shape, lambda b, i: (0, 0)),
                  pl.BlockSpec(dirm.shape, lambda b, i: (0, 0, 0, 0)),
                  pl.BlockSpec(lev.shape, lambda b, i: (0, 0, 0)),
                  pl.BlockSpec(incl1.shape, lambda b, i: (0, 0, 0))],
        out_specs=[mat_spec, mat_spec, seq_spec, seq_spec],
        compiler_params=_cparams(("arbitrary", "arbitrary")),
        name="rwkv_local",
    )(shared, dirs, bd, eye, dirm, lev, incl1)


def _rwkv_scan_kernel(gf_ref, hf_ref, rf_ref, yf_ref, gb_ref, hb_ref, rb_ref, yb_ref,
                      of_ref, ob_ref, st_ref, *, nb):
    C = RWKV_CHUNK
    n = 4 * C
    j = pl.program_id(0)

    @pl.when(j == 0)
    def _():
        st_ref[...] = jnp.zeros(st_ref.shape, F32)

    rr = _iota((n, n), 0)
    cc = _iota((n, n), 1)
    bd = (rr >> 6) == (cc >> 6)
    dirs = ((gf_ref, hf_ref, rf_ref, yf_ref, of_ref), (gb_ref, hb_ref, rb_ref, yb_ref, ob_ref))
    seqs = [(d, b) for b in range(nb) for d in range(2)]

    w_hi, w_mid, l_hi, l_mid = [], [], [], []
    for d, b in seqs:
        g_ref, _, r_ref, _, _ = dirs[d]
        wh, wm, _ = _split3(_expand(st_ref[d, b], bd))
        lh, lm, _ = _split3(jnp.concatenate([r_ref[b, 0], g_ref[b, 0, 0]], axis=0))
        w_hi.append(wh)
        w_mid.append(wm)
        l_hi.append(lh)
        l_mid.append(lm)
    p_hi = [_dot(jnp.concatenate([l_hi[i], l_mid[i]], axis=0), w_hi[i]) for i in range(len(seqs))]
    p_mid = [_dot(l_hi[i], w_mid[i]) for i in range(len(seqs))]
    for i, (d, b) in enumerate(seqs):
        _, h_ref, _, y_ref, o_ref = dirs[d]
        tot = p_hi[i][0:2 * C] + (p_hi[i][2 * C:4 * C] + p_mid[i])
        o_ref[b] = tot[0:C] + y_ref[b, 0]
        st_ref[d, b] = tot[C:2 * C] + h_ref[b, 0, 0]


def _rwkv_scan_call(g, hh, rh, yh, lc):
    nb, _, nch, C, _ = g.shape
    t = nch * C
    ncc = lc // C

    def bwd(j):
        return jnp.where(j < ncc, ncc - 1 - j, nch - 1 + ncc - j)

    def mat_spec(d):
        if d == 0:
            return pl.BlockSpec((nb, 1, 1, C, 256), lambda j: (0, 0, j, 0, 0))
        return pl.BlockSpec((nb, 1, 1, C, 256), lambda j: (0, 1, bwd(j), 0, 0))

    def seq_spec(d):
        if d == 0:
            return pl.BlockSpec((nb, 1, C, 256), lambda j: (0, 0, j, 0))
        return pl.BlockSpec((nb, 1, C, 256), lambda j: (0, 1, bwd(j), 0))

    kern = functools.partial(_rwkv_scan_kernel, nb=nb)
    out = jax.ShapeDtypeStruct((nb, t, 256), F32)
    specs = []
    for d in range(2):
        specs += [mat_spec(d), mat_spec(d), seq_spec(d), seq_spec(d)]
    of, ob = pl.pallas_call(
        kern,
        out_shape=[out, out],
        grid=(nch,),
        in_specs=specs,
        out_specs=[pl.BlockSpec((nb, C, 256), lambda j: (0, j, 0)),
                   pl.BlockSpec((nb, C, 256), lambda j: (0, bwd(j), 0))],
        scratch_shapes=[pltpu.VMEM((2, nb, C, 256), F32)],
        compiler_params=_cparams(("arbitrary",)),
        name="rwkv_scan",
    )(g, hh, rh, yh, g, hh, rh, yh)
    return of, ob


def _merge_kernel(x_ref, mod_ref, oa_ref, oc_ref, od_ref, yf_ref, yb_ref, bon_ref, sg_ref, mg_ref,
                  bw_ref, ow_ref, mb_ref, gnw_ref, gnb_ref, lng_ref, lnb_ref, o_ref,
                  *, lc, tm, nb, row_off, alpha):
    b = pl.program_id(0)
    i = pl.program_id(1)

    def pair_cat(ref):
        return jnp.concatenate([ref[0, 0], ref[0, 1]], axis=1)

    def per_head(fn, x):
        return jnp.concatenate([fn(x[:, 0:LANES]), fn(x[:, LANES:2 * LANES])], axis=1)

    y = yf_ref[0] + yb_ref[0]
    mu = per_head(_half_sums, y) * (1.0 / RWKV_N)
    yc = y - mu
    var = per_head(_half_sums, yc * yc) * (1.0 / RWKV_N)
    ob = yc * lax.rsqrt(var + RWKV_GN_EPS) * gnw_ref[...] + gnb_ref[...] + bon_ref[0]

    outs = (pair_cat(oa_ref), ob, pair_cat(oc_ref), pair_cat(od_ref))
    acc = None
    for k in range(N_BRANCH):
        sg = sg_ref[0, :, k * BRANCH_W:(k + 1) * BRANCH_W].astype(F32)
        u = outs[k] * (sg * jax.nn.sigmoid(sg))
        z = _dot(u.astype(BF16), bw_ref[k])
        gate = jax.nn.sigmoid(mg_ref[0, :, k * D_MODEL:(k + 1) * D_MODEL].astype(F32)
                              + mb_ref[:, k * D_MODEL:(k + 1) * D_MODEL])
        acc = gate * z if acc is None else acc + gate * z
    y2 = _dot(acc.astype(BF16), ow_ref[...])

    row = row_off + i * tm + _iota((tm, D_MODEL), 0)
    gl = mod_ref[pl.ds(b, 1), 2 * D_MODEL:3 * D_MODEL]
    gc = mod_ref[pl.ds(nb, 1), 2 * D_MODEL:3 * D_MODEL]
    h = alpha * x_ref[0] + jnp.where(row < lc, gc, gl) * y2
    m = jnp.mean(h, -1, keepdims=True)
    hc = h - m
    var = jnp.mean(hc * hc, -1, keepdims=True)
    o_ref[0] = hc * lax.rsqrt(var + 1e-5) * lng_ref[...] + lnb_ref[...]


def _merge_call(xx, mod, oa, oc, od, yf, yb, bon, p, bw, ow, mb, gnw, gnb, lng, lnb, *, lc, row_off, alpha):
    nb, t, _ = xx.shape
    nrows = t - row_off
    tm = _pick_tile(math.gcd(nrows, row_off) if row_off else nrows, 256, 16)
    ro = row_off // tm
    kern = functools.partial(_merge_kernel, lc=lc, tm=tm, nb=nb, row_off=row_off, alpha=alpha)

    def full(a):
        return pl.BlockSpec(a.shape, lambda b, i: (0,) * a.ndim)

    pair_spec = pl.BlockSpec((1, 2, tm, LANES), lambda b, i: (b, 0, i, 0))
    return pl.pallas_call(
        kern,
        out_shape=jax.ShapeDtypeStruct((nb, nrows, D_MODEL), F32),
        grid=(nb, nrows // tm),
        in_specs=[
            pl.BlockSpec((1, tm, D_MODEL), lambda b, i: (b, i + ro, 0)),
            full(mod), pair_spec, pair_spec, pair_spec,
            pl.BlockSpec((1, tm, 256), lambda b, i: (b, i + ro, 0)),
            pl.BlockSpec((1, tm, 256), lambda b, i: (b, i + ro, 0)),
            pl.BlockSpec((1, tm, 256), lambda b, i: (b, i + ro, 0)),
            pl.BlockSpec((1, tm, D_MODEL), lambda b, i: (b, i + ro, P_SILU // D_MODEL)),
            pl.BlockSpec((1, tm, N_BRANCH * D_MODEL), lambda b, i: (b, i + ro, P_MERGE // (N_BRANCH * D_MODEL))),
            full(bw), full(ow), full(mb), full(gnw), full(gnb), full(lng), full(lnb),
        ],
        out_specs=pl.BlockSpec((1, tm, D_MODEL), lambda b, i: (b, i, 0)),
        compiler_params=_cparams(("arbitrary", "arbitrary")),
        name="merge",
    )(xx, mod, oa, oc, od, yf, yb, bon, p, p, bw, ow, mb, gnw, gnb, lng, lnb)


def _pack_in_w(w):
    z = lambda n: jnp.zeros((D_MODEL, n), w.dtype)
    a0 = 0
    b0 = 672
    c0 = b0 + 1280
    d0 = c0 + 768
    m0 = d0 + 1024
    cols = [
        w[:, a0:a0 + 256], w[:, a0 + 256:a0 + 384],
        z(64), w[:, a0 + 384:a0 + 416], z(32),
        w[:, c0:c0 + 256], w[:, c0 + 256:c0 + 384], w[:, c0 + 384:c0 + 512],
        w[:, d0:d0 + 256], w[:, d0 + 256:d0 + 512], w[:, d0 + 512:d0 + 768],
        z(256),
        w[:, b0:b0 + 1024],
        w[:, a0 + 416:a0 + 672], w[:, b0 + 1024:b0 + 1280], w[:, c0 + 512:c0 + 768], w[:, d0 + 768:d0 + 1024],
        w[:, m0:m0 + 4096],
    ]
    out = jnp.concatenate(cols, axis=1)
    assert out.shape[1] == P_WIDTH
    return out.astype(BF16)


def _pack_mla(w_uq, w_ukv):
    zq = jnp.zeros((MLA_Q_LORA, LANES - MLA_NOPE - MLA_ROPE), w_uq.dtype)
    zk = jnp.zeros((MLA_KV_LORA, LANES - MLA_NOPE), w_ukv.dtype)
    zv = jnp.zeros((MLA_KV_LORA, MLA_V), w_ukv.dtype)
    qc, kc, vc = [], [], []
    for h in range(MLA_HEADS):
        qc += [w_uq[:, h * 96:(h + 1) * 96], zq]
        kc += [w_ukv[:, h * 128:h * 128 + 64], zk]
        vh = w_ukv[:, h * 128 + 64:(h + 1) * 128]
        vc += [vh, zv] if h % 2 == 0 else [zv, vh]
    cat = lambda xs: jnp.concatenate(xs, axis=1).astype(BF16)
    return cat(qc), cat(kc), cat(vc)


def _rope_tables(row, col, rot_dim, lc, pattern):
    f32 = np.float32
    quarter = rot_dim // 4
    inv_freq = (f32(ROPE_BASE) ** (-np.arange(quarter, dtype=f32) / f32(quarter))).astype(f32)
    ang = np.concatenate([row[:, None] * inv_freq, col[:, None] * inv_freq], axis=-1).astype(f32)
    cos, sin = np.cos(ang).astype(f32), np.sin(ang).astype(f32)
    n = ang.shape[0]
    ones = lambda w: np.ones((n, w), f32)
    zeros = lambda w: np.zeros((n, w), f32)
    if pattern == "mla":
        c = np.concatenate([ones(64), cos, cos, ones(32)], axis=1)
        s = np.concatenate([zeros(64), -sin, sin, zeros(32)], axis=1)
    else:
        reps = LANES // rot_dim
        c = np.concatenate([cos, cos] * reps, axis=1)
        s = np.concatenate([-sin, sin] * reps, axis=1)
    c = np.concatenate([np.ones((lc, LANES), f32), c], axis=0)
    s = np.concatenate([np.zeros((lc, LANES), f32), s], axis=0)
    return jnp.asarray(c), jnp.asarray(s)


def kernel(x, c, ctx, c_ctx, ada_w, ada_b, in_w, mla_q_norm, mla_w_uq, mla_kv_norm, mla_w_ukv, rwkv_mu, rwkv_w0, rwkv_w_up, rwkv_a0, rwkv_a_up, rwkv_k_k, rwkv_k_a, rwkv_r_k, rwkv_gn_w, rwkv_gn_b, gqa_q_norm, gqa_k_norm, diff_lambda, diff_subln, merge_b, branch_w, out_w, ln_g, ln_b):
    nb, ll, _ = x.shape
    lc = ctx.shape[1]
    depth = ada_w.shape[0]
    t = lc + ll
    alpha = (2 * depth) ** 0.25

    rows = ll // GRID_W
    row = np.repeat(np.arange(rows), GRID_W).astype(np.float32)
    col = np.tile(np.arange(GRID_W), rows).astype(np.float32)
    tabs = (_rope_tables(row, col, MLA_ROPE, lc, "mla") + _rope_tables(row, col, HEAD_DIM, lc, "tile")
            + _rope_tables(row, col, DIFF_D, lc, "tile"))

    crows = -(-(nb + 1) // 8) * 8
    cvec = jnp.concatenate([c, c_ctx[None, :], jnp.zeros((crows - nb - 1, D_MODEL), F32)], axis=0)
    mods = _ada_call(cvec, ada_w, ada_b)

    xx = jnp.concatenate([ctx, x], axis=1)
    tile2 = lambda a: jnp.concatenate([a, a])[None, :]
    for l in range(depth):
        last = l == depth - 1
        lam_init = 0.8 - 0.6 * math.exp(-0.3 * l)
        mod = mods[l]
        p = _inproj_call(xx, mod, _pack_in_w(in_w[l]), lc)

        wuq, wk, wv = _pack_mla(mla_w_uq[l], mla_w_ukv[l])
        qa, ka, va, qc, kc, vc, qd, kd, vd = _prep_call(
            p, tabs, (mla_q_norm[l][None, :], wuq, mla_kv_norm[l][None, :], wk, wv,
                      tile2(gqa_q_norm[l]), tile2(gqa_k_norm[l])))
        lam_p = jnp.zeros((8, LANES), F32).at[0:4, 0:DIFF_D].set(diff_lambda[l])
        sub = tile2(diff_subln[l])
        att = lambda q, k, v, **kw: _attn_call(q, k, v, lk=t, q_off=lc, nq_rows=ll, **kw)
        oa = att(qa, ka, va, n_maps=1, shared_k=False)
        oc = att(qc, kc, vc, n_maps=1, shared_k=True)
        od = att(qd, kd, vd, n_maps=2, shared_k=False, extra=(lam_p, sub), lam_init=lam_init)
        if not last:
            catt = lambda q, k, v, **kw: _attn_call(q, k, v, lk=lc, q_off=0, nq_rows=lc, **kw)
            merge_rows = lambda lat_o, ctx_o: jnp.concatenate([ctx_o, lat_o], axis=2)
            oa = merge_rows(oa, catt(qa, ka, va, n_maps=1, shared_k=False))
            oc = merge_rows(oc, catt(qc, kc, vc, n_maps=1, shared_k=True))
            od = merge_rows(od, catt(qd, kd, vd, n_maps=2, shared_k=False, extra=(lam_p, sub), lam_init=lam_init))

        wl = jnp.zeros((256, 1024), F32)
        for d in range(2):
            wl = wl.at[d * 64:(d + 1) * 64, d * 256:(d + 1) * 256].set(rwkv_w_up[l, d])
            wl = wl.at[128 + d * 64:128 + (d + 1) * 64, 512 + d * 256:512 + (d + 1) * 256].set(rwkv_a_up[l, d])
        bias = jnp.concatenate([rwkv_w0[l, 0], rwkv_w0[l, 1], rwkv_a0[l, 0], rwkv_a0[l, 1]])[None, :]
        shared, dirs, bon = _rwkv_prep_call(p, rwkv_mu[l], wl.astype(BF16), bias, rwkv_k_k[l][None, :],
                                            rwkv_k_a[l][None, :], rwkv_r_k[l][None, :], lc)
        g, hh, rh, yh = _rwkv_local_call(shared, dirs)
        yf, yb = _rwkv_scan_call(g, hh, rh, yh, lc)

        xx = _merge_call(xx, mod, oa, oc, od, yf, yb, bon, p,
                         branch_w[l].astype(BF16), out_w[l].astype(BF16), merge_b[l][None, :],
                         rwkv_gn_w[l][None, :], rwkv_gn_b[l][None, :], ln_g[l][None, :], ln_b[l][None, :],
                         lc=lc, row_off=lc if last else 0, alpha=alpha)
    return xx
```

```python
import functools
import math

import jax
import jax.numpy as jnp
import numpy as np
from jax import lax
from jax.experimental import pallas as pl
from jax.experimental.pallas import tpu as pltpu

F32 = jnp.float32
BF16 = jnp.bfloat16

D_MODEL = 1024
GRID_W = 64
ROPE_BASE = 10000.0
HEAD_DIM = 64
N_BRANCH = 4
BRANCH_W = 256
MLA_HEADS = 4
MLA_Q_LORA = 256
MLA_KV_LORA = 128
MLA_NOPE = 64
MLA_ROPE = 32
MLA_V = 64
MLA_SCALE = (MLA_NOPE + MLA_ROPE) ** -0.5
RWKV_N = 64
RWKV_GN_EPS = 64e-5
RWKV_DECAY_SCALE = math.exp(-0.5)
GQA_SCALE = HEAD_DIM ** -0.5
DIFF_D = 32
DIFF_SCALE = DIFF_D ** -0.5
LOG2E = 1.0 / math.log(2.0)

LANES = 128
VMEM_LIMIT = 56 * 1024 * 1024

P_AQ, P_AKV, P_AKR = 0, 256, 384
P_CQ, P_CK, P_CV = 512, 768, 896
P_DQ, P_DK, P_DV = 1024, 1280, 1536
P_RWKV = 2048
P_SILU = 3072
P_MERGE = 4096
P_WIDTH = 8192
ATTN_W = 2048

RWKV_CHUNK = 64
ATTN_TQ = 256
ATTN_MAPS = 32


def _cparams(sem):
    return pltpu.CompilerParams(dimension_semantics=sem, vmem_limit_bytes=VMEM_LIMIT)


def _split3(x):
    h = x.astype(BF16)
    r = x - h.astype(F32)
    m = r.astype(BF16)
    l = (r - m.astype(F32)).astype(BF16)
    return h, m, l


def _dot(a, b):
    return jnp.dot(a, b, preferred_element_type=F32)


def _dot_hi(a, b):
    ah, am, _ = _split3(a)
    bh, bm, _ = _split3(b)
    return _dot(ah, bh) + (_dot(ah, bm) + _dot(am, bh))


def _dot_exact_lhs(a_bf16, b):
    bh, bm, bl = _split3(b)
    return _dot(a_bf16, bh) + (_dot(a_bf16, bm) + _dot(a_bf16, bl))


def _iota(shape, dim):
    return lax.broadcasted_iota(jnp.int32, shape, dim)


def _ada_kernel(c_ref, w_ref, b_ref, o_ref):
    c = c_ref[...]
    s = c * jax.nn.sigmoid(c)
    o_ref[0] = _dot_hi(s, w_ref[0]) + b_ref[0]


def _ada_call(cvec, ada_w, ada_b):
    depth = ada_w.shape[0]
    rows = cvec.shape[0]
    nblk = 3
    return pl.pallas_call(
        _ada_kernel,
        out_shape=jax.ShapeDtypeStruct((depth, rows, 3 * D_MODEL), F32),
        grid=(depth, nblk),
        in_specs=[
            pl.BlockSpec((rows, D_MODEL), lambda l, j: (0, 0)),
            pl.BlockSpec((1, D_MODEL, D_MODEL), lambda l, j: (l, 0, j)),
            pl.BlockSpec((1, 1, D_MODEL), lambda l, j: (l, 0, j)),
        ],
        out_specs=pl.BlockSpec((1, rows, D_MODEL), lambda l, j: (l, 0, j)),
        compiler_params=_cparams(("arbitrary", "arbitrary")),
        name="ada",
    )(cvec, ada_w, ada_b.reshape(depth, 1, 3 * D_MODEL))


def _inproj_kernel(x_ref, mod_ref, w_ref, o_ref, xm_ref, *, lc, tm, nb):
    b = pl.program_id(0)
    i = pl.program_id(1)
    j = pl.program_id(2)

    @pl.when(j == 0)
    def _():
        x = x_ref[0]
        mu = jnp.mean(x, -1, keepdims=True)
        xc = x - mu
        var = jnp.mean(xc * xc, -1, keepdims=True)
        xn = xc * lax.rsqrt(var + 1e-6)
        row = i * tm + _iota((tm, D_MODEL), 0)
        is_ctx = row < lc
        ml = mod_ref[pl.ds(b, 1), :]
        mc = mod_ref[pl.ds(nb, 1), :]
        shift = jnp.where(is_ctx, mc[:, 0:D_MODEL], ml[:, 0:D_MODEL])
        scale = jnp.where(is_ctx, mc[:, D_MODEL:2 * D_MODEL], ml[:, D_MODEL:2 * D_MODEL])
        xm_ref[...] = (xn * (1.0 + scale) + shift).astype(BF16)

    o_ref[0] = _dot(xm_ref[...], w_ref[...]).astype(BF16)


def _inproj_call(xx, mod, w_packed, lc):
    nb, t, _ = xx.shape
    tm = _pick_tile(t, 1088, 16)
    tn = 1024
    kern = functools.partial(_inproj_kernel, lc=lc, tm=tm, nb=nb)
    return pl.pallas_call(
        kern,
        out_shape=jax.ShapeDtypeStruct((nb, t, P_WIDTH), BF16),
        grid=(nb, t // tm, P_WIDTH // tn),
        in_specs=[
            pl.BlockSpec((1, tm, D_MODEL), lambda b, i, j: (b, i, 0)),
            pl.BlockSpec(mod.shape, lambda b, i, j: (0, 0)),
            pl.BlockSpec((D_MODEL, tn), lambda b, i, j: (0, j)),
        ],
        out_specs=pl.BlockSpec((1, tm, tn), lambda b, i, j: (b, i, j)),
        scratch_shapes=[pltpu.VMEM((tm, D_MODEL), BF16)],
        compiler_params=_cparams(("arbitrary", "arbitrary", "arbitrary")),
        name="inproj",
    )(xx, mod, w_packed)


def _pick_tile(n, cap, mult):
    best = None
    for d in range(mult, min(n, cap) + 1, mult):
        if n % d == 0:
            best = d
    assert best is not None, (n, cap, mult)
    return best


def _swap_halves(x, half):
    n = x.shape[-1]
    first = (_iota(x.shape, 1) & (2 * half - 1)) < half
    up = pltpu.roll(x, n - half, 1)
    dn = pltpu.roll(x, half, 1)
    return jnp.where(first, up, dn)


def _half_sums(x):
    lo = _iota(x.shape, 1) < HEAD_DIM
    s_lo = jnp.sum(jnp.where(lo, x, 0.0), -1, keepdims=True)
    s_hi = jnp.sum(jnp.where(lo, 0.0, x), -1, keepdims=True)
    return jnp.where(lo, s_lo, s_hi)


def _prep_kernel(p_ref, cosa_ref, sina_ref, cosc_ref, sinc_ref, cosd_ref, sind_ref,
                 qn_ref, wuq_ref, kvn_ref, wk_ref, wv_ref, gq_ref, gk_ref,
                 qa_ref, ka_ref, va_ref, qc_ref, kc_ref, vc_ref, qd_ref, kd_ref, vd_ref):
    def seg(off, w):
        return p_ref[0, :, off:off + w].astype(F32)

    lane = _iota((p_ref.shape[1], LANES), 1)
    lo = lane < HEAD_DIM

    cosa, sina = cosa_ref[...], sina_ref[...]
    ql = seg(P_AQ, MLA_Q_LORA)
    ql = ql * lax.rsqrt(jnp.mean(ql * ql, -1, keepdims=True) + 1e-6) * qn_ref[...]
    q = _dot(ql.astype(BF16), wuq_ref[...])
    kvl = seg(P_AKV, MLA_KV_LORA)
    kvl = (kvl * lax.rsqrt(jnp.mean(kvl * kvl, -1, keepdims=True) + 1e-6) * kvn_ref[...]).astype(BF16)
    kn = _dot(kvl, wk_ref[...])
    vv = _dot(kvl, wv_ref[...])
    kr = seg(P_AKR, LANES)
    kr = kr * cosa + _swap_halves(kr, MLA_ROPE // 2) * sina
    for h in range(MLA_HEADS):
        qh = q[:, h * LANES:(h + 1) * LANES]
        qh = qh * cosa + _swap_halves(qh, MLA_ROPE // 2) * sina
        qa_ref[0, h] = (qh * (MLA_SCALE * LOG2E)).astype(BF16)
        ka_ref[0, h] = (kn[:, h * LANES:(h + 1) * LANES] + kr).astype(BF16)
        va_ref[0, h] = vv[:, h * LANES:(h + 1) * LANES].T.astype(BF16)

    cosc, sinc = cosc_ref[...], sinc_ref[...]

    def norm_rope(x, g):
        x = x * lax.rsqrt(_half_sums(x * x) * (1.0 / HEAD_DIM) + 1e-6) * g
        return x * cosc + _swap_halves(x, HEAD_DIM // 2) * sinc

    def split_heads(blk):
        return jnp.where(lo, blk, 0.0), jnp.where(lo, pltpu.roll(blk, HEAD_DIM, 1), 0.0)

    for pr in range(2):
        qb = norm_rope(seg(P_CQ + pr * LANES, LANES), gq_ref[...]) * (GQA_SCALE * LOG2E)
        q0, q1 = split_heads(qb)
        qc_ref[0, 2 * pr] = q0.astype(BF16)
        qc_ref[0, 2 * pr + 1] = q1.astype(BF16)
    k0, k1 = split_heads(norm_rope(seg(P_CK, LANES), gk_ref[...]))
    kc_ref[0, 0] = k0.astype(BF16)
    kc_ref[0, 1] = k1.astype(BF16)
    vb = seg(P_CV, LANES)
    v_g0 = jnp.where(lo, vb, 0.0)
    v_g1 = jnp.where(lo, 0.0, vb)
    vc_ref[0, 0] = v_g0.T.astype(BF16)
    vc_ref[0, 1] = pltpu.roll(v_g0, HEAD_DIM, 1).T.astype(BF16)
    vc_ref[0, 2] = pltpu.roll(v_g1, HEAD_DIM, 1).T.astype(BF16)
    vc_ref[0, 3] = v_g1.T.astype(BF16)

    cosd, sind = cosd_ref[...], sind_ref[...]
    piece = lane < DIFF_D
    for pr in range(2):
        qb = seg(P_DQ + pr * LANES, LANES)
        qb = (qb * cosd + _swap_halves(qb, DIFF_D // 2) * sind) * (DIFF_SCALE * LOG2E)
        kb = seg(P_DK + pr * LANES, LANES)
        kb = kb * cosd + _swap_halves(kb, DIFF_D // 2) * sind
        vb = seg(P_DV + pr * LANES, LANES)
        for hh in range(2):
            for m in range(2):
                off = hh * HEAD_DIM + m * DIFF_D
                idx = (2 * pr + hh) * 2 + m
                qs = qb if off == 0 else pltpu.roll(qb, LANES - off, 1)
                ks = kb if off == 0 else pltpu.roll(kb, LANES - off, 1)
                qd_ref[0, idx] = jnp.where(piece, qs, 0.0).astype(BF16)
                kd_ref[0, idx] = jnp.where(piece, ks, 0.0).astype(BF16)
        vd_ref[0, 2 * pr] = jnp.where(lo, vb, 0.0).T.astype(BF16)
        vd_ref[0, 2 * pr + 1] = jnp.where(lo, 0.0, vb).T.astype(BF16)


def _prep_call(p, tabs, wts):
    nb, t, _ = p.shape
    tm = _pick_tile(t, 256, 16)
    row_spec = pl.BlockSpec((tm, LANES), lambda b, i: (i, 0))

    def full(a):
        return pl.BlockSpec(a.shape, lambda b, i: (0,) * a.ndim)

    def head_out(nh):
        return (jax.ShapeDtypeStruct((nb, nh, t, LANES), BF16),
                pl.BlockSpec((1, nh, tm, LANES), lambda b, i: (b, 0, i, 0)))

    def head_out_t(nh):
        return (jax.ShapeDtypeStruct((nb, nh, LANES, t), BF16),
                pl.BlockSpec((1, nh, LANES, tm), lambda b, i: (b, 0, 0, i)))

    outs = [head_out(4), head_out(4), head_out_t(4), head_out(4), head_out(2), head_out_t(4),
            head_out(8), head_out(8), head_out_t(4)]
    return pl.pallas_call(
        _prep_kernel,
        out_shape=[o[0] for o in outs],
        grid=(nb, t // tm),
        in_specs=[pl.BlockSpec((1, tm, ATTN_W), lambda b, i: (b, i, 0))]
        + [row_spec] * 6 + [full(w) for w in wts],
        out_specs=[o[1] for o in outs],
        compiler_params=_cparams(("arbitrary", "arbitrary")),
        name="attn_prep",
    )(p, *tabs, *wts)


def _attn_kernel(*refs, hp, n_maps, qsub, tq, shared_k, lk, tk, lam_init):
    if n_maps == 2:
        q_ref, k_ref, vt_ref, lam_ref, sub_ref, o_ref, s_scr, p_scr = refs
    else:
        q_ref, k_ref, vt_ref, o_ref, s_scr, p_scr = refs
    nchunks = lk // tk
    maps = [(qs, hh * n_maps + mm, hh // 2 if shared_k else hh * n_maps + mm, hh)
            for qs in range(qsub) for hh in range(hp) for mm in range(n_maps)]
    per_sub = hp * n_maps

    def scores(j):
        qs, qi, ki, _ = maps[j]
        s_scr[j % 2] = lax.dot_general(k_ref[0, ki], q_ref[0, qi, qs * tq:(qs + 1) * tq, :],
                                       (((1,), (1,)), ((), ())), preferred_element_type=F32)

    def softmax_rows(j):
        buf = j % 2
        mx = None
        for c in range(nchunks):
            sc = s_scr[buf, c * tk:(c + 1) * tk, :]
            mx = sc if mx is None else jnp.maximum(mx, sc)
        m = jnp.max(mx, 0, keepdims=True)
        ls = None
        for c in range(nchunks):
            p = jnp.exp2(s_scr[buf, c * tk:(c + 1) * tk, :] - m)
            p_scr[buf, c * tk:(c + 1) * tk, :] = p.astype(BF16)
            ls = p if ls is None else ls + p
        return jnp.sum(ls, 0, keepdims=True)

    def weighted_values(j, l):
        ot = _dot(vt_ref[0, maps[j][3]], p_scr[j % 2])
        return (ot / l).T

    outs = []
    scores(0)
    for j in range(len(maps)):
        if j + 1 < len(maps):
            scores(j + 1)
        outs.append(weighted_values(j, softmax_rows(j)))

    for qs in range(qsub):
        for pr in range(hp // 2):
            total = None
            for hh in (2 * pr, 2 * pr + 1):
                o = outs[qs * per_sub + hh * n_maps]
                if n_maps == 2:
                    lp = lam_ref[...]
                    lam = (jnp.exp(jnp.sum(lp[0:1] * lp[1:2], keepdims=True))
                           - jnp.exp(jnp.sum(lp[2:3] * lp[3:4], keepdims=True)) + lam_init)
                    o = o - lam * outs[qs * per_sub + hh * n_maps + 1]
                    ms = jnp.sum(o * o, -1, keepdims=True) * (1.0 / HEAD_DIM)
                    o = o * lax.rsqrt(ms + 1e-5) * sub_ref[...] * (1.0 - lam_init)
                total = o if total is None else total + o
            o_ref[0, pr, qs * tq:(qs + 1) * tq, :] = total


def _attn_call(q, k, v, *, n_maps, shared_k, lk, q_off, nq_rows, extra=(), lam_init=0.0):
    nb = q.shape[0]
    hp = 4
    tq = _pick_tile(nq_rows, ATTN_TQ, 16)
    qsub = max(1, ATTN_MAPS // (hp * n_maps))
    while nq_rows % (qsub * tq):
        qsub //= 2
    tb = qsub * tq
    tk = 256 if lk % 256 == 0 else _pick_tile(lk, 256, LANES)
    kh = hp // 2 if shared_k else hp * n_maps
    kern = functools.partial(_attn_kernel, hp=hp, n_maps=n_maps, qsub=qsub, tq=tq, shared_k=shared_k,
                             lk=lk, tk=tk, lam_init=lam_init)
    in_specs = [
        pl.BlockSpec((pl.Element(1), pl.Element(hp * n_maps), pl.Element(tb), pl.Element(LANES)),
                     lambda b, i: (b, 0, pl.multiple_of(i * tb + q_off, 16), 0)),
        pl.BlockSpec((1, kh, lk, LANES), lambda b, i: (b, 0, 0, 0)),
        pl.BlockSpec((1, hp, LANES, lk), lambda b, i: (b, 0, 0, 0)),
    ] + [pl.BlockSpec(e.shape, lambda b, i: (0, 0)) for e in extra]
    return pl.pallas_call(
        kern,
        out_shape=jax.ShapeDtypeStruct((nb, 2, nq_rows, LANES), F32),
        grid=(nb, nq_rows // tb),
        in_specs=in_specs,
        out_specs=pl.BlockSpec((1, hp // 2, tb, LANES), lambda b, i: (b, 0, i, 0)),
        scratch_shapes=[pltpu.VMEM((2, lk, tq), F32), pltpu.VMEM((2, lk, tq), BF16)],
        compiler_params=_cparams(("arbitrary", "arbitrary")),
        name="attn",
    )(q, k, v, *extra)


def _rwkv_prep_kernel(cur_ref, prv_ref, nxt_ref, mu_ref, wl_ref, bias_ref, kk_ref, ka_ref, rk_ref,
                      sh_ref, dr_ref, bon_ref, *, lc, t_total, tm, halo):
    i = pl.program_id(1)
    cur = cur_ref[0]
    r_i = _iota((tm, tm), 0)
    c_i = _iota((tm, tm), 1)
    s_dn = jnp.where(c_i == r_i - 1, 1.0, 0.0).astype(BF16)
    s_up = jnp.where(c_i == r_i + 1, 1.0, 0.0).astype(BF16)
    row = _iota((tm, D_MODEL), 0)
    tg = i * tm + row
    prv = jnp.where(row == 0, prv_ref[0].astype(F32)[halo - 1:halo, :], _dot(s_dn, cur))
    nxt = jnp.where(row == tm - 1, nxt_ref[0].astype(F32)[0:1, :], _dot(s_up, cur))
    prv = jnp.where((tg == 0) | (tg == lc), 0.0, prv)
    nxt = jnp.where((tg == lc - 1) | (tg == t_total - 1), 0.0, nxt)
    x = cur.astype(F32)
    sh = x + mu_ref[0:1, :] * (prv - x) + mu_ref[1:2, :] * (nxt - x)

    r = sh[:, 0:256]
    k = sh[:, 256:512]
    v = sh[:, 512:768]
    lora = sh[:, 768:1024]
    lora = jnp.where(_iota(lora.shape, 1) < 2 * RWKV_N, jnp.tanh(lora), lora)
    z = _dot(lora.astype(BF16), wl_ref[...]) + bias_ref[...]

    def per_head(fn, x):
        return jnp.concatenate([fn(x[:, 0:LANES]), fn(x[:, LANES:2 * LANES])], axis=1)

    kq = k * kk_ref[...]
    kk = kq * lax.rsqrt(per_head(_half_sums, kq * kq) + 1e-12)
    sh_ref[0, :, 0:256] = r
    sh_ref[0, :, 256:512] = v
    sh_ref[0, :, 512:768] = kk
    hs = None
    for d in range(2):
        logw = -RWKV_DECAY_SCALE * jax.nn.sigmoid(z[:, d * 256:(d + 1) * 256])
        a = jax.nn.sigmoid(z[:, 512 + d * 256:512 + (d + 1) * 256])
        kd = k * (1.0 + (a - 1.0) * ka_ref[...])
        dr_ref[0, d, :, 0:256] = logw
        dr_ref[0, d, :, 256:512] = kd
        dr_ref[0, d, :, 512:768] = a * kk
        s = per_head(_half_sums, r * kd * rk_ref[...])
        hs = s if hs is None else hs + s
    bon_ref[0] = hs * v


def _rwkv_prep_call(p, mu, wl, bias, k_k, k_a, r_k, lc):
    nb, t, _ = p.shape
    tm = _pick_tile(t, 256, 16)
    halo = 16
    cb = P_RWKV // D_MODEL
    hb = tm // halo
    last = t // halo - 1
    kern = functools.partial(_rwkv_prep_kernel, lc=lc, t_total=t, tm=tm, halo=halo)

    def full(a):
        return pl.BlockSpec(a.shape, lambda b, i: (0,) * a.ndim)

    return pl.pallas_call(
        kern,
        out_shape=[jax.ShapeDtypeStruct((nb, t, 768), F32),
                   jax.ShapeDtypeStruct((nb, 2, t, 768), F32),
                   jax.ShapeDtypeStruct((nb, t, 256), F32)],
        grid=(nb, t // tm),
        in_specs=[
            pl.BlockSpec((1, tm, D_MODEL), lambda b, i: (b, i, cb)),
            pl.BlockSpec((1, halo, D_MODEL), lambda b, i: (b, jnp.maximum(i * hb - 1, 0), cb)),
            pl.BlockSpec((1, halo, D_MODEL), lambda b, i: (b, jnp.minimum((i + 1) * hb, last), cb)),
            full(mu), full(wl), full(bias), full(k_k), full(k_a), full(r_k),
        ],
        out_specs=[pl.BlockSpec((1, tm, 768), lambda b, i: (b, i, 0)),
                   pl.BlockSpec((1, 2, tm, 768), lambda b, i: (b, 0, i, 0)),
                   pl.BlockSpec((1, tm, 256), lambda b, i: (b, i, 0))],
        compiler_params=_cparams(("arbitrary", "arbitrary")),
        name="rwkv_prep",
    )(p, p, p, mu, wl, bias, k_k, k_a, r_k)


def _expand(x, bd):
    return jnp.where(bd, jnp.concatenate([x] * 4, axis=0), 0.0)


def _contract(x):
    c = x.shape[0] // 4
    return (x[0:c] + x[c:2 * c]) + (x[2 * c:3 * c] + x[3 * c:4 * c])


def _rwkv_masks():
    C = RWKV_CHUNK
    n = 4 * C
    rr, cc = np.indices((n, n))
    bd = ((rr // C) == (cc // C)).astype(np.float32)
    tr, tc = np.indices((C, n))
    tc = tc % C
    eye = (tr == tc).astype(np.float32)
    dirm = np.stack([np.stack([tc < tr, tc <= tr]), np.stack([tc > tr, tc >= tr])]).astype(np.float32)
    lev = [(tr >> 1) == (tc >> 1)]
    s = 1
    while (1 << s) < C:
        lev.append(((tr >> s) != (tc >> s)) & ((tr >> (s + 1)) == (tc >> (s + 1))))
        s += 1
    r1, c1 = np.indices((C, C))
    incl1 = np.stack([c1 <= r1, c1 >= r1]).astype(np.float32)
    return (jnp.asarray(bd), jnp.asarray(eye), jnp.asarray(dirm),
            jnp.asarray(np.stack(lev).astype(np.float32), BF16), jnp.asarray(incl1, BF16))


def _rwkv_local_kernel(sh_ref, dr_ref, bd_ref, eye_ref, dirm_ref, lev_ref, incl1_ref,
                       g_ref, hh_ref, rh_ref, yh_ref, *, nck):
    C = RWKV_CHUNK
    n = 4 * C
    bd = bd_ref[...]
    eye = eye_ref[...]
    bd_b = bd.astype(BF16)
    nt = (((1,), (1,)), ((), ()))
    tn = (((0,), (0,)), ((), ()))

    def expand_b(x):
        return jnp.concatenate([x.astype(BF16)] * 4, axis=0) * bd_b

    items = [(d, ck) for d in range(2) for ck in range(nck)]
    cks = range(len(items))
    rows = [slice(ck * C, (ck + 1) * C) for _, ck in items]
    strict = [dirm_ref[d, 0] for d, _ in items]
    incl = [dirm_ref[d, 1] for d, _ in items]
    r = [sh_ref[0, rw, 0:256] for rw in rows]
    v = [sh_ref[0, rw, 256:512] for rw in rows]
    kk = [sh_ref[0, rw, 512:768] for rw in rows]
    lw = [dr_ref[0, d, rows[i], 0:256] for i, (d, _) in enumerate(items)]
    kd = [dr_ref[0, d, rows[i], 256:512] for i, (d, _) in enumerate(items)]
    b = [dr_ref[0, d, rows[i], 512:768] for i, (d, _) in enumerate(items)]

    lp = [_dot_exact_lhs(incl1_ref[items[i][0]], lw[i]) for i in cks]
    ltot = [jnp.sum(lw[i], 0, keepdims=True) for i in cks]
    e_neg = [jnp.exp(-lp[i]) for i in cks]
    kap = [kk[i] * jnp.exp(lp[i] - lw[i]) for i in cks]
    rt = [r[i] * jnp.exp(lp[i]) for i in cks]
    bt = [b[i] * e_neg[i] for i in cks]
    kt = [kd[i] * e_neg[i] for i in cks]
    e_rem = [jnp.exp(ltot[i] - lp[i]) for i in cks]
    bh = [b[i] * e_rem[i] for i in cks]
    kh = [kd[i] * e_rem[i] for i in cks]

    lhs = [jnp.concatenate([kap[i], rt[i]], axis=0).astype(BF16) for i in cks]
    gram_b = [lax.dot_general(lhs[i], expand_b(bt[i]), nt, preferred_element_type=F32) for i in cks]
    gram_k = [lax.dot_general(lhs[i], expand_b(kt[i]), nt, preferred_element_type=F32) for i in cks]
    mb_b = [(gram_b[i][0:C] * strict[i]).astype(BF16) for i in cks]
    mrb = [gram_b[i][C:2 * C] * incl[i] for i in cks]
    mk = [gram_k[i][0:C] * strict[i] for i in cks]
    mrk = [gram_k[i][C:2 * C] * incl[i] for i in cks]

    inv = [eye - (mb_b[i] * lev_ref[0]).astype(F32) for i in cks]
    for lv in range(1, lev_ref.shape[0]):
        inv_b = [inv[i].astype(BF16) for i in cks]
        step = [_dot(inv_b[i], expand_b(mb_b[i] * lev_ref[lv])) for i in cks]
        inv = [inv[i] - _dot(step[i].astype(BF16), expand_b(inv_b[i])) for i in cks]

    mv = [_dot(jnp.concatenate([mk[i], mrk[i]], axis=0).astype(BF16), expand_b(v[i])) for i in cks]
    wu_b = [_dot(inv[i].astype(BF16),
                 jnp.concatenate([expand_b(kap[i]), expand_b(mv[i][0:C])], axis=1)).astype(BF16)
            for i in cks]
    mw = [_dot(mrb[i].astype(BF16),
               jnp.concatenate([expand_b(wu_b[i][:, 0:n]), expand_b(wu_b[i][:, n:2 * n])], axis=1))
          for i in cks]
    bw = [lax.dot_general(bh[i].astype(BF16), wu_b[i], tn, preferred_element_type=F32) for i in cks]
    kv = [lax.dot_general(kh[i].astype(BF16), v[i].astype(BF16), tn, preferred_element_type=F32) for i in cks]
    for i, (d, ck) in enumerate(items):
        rh_ref[0, d, rows[i], :] = rt[i] - mw[i][:, 0:n]
        yh_ref[0, d, rows[i], :] = mv[i][C:2 * C] - mw[i][:, n:2 * n]
        g_ref[0, d, ck] = eye * jnp.exp(ltot[i]) - _contract(bw[i][:, 0:n] * bd)
        hh_ref[0, d, ck] = _contract((kv[i] - bw[i][:, n:2 * n]) * bd)


def _rwkv_local_call(shared, dirs):
    nb, t, _ = shared.shape
    C = RWKV_CHUNK
    nch = t // C
    nck = 4 if nch % 4 == 0 else (2 if nch % 2 == 0 else 1)
    tm = nck * C
    n = 4 * C
    bd, eye, dirm, lev, incl1 = _rwkv_masks()
    kern = functools.partial(_rwkv_local_kernel, nck=nck)
    mat = jax.ShapeDtypeStruct((nb, 2, nch, C, 256), F32)
    seq = jax.ShapeDtypeStruct((nb, 2, t, 256), F32)
    mat_spec = pl.BlockSpec((1, 2, nck, C, 256), lambda b, i: (b, 0, i, 0, 0))
    seq_spec = pl.BlockSpec((1, 2, tm, 256), lambda b, i: (b, 0, i, 0))
    return pl.pallas_call(
        kern,
        out_shape=[mat, mat, seq, seq],
        grid=(nb, nch // nck),
        in_specs=[pl.BlockSpec((1, tm, 768), lambda b, i: (b, i, 0)),
                  pl.BlockSpec((1, 2, tm, 768), lambda b, i: (b, 0, i, 0)),
                  pl.BlockSpec(bd.shape, lambda b, i: (0, 0)),
                  pl.BlockSpec(eye.shape, lambda b, i: (0, 0)),
                  pl.BlockSpec(dirm.shape, lambda b, i: (0, 0, 0, 0)),
                  pl.BlockSpec(lev.shape, lambda b, i: (0, 0, 0)),
                  pl.BlockSpec(incl1.shape, lambda b, i: (0, 0, 0))],
        out_specs=[mat_spec, mat_spec, seq_spec, seq_spec],
        compiler_params=_cparams(("arbitrary", "arbitrary")),
        name="rwkv_local",
    )(shared, dirs, bd, eye, dirm, lev, incl1)


def _rwkv_scan_kernel(gf_ref, hf_ref, rf_ref, yf_ref, gb_ref, hb_ref, rb_ref, yb_ref,
                      of_ref, ob_ref, st_ref, *, nb):
    C = RWKV_CHUNK
    n = 4 * C
    j = pl.program_id(0)

    @pl.when(j == 0)
    def _():
        st_ref[...] = jnp.zeros(st_ref.shape, F32)

    rr = _iota((n, n), 0)
    cc = _iota((n, n), 1)
    bd = (rr >> 6) == (cc >> 6)
    dirs = ((gf_ref, hf_ref, rf_ref, yf_ref, of_ref), (gb_ref, hb_ref, rb_ref, yb_ref, ob_ref))
    seqs = [(d, b) for b in range(nb) for d in range(2)]

    w_hi, w_mid, l_hi, l_mid = [], [], [], []
    for d, b in seqs:
        g_ref, _, r_ref, _, _ = dirs[d]
        wh, wm, _ = _split3(_expand(st_ref[d, b], bd))
        lh, lm, _ = _split3(jnp.concatenate([r_ref[b, 0], g_ref[b, 0, 0]], axis=0))
        w_hi.append(wh)
        w_mid.append(wm)
        l_hi.append(lh)
        l_mid.append(lm)
    p_hi = [_dot(jnp.concatenate([l_hi[i], l_mid[i]], axis=0), w_hi[i]) for i in range(len(seqs))]
    p_mid = [_dot(l_hi[i], w_mid[i]) for i in range(len(seqs))]
    for i, (d, b) in enumerate(seqs):
        _, h_ref, _, y_ref, o_ref = dirs[d]
        tot = p_hi[i][0:2 * C] + (p_hi[i][2 * C:4 * C] + p_mid[i])
        o_ref[b] = tot[0:C] + y_ref[b, 0]
        st_ref[d, b] = tot[C:2 * C] + h_ref[b, 0, 0]


def _rwkv_scan_call(g, hh, rh, yh, lc):
    nb, _, nch, C, _ = g.shape
    t = nch * C
    ncc = lc // C

    def bwd(j):
        return jnp.where(j < ncc, ncc - 1 - j, nch - 1 + ncc - j)

    def mat_spec(d):
        if d == 0:
            return pl.BlockSpec((nb, 1, 1, C, 256), lambda j: (0, 0, j, 0, 0))
        return pl.BlockSpec((nb, 1, 1, C, 256), lambda j: (0, 1, bwd(j), 0, 0))

    def seq_spec(d):
        if d == 0:
            return pl.BlockSpec((nb, 1, C, 256), lambda j: (0, 0, j, 0))
        return pl.BlockSpec((nb, 1, C, 256), lambda j: (0, 1, bwd(j), 0))

    kern = functools.partial(_rwkv_scan_kernel, nb=nb)
    out = jax.ShapeDtypeStruct((nb, t, 256), F32)
    specs = []
    for d in range(2):
        specs += [mat_spec(d), mat_spec(d), seq_spec(d), seq_spec(d)]
    of, ob = pl.pallas_call(
        kern,
        out_shape=[out, out],
        grid=(nch,),
        in_specs=specs,
        out_specs=[pl.BlockSpec((nb, C, 256), lambda j: (0, j, 0)),
                   pl.BlockSpec((nb, C, 256), lambda j: (0, bwd(j), 0))],
        scratch_shapes=[pltpu.VMEM((2, nb, C, 256), F32)],
        compiler_params=_cparams(("arbitrary",)),
        name="rwkv_scan",
    )(g, hh, rh, yh, g, hh, rh, yh)
    return of, ob


def _merge_kernel(x_ref, mod_ref, oa_ref, oc_ref, od_ref, yf_ref, yb_ref, bon_ref, sg_ref, mg_ref,
                  bw_ref, ow_ref, mb_ref, gnw_ref, gnb_ref, lng_ref, lnb_ref, o_ref,
                  *, lc, tm, nb, row_off, alpha):
    b = pl.program_id(0)
    i = pl.program_id(1)

    def pair_cat(ref):
        return jnp.concatenate([ref[0, 0], ref[0, 1]], axis=1)

    def per_head(fn, x):
        return jnp.concatenate([fn(x[:, 0:LANES]), fn(x[:, LANES:2 * LANES])], axis=1)

    y = yf_ref[0] + yb_ref[0]
    mu = per_head(_half_sums, y) * (1.0 / RWKV_N)
    yc = y - mu
    var = per_head(_half_sums, yc * yc) * (1.0 / RWKV_N)
    ob = yc * lax.rsqrt(var + RWKV_GN_EPS) * gnw_ref[...] + gnb_ref[...] + bon_ref[0]

    outs = (pair_cat(oa_ref), ob, pair_cat(oc_ref), pair_cat(od_ref))
    acc = None
    for k in range(N_BRANCH):
        sg = sg_ref[0, :, k * BRANCH_W:(k + 1) * BRANCH_W].astype(F32)
        u = outs[k] * (sg * jax.nn.sigmoid(sg))
        z = _dot(u.astype(BF16), bw_ref[k])
        gate = jax.nn.sigmoid(mg_ref[0, :, k * D_MODEL:(k + 1) * D_MODEL].astype(F32)
                              + mb_ref[:, k * D_MODEL:(k + 1) * D_MODEL])
        acc = gate * z if acc is None else acc + gate * z
    y2 = _dot(acc.astype(BF16), ow_ref[...])

    row = row_off + i * tm + _iota((tm, D_MODEL), 0)
    gl = mod_ref[pl.ds(b, 1), 2 * D_MODEL:3 * D_MODEL]
    gc = mod_ref[pl.ds(nb, 1), 2 * D_MODEL:3 * D_MODEL]
    h = alpha * x_ref[0] + jnp.where(row < lc, gc, gl) * y2
    m = jnp.mean(h, -1, keepdims=True)
    hc = h - m
    var = jnp.mean(hc * hc, -1, keepdims=True)
    o_ref[0] = hc * lax.rsqrt(var + 1e-5) * lng_ref[...] + lnb_ref[...]


def _merge_call(xx, mod, oa, oc, od, yf, yb, bon, p, bw, ow, mb, gnw, gnb, lng, lnb, *, lc, row_off, alpha):
    nb, t, _ = xx.shape
    nrows = t - row_off
    tm = _pick_tile(math.gcd(nrows, row_off) if row_off else nrows, 256, 16)
    ro = row_off // tm
    kern = functools.partial(_merge_kernel, lc=lc, tm=tm, nb=nb, row_off=row_off, alpha=alpha)

    def full(a):
        return pl.BlockSpec(a.shape, lambda b, i: (0,) * a.ndim)

    pair_spec = pl.BlockSpec((1, 2, tm, LANES), lambda b, i: (b, 0, i, 0))
    return pl.pallas_call(
        kern,
        out_shape=jax.ShapeDtypeStruct((nb, nrows, D_MODEL), F32),
        grid=(nb, nrows // tm),
        in_specs=[
            pl.BlockSpec((1, tm, D_MODEL), lambda b, i: (b, i + ro, 0)),
            full(mod), pair_spec, pair_spec, pair_spec,
            pl.BlockSpec((1, tm, 256), lambda b, i: (b, i + ro, 0)),
            pl.BlockSpec((1, tm, 256), lambda b, i: (b, i + ro, 0)),
            pl.BlockSpec((1, tm, 256), lambda b, i: (b, i + ro, 0)),
            pl.BlockSpec((1, tm, D_MODEL), lambda b, i: (b, i + ro, P_SILU // D_MODEL)),
            pl.BlockSpec((1, tm, N_BRANCH * D_MODEL), lambda b, i: (b, i + ro, P_MERGE // (N_BRANCH * D_MODEL))),
            full(bw), full(ow), full(mb), full(gnw), full(gnb), full(lng), full(lnb),
        ],
        out_specs=pl.BlockSpec((1, tm, D_MODEL), lambda b, i: (b, i, 0)),
        compiler_params=_cparams(("arbitrary", "arbitrary")),
        name="merge",
    )(xx, mod, oa, oc, od, yf, yb, bon, p, p, bw, ow, mb, gnw, gnb, lng, lnb)


def _pack_in_w(w):
    z = lambda n: jnp.zeros((D_MODEL, n), w.dtype)
    a0 = 0
    b0 = 672
    c0 = b0 + 1280
    d0 = c0 + 768
    m0 = d0 + 1024
    cols = [
        w[:, a0:a0 + 256], w[:, a0 + 256:a0 + 384],
        z(64), w[:, a0 + 384:a0 + 416], z(32),
        w[:, c0:c0 + 256], w[:, c0 + 256:c0 + 384], w[:, c0 + 384:c0 + 512],
        w[:, d0:d0 + 256], w[:, d0 + 256:d0 + 512], w[:, d0 + 512:d0 + 768],
        z(256),
        w[:, b0:b0 + 1024],
        w[:, a0 + 416:a0 + 672], w[:, b0 + 1024:b0 + 1280], w[:, c0 + 512:c0 + 768], w[:, d0 + 768:d0 + 1024],
        w[:, m0:m0 + 4096],
    ]
    out = jnp.concatenate(cols, axis=1)
    assert out.shape[1] == P_WIDTH
    return out.astype(BF16)


def _pack_mla(w_uq, w_ukv):
    zq = jnp.zeros((MLA_Q_LORA, LANES - MLA_NOPE - MLA_ROPE), w_uq.dtype)
    zk = jnp.zeros((MLA_KV_LORA, LANES - MLA_NOPE), w_ukv.dtype)
    zv = jnp.zeros((MLA_KV_LORA, MLA_V), w_ukv.dtype)
    qc, kc, vc = [], [], []
    for h in range(MLA_HEADS):
        qc += [w_uq[:, h * 96:(h + 1) * 96], zq]
        kc += [w_ukv[:, h * 128:h * 128 + 64], zk]
        vh = w_ukv[:, h * 128 + 64:(h + 1) * 128]
        vc += [vh, zv] if h % 2 == 0 else [zv, vh]
    cat = lambda xs: jnp.concatenate(xs, axis=1).astype(BF16)
    return cat(qc), cat(kc), cat(vc)


def _rope_tables(row, col, rot_dim, lc, pattern):
    f32 = np.float32
    quarter = rot_dim // 4
    inv_freq = (f32(ROPE_BASE) ** (-np.arange(quarter, dtype=f32) / f32(quarter))).astype(f32)
    ang = np.concatenate([row[:, None] * inv_freq, col[:, None] * inv_freq], axis=-1).astype(f32)
    cos, sin = np.cos(ang).astype(f32), np.sin(ang).astype(f32)
    n = ang.shape[0]
    ones = lambda w: np.ones((n, w), f32)
    zeros = lambda w: np.zeros((n, w), f32)
    if pattern == "mla":
        c = np.concatenate([ones(64), cos, cos, ones(32)], axis=1)
        s = np.concatenate([zeros(64), -sin, sin, zeros(32)], axis=1)
    else:
        reps = LANES // rot_dim
        c = np.concatenate([cos, cos] * reps, axis=1)
        s = np.concatenate([-sin, sin] * reps, axis=1)
    c = np.concatenate([np.ones((lc, LANES), f32), c], axis=0)
    s = np.concatenate([np.zeros((lc, LANES), f32), s], axis=0)
    return jnp.asarray(c), jnp.asarray(s)


def kernel(x, c, ctx, c_ctx, ada_w, ada_b, in_w, mla_q_norm, mla_w_uq, mla_kv_norm, mla_w_ukv, rwkv_mu, rwkv_w0, rwkv_w_up, rwkv_a0, rwkv_a_up, rwkv_k_k, rwkv_k_a, rwkv_r_k, rwkv_gn_w, rwkv_gn_b, gqa_q_norm, gqa_k_norm, diff_lambda, diff_subln, merge_b, branch_w, out_w, ln_g, ln_b):
    nb, ll, _ = x.shape
    lc = ctx.shape[1]
    depth = ada_w.shape[0]
    t = lc + ll
    alpha = (2 * depth) ** 0.25

    rows = ll // GRID_W
    row = np.repeat(np.arange(rows), GRID_W).astype(np.float32)
    col = np.tile(np.arange(GRID_W), rows).astype(np.float32)
    tabs = (_rope_tables(row, col, MLA_ROPE, lc, "mla") + _rope_tables(row, col, HEAD_DIM, lc, "tile")
            + _rope_tables(row, col, DIFF_D, lc, "tile"))

    crows = -(-(nb + 1) // 8) * 8
    cvec = jnp.concatenate([c, c_ctx[None, :], jnp.zeros((crows - nb - 1, D_MODEL), F32)], axis=0)
    mods = _ada_call(cvec, ada_w, ada_b)

    xx = jnp.concatenate([ctx, x], axis=1)
    tile2 = lambda a: jnp.concatenate([a, a])[None, :]
    for l in range(depth):
        last = l == depth - 1
        lam_init = 0.8 - 0.6 * math.exp(-0.3 * l)
        mod = mods[l]
        p = _inproj_call(xx, mod, _pack_in_w(in_w[l]), lc)

        wuq, wk, wv = _pack_mla(mla_w_uq[l], mla_w_ukv[l])
        qa, ka, va, qc, kc, vc, qd, kd, vd = _prep_call(
            p, tabs, (mla_q_norm[l][None, :], wuq, mla_kv_norm[l][None, :], wk, wv,
                      tile2(gqa_q_norm[l]), tile2(gqa_k_norm[l])))
        lam_p = jnp.zeros((8, LANES), F32).at[0:4, 0:DIFF_D].set(diff_lambda[l])
        sub = tile2(diff_subln[l])
        att = lambda q, k, v, **kw: _attn_call(q, k, v, lk=t, q_off=lc, nq_rows=ll, **kw)
        oa = att(qa, ka, va, n_maps=1, shared_k=False)
        oc = att(qc, kc, vc, n_maps=1, shared_k=True)
        od = att(qd, kd, vd, n_maps=2, shared_k=False, extra=(lam_p, sub), lam_init=lam_init)
        if not last:
            catt = lambda q, k, v, **kw: _attn_call(q, k, v, lk=lc, q_off=0, nq_rows=lc, **kw)
            merge_rows = lambda lat_o, ctx_o: jnp.concatenate([ctx_o, lat_o], axis=2)
            oa = merge_rows(oa, catt(qa, ka, va, n_maps=1, shared_k=False))
            oc = merge_rows(oc, catt(qc, kc, vc, n_maps=1, shared_k=True))
            od = merge_rows(od, catt(qd, kd, vd, n_maps=2, shared_k=False, extra=(lam_p, sub), lam_init=lam_init))

        wl = jnp.zeros((256, 1024), F32)
        for d in range(2):
            wl = wl.at[d * 64:(d + 1) * 64, d * 256:(d + 1) * 256].set(rwkv_w_up[l, d])
            wl = wl.at[128 + d * 64:128 + (d + 1) * 64, 512 + d * 256:512 + (d + 1) * 256].set(rwkv_a_up[l, d])
        bias = jnp.concatenate([rwkv_w0[l, 0], rwkv_w0[l, 1], rwkv_a0[l, 0], rwkv_a0[l, 1]])[None, :]
        shared, dirs, bon = _rwkv_prep_call(p, rwkv_mu[l], wl.astype(BF16), bias, rwkv_k_k[l][None, :],
                                            rwkv_k_a[l][None, :], rwkv_r_k[l][None, :], lc)
        g, hh, rh, yh = _rwkv_local_call(shared, dirs)
        yf, yb = _rwkv_scan_call(g, hh, rh, yh, lc)

        xx = _merge_call(xx, mod, oa, oc, od, yf, yb, bon, p,
                         branch_w[l].astype(BF16), out_w[l].astype(BF16), merge_b[l][None, :],
                         rwkv_gn_w[l][None, :], rwkv_gn_b[l][None, :], ln_g[l][None, :], ln_b[l][None, :],
                         lc=lc, row_off=lc if last else 0, alpha=alpha)
    return xx
```
